```python
import math
import jax, jax.numpy as jnp
from jax import lax
import numpy as np

D_MODEL = 2048
BATCH = 2
SEQ = 4096
DEPTH = 4

GRID_W = 64
CTX_LEN = 256
D_MIX = D_MODEL
H_NA = 8
DH_NA = D_MIX // 2 // H_NA
WIN_R = 8
WIN_C = 16
H_M = 4
DV_M = D_MIX // 2 // H_M
DK_M = DV_M // 2
CHUNK = 128
ROPE_BASE = 10000.0
N_EXPERTS = 8
TOP_K = 2
D_FF = ((8 * D_MODEL // 3 + 255) // 256) * 256
N_DENSE = (DEPTH + 1) // 2
N_MOE = DEPTH // 2
EPS = 1e-6
PROJ_SIZES = [H_NA * DH_NA] * 3 + [H_M * DK_M] * 2 + [H_M * DV_M] * 2 + [4 * H_M]
N_IN = sum(PROJ_SIZES)
PROJ_OFFSETS = [sum(PROJ_SIZES[: i + 1]) for i in range(len(PROJ_SIZES) - 1)]

kernel_name = "hybrid_mlstm_natten_moe_dit"


def rms_norm(t, g):
    tf = t.astype(jnp.float32)
    y = tf * lax.rsqrt(jnp.mean(tf * tf, axis=-1, keepdims=True) + EPS)
    return (y * g.astype(jnp.float32)).astype(t.dtype)


def modulate(h, shift, scale):
    return h * (1 + scale) + shift


def heads(t, n_heads):
    b, l, _ = t.shape
    return t.reshape(b, l, n_heads, -1).transpose(0, 2, 1, 3)


def merge_heads(t):
    b, h, l, d = t.shape
    return t.transpose(0, 2, 1, 3).reshape(b, l, h * d)


def flip(t):
    return jnp.flip(t, axis=2)


def axial_rope(n_tok, dim):
    t = jnp.arange(n_tok)
    row = (t // GRID_W).astype(jnp.float32)
    col = (t % GRID_W).astype(jnp.float32)
    quarter = dim // 4
    inv = ROPE_BASE ** (-jnp.arange(quarter, dtype=jnp.float32) / quarter)
    ang = jnp.concatenate([row[:, None] * inv, col[:, None] * inv], axis=-1)
    return jnp.cos(ang), jnp.sin(ang)


def apply_rope(t, cos, sin):
    te, to = t[..., 0::2], t[..., 1::2]
    cos = cos.astype(t.dtype)
    sin = sin.astype(t.dtype)
    return jnp.stack([te * cos - to * sin, te * sin + to * cos], axis=-1).reshape(t.shape)


def mlstm_scan(q, k, v, logi, logf, state, with_h=True):
    b_, h_, l_, _ = q.shape
    nc = l_ // CHUNK

    def chunks(a):
        a = a.reshape(b_, h_, nc, CHUNK, *a.shape[3:])
        return jnp.moveaxis(a, 2, 0)

    causal = jnp.tril(jnp.ones((CHUNK, CHUNK), dtype=bool))

    def step(carry, xs):
        C, n, m = carry
        qb, kb, vb, ib, fb = xs
        bcum = jnp.cumsum(fb, axis=-1)
        b_last = bcum[..., -1]
        d_end = b_last[..., None] - bcum + ib
        m_new = jnp.maximum(b_last + m, jnp.max(d_end, axis=-1))
        w_end = jnp.exp(d_end - m_new[..., None])
        decay = jnp.exp(b_last + m - m_new)
        C_new = decay[..., None, None] * C + jnp.einsum("bhs,bhsd,bhsv->bhdv", w_end, kb, vb)
        n_new = decay[..., None] * n + jnp.einsum("bhs,bhsd->bhd", w_end, kb)
        if not with_h:
            return (C_new, n_new, m_new), None
        d_mat = jnp.where(causal, bcum[..., :, None] - bcum[..., None, :] + ib[..., None, :], -jnp.inf)
        m_in = bcum + m[..., None]
        m_t = jnp.maximum(m_in, jnp.max(d_mat, axis=-1))
        s = jnp.einsum("bhtd,bhsd->bhts", qb, kb) * jnp.exp(d_mat - m_t[..., None])
        a_in = jnp.exp(m_in - m_t)
        num = jnp.einsum("bhts,bhsv->bhtv", s, vb) + a_in[..., None] * jnp.einsum("bhtd,bhdv->bhtv", qb, C)
        den = jnp.sum(s, axis=-1) + a_in * jnp.einsum("bhtd,bhd->bht", qb, n)
        h = num / jnp.maximum(jnp.abs(den), jnp.exp(-m_t))[..., None]
        return (C_new, n_new, m_new), h

    state, h = lax.scan(step, state, tuple(chunks(a) for a in (q, k, v, logi, logf)))
    if not with_h:
        return None, state
    h = jnp.moveaxis(h, 0, 2).reshape(b_, h_, l_, -1)
    return h, state


def mlstm_inputs(p, gate_b, rope):
    q = heads(p[3], H_M).astype(jnp.float32)
    k = heads(p[4], H_M).astype(jnp.float32)
    if rope is not None:
        q = apply_rope(q, rope[0], rope[1])
        k = apply_rope(k, rope[0], rope[1])
    k = k * (DK_M ** -0.5)
    v = heads(p[5], H_M).astype(jnp.float32)
    b, l, _ = p[7].shape
    g = (p[7] + gate_b).astype(jnp.float32).reshape(b, l, 4, H_M).transpose(2, 0, 3, 1)
    gates = (g[0], jax.nn.log_sigmoid(g[1]), g[2], jax.nn.log_sigmoid(g[3]))
    return q, k, v, p[6], gates


def mlstm_output(h, o, m_norm_g):
    h = rms_norm(h, m_norm_g.reshape(H_M, 1, DV_M))
    return merge_heads(h).astype(o.dtype) * jax.nn.sigmoid(o)


def zero_state(b):
    return (jnp.zeros((b, H_M, DK_M, DV_M), jnp.float32),
            jnp.zeros((b, H_M, DK_M), jnp.float32),
            jnp.zeros((b, H_M), jnp.float32))


def neighbourhood_attention(q, k, v, kc, vc, rpb, rows):
    b, h, l, dh = q.shape
    wr = min(WIN_R, rows)
    scale = dh ** -0.5
    r_idx = jnp.arange(rows)
    row_start = jnp.clip(r_idx - wr // 2, 0, rows - wr)
    band_rows = row_start[:, None] + jnp.arange(wr)[None, :]
    q_g = q.reshape(b, h, rows, GRID_W, dh)
    k_band = k.reshape(b, h, rows, GRID_W, dh)[:, :, band_rows].reshape(b, h, rows, wr * GRID_W, dh)
    v_band = v.reshape(b, h, rows, GRID_W, dh)[:, :, band_rows].reshape(b, h, rows, wr * GRID_W, dh)
    c_idx = jnp.arange(GRID_W)
    col_start = jnp.clip(c_idx - WIN_C // 2, 0, GRID_W - WIN_C)
    col_ok = (c_idx[None, :] >= col_start[:, None]) & (c_idx[None, :] < col_start[:, None] + WIN_C)
    mask = jnp.broadcast_to(col_ok[:, None, :], (GRID_W, wr, GRID_W)).reshape(GRID_W, wr * GRID_W)
    dr = band_rows - r_idx[:, None] + (WIN_R - 1)
    dc = jnp.clip(c_idx[None, :] - c_idx[:, None], -(WIN_C - 1), WIN_C - 1) + (WIN_C - 1)
    bias = rpb[:, dr[:, None, :, None], dc[None, :, None, :]]
    bias = bias.reshape(h, rows, GRID_W, wr * GRID_W).astype(jnp.float32)
    s_band = jnp.einsum("bhrqd,bhrkd->bhrqk", q_g, k_band).astype(jnp.float32) * scale + bias[None]
    s_band = jnp.where(mask, s_band, -jnp.inf)
    s_ctx = jnp.einsum("bhrqd,bhcd->bhrqc", q_g, kc).astype(jnp.float32) * scale
    p = jax.nn.softmax(jnp.concatenate([s_band, s_ctx], axis=-1), axis=-1).astype(v.dtype)
    p_band, p_ctx = p[..., : wr * GRID_W], p[..., wr * GRID_W:]
    out = jnp.einsum("bhrqk,bhrkd->bhrqd", p_band, v_band) + jnp.einsum("bhrqc,bhcd->bhrqd", p_ctx, vc)
    return merge_heads(out.reshape(b, h, l, dh))


def context_attention(q, k, v):
    s = jnp.einsum("bhqd,bhkd->bhqk", q, k).astype(jnp.float32) * (q.shape[-1] ** -0.5)
    p = jax.nn.softmax(s, axis=-1).astype(v.dtype)
    return merge_heads(jnp.einsum("bhqk,bhkd->bhqd", p, v))


def token_mixer(hx, hc, rope, w_in, w_out, gate_b, na_q_g, na_k_g, na_rpb, m_norm_g, with_ctx_out):
    b, l, _ = hx.shape
    rows = l // GRID_W
    px = jnp.split(hx @ w_in, PROJ_OFFSETS, axis=-1)
    pc = jnp.split(hc @ w_in, PROJ_OFFSETS, axis=-1)
    qx = rms_norm(heads(px[0], H_NA), na_q_g)
    kx = rms_norm(heads(px[1], H_NA), na_k_g)
    vx = heads(px[2], H_NA)
    kc = rms_norm(heads(pc[1], H_NA), na_k_g)
    vc = heads(pc[2], H_NA)
    na_x = neighbourhood_attention(qx, kx, vx, kc, vc, na_rpb, rows)
    qmc, kmc, vmc, omc, gc = mlstm_inputs(pc, gate_b, None)
    qmx, kmx, vmx, omx, gx = mlstm_inputs(px, gate_b, rope)
    z = zero_state(b)
    hcf, st_f = mlstm_scan(qmc, kmc, vmc, gc[0], gc[1], z, with_h=with_ctx_out)
    hcb, st_b = mlstm_scan(flip(qmc), flip(kmc), flip(vmc), flip(gc[2]), flip(gc[3]), z, with_h=with_ctx_out)
    hxf, _ = mlstm_scan(qmx, kmx, vmx, gx[0], gx[1], st_f)
    hxb, _ = mlstm_scan(flip(qmx), flip(kmx), flip(vmx), flip(gx[2]), flip(gx[3]), st_b)
    m_x = mlstm_output(hxf + flip(hxb), omx, m_norm_g)
    out_x = jnp.concatenate([na_x, m_x], axis=-1) @ w_out
    if not with_ctx_out:
        return out_x, None
    qc = rms_norm(heads(pc[0], H_NA), na_q_g)
    na_c = context_attention(qc, kc, vc)
    m_c = mlstm_output(hcf + flip(hcb), omc, m_norm_g)
    out_c = jnp.concatenate([na_c, m_c], axis=-1) @ w_out
    return out_x, out_c


def swiglu(h, wg, wu, wd):
    return (jax.nn.silu(h @ wg) * (h @ wu)) @ wd


def moe_swiglu(h, router, wg, wu, wd):
    logits = (h @ router).astype(jnp.float32)
    top_val, top_idx = lax.top_k(logits, TOP_K)
    top_w = jax.nn.softmax(top_val, axis=-1)
    gates = jnp.sum(jax.nn.one_hot(top_idx, N_EXPERTS, dtype=jnp.float32) * top_w[..., None], axis=-2)
    out = jnp.zeros_like(h)
    for e in range(N_EXPERTS):
        out = out + gates[..., e:e + 1].astype(h.dtype) * swiglu(h, wg[e], wu[e], wd[e])
    return out


def setup_inputs(seed: int = 0) -> dict:
    key = jax.random.key(seed)
    ks = jax.random.split(key, 24)
    f32 = jnp.float32
    nrm = lambda k, shape, s: jax.random.normal(k, shape, f32) * s
    d = D_MODEL
    gate_b = jnp.concatenate([
        nrm(ks[9], (DEPTH, H_M), 0.1),
        3.0 + 3.0 * jax.random.uniform(ks[10], (DEPTH, H_M), f32),
        nrm(ks[11], (DEPTH, H_M), 0.1),
        3.0 + 3.0 * jax.random.uniform(ks[12], (DEPTH, H_M), f32),
    ], axis=-1)
    return {
        "x": nrm(ks[0], (BATCH, SEQ, d), 1.0),
        "c": nrm(ks[1], (BATCH, d), 1.0),
        "ctx": nrm(ks[2], (BATCH, CTX_LEN, d), 1.0),
        "c_ctx": nrm(ks[3], (d,), 1.0),
        "ada_w": nrm(ks[4], (DEPTH, d, 6 * d), 0.5 * d ** -0.5),
        "ada_b": nrm(ks[5], (DEPTH, 6 * d), 0.02),
        "norm1_g": 1.0 + nrm(ks[6], (DEPTH, d), 0.1),
        "norm2_g": 1.0 + nrm(ks[7], (DEPTH, d), 0.1),
        "w_in": nrm(ks[8], (DEPTH, d, N_IN), d ** -0.5),
        "gate_b": gate_b,
        "na_q_g": 1.0 + nrm(ks[13], (DEPTH, DH_NA), 0.1),
        "na_k_g": 1.0 + nrm(ks[14], (DEPTH, DH_NA), 0.1),
        "na_rpb": nrm(ks[15], (DEPTH, H_NA, 2 * WIN_R - 1, 2 * WIN_C - 1), 0.1),
        "m_norm_g": 1.0 + nrm(ks[16], (DEPTH, H_M * DV_M), 0.1),
        "w_out": nrm(ks[17], (DEPTH, D_MIX, d), D_MIX ** -0.5),
        "ffn_w_gate": nrm(ks[18], (N_DENSE, d, D_FF), d ** -0.5),
        "ffn_w_up": nrm(ks[19], (N_DENSE, d, D_FF), d ** -0.5),
        "ffn_w_down": nrm(ks[20], (N_DENSE, D_FF, d), D_FF ** -0.5),
        "moe_router": nrm(ks[21], (N_MOE, d, N_EXPERTS), d ** -0.5),
        "moe_w_gate": nrm(ks[22], (N_MOE, N_EXPERTS, d, D_FF), d ** -0.5),
        "moe_w_up": nrm(jax.random.fold_in(ks[22], 1), (N_MOE, N_EXPERTS, d, D_FF), d ** -0.5),
        "moe_w_down": nrm(ks[23], (N_MOE, N_EXPERTS, D_FF, d), D_FF ** -0.5),
    }


def reference(x, c, ctx, c_ctx, ada_w, ada_b, norm1_g, norm2_g, w_in, gate_b, na_q_g, na_k_g,
              na_rpb, m_norm_g, w_out, ffn_w_gate, ffn_w_up, ffn_w_down, moe_router,
              moe_w_gate, moe_w_up, moe_w_down):
    L = x.shape[1]
    rope = axial_rope(L, DK_M)
    cond_x = jax.nn.silu(c)
    cond_c = jax.nn.silu(c_ctx)[None]
    for layer in range(DEPTH):
        last = layer == DEPTH - 1
        mod_x = jnp.split((cond_x @ ada_w[layer] + ada_b[layer])[:, None, :], 6, axis=-1)
        mod_c = jnp.split((cond_c @ ada_w[layer] + ada_b[layer])[:, None, :], 6, axis=-1)
        hx = modulate(rms_norm(x, norm1_g[layer]), mod_x[0], mod_x[1])
        hc = modulate(rms_norm(ctx, norm1_g[layer]), mod_c[0], mod_c[1])
        mix_x, mix_c = token_mixer(hx, hc, rope, w_in[layer], w_out[layer], gate_b[layer],
                                   na_q_g[layer], na_k_g[layer], na_rpb[layer], m_norm_g[layer],
                                   with_ctx_out=not last)
        x = x + mod_x[2] * mix_x
        if not last:
            ctx = ctx + mod_c[2] * mix_c
        h2 = modulate(rms_norm(x, norm2_g[layer]), mod_x[3], mod_x[4])
        if not last:
            hc2 = modulate(rms_norm(ctx, norm2_g[layer]), mod_c[3], mod_c[4])
            h2 = jnp.concatenate([h2, hc2], axis=1)
        idx = layer // 2
        if layer % 2 == 0:
            f = swiglu(h2, ffn_w_gate[idx], ffn_w_up[idx], ffn_w_down[idx])
        else:
            f = moe_swiglu(h2, moe_router[idx], moe_w_gate[idx], moe_w_up[idx], moe_w_down[idx])
        x = x + mod_x[5] * f[:, :L]
        if not last:
            ctx = ctx + mod_c[5] * f[:, L:]
    return x
```

```python
import functools
import math
from typing import NamedTuple

import jax
import jax.numpy as jnp
import numpy as np
from jax import lax
from jax.experimental import pallas as pl
from jax.experimental.pallas import tpu as pltpu

F32 = jnp.float32
BF16 = jnp.bfloat16
I32 = jnp.int32

GRID_W = 64
H_NA = 8
DH_NA = 128
WIN_R = 8
WIN_C = 16
H_M = 4
DK_M = 128
DV_M = 256
CHUNK = 128
ROPE_BASE = 10000.0
N_EXPERTS = 8
TOP_K = 2
EPS = 1e-6
D_MIX = 2048
NEG = -1e30

OFF_Q, OFF_K, OFF_V = 0, 1024, 2048
OFF_MQ, OFF_MK, OFF_MV, OFF_MO = 3072, 3584, 4096, 5120
N_MAIN = 6144
DV_EXT = DV_M + 128

LANES = 128
VMEM_LIMIT = 56 * 1024 * 1024
MOD_ROWS = 16


class Dims(NamedTuple):
    batch: int
    seq: int
    ctx: int
    d_model: int
    d_ff: int
    depth: int

    @property
    def m_x(self):
        return self.batch * self.seq

    @property
    def m_all(self):
        return self.batch * (self.seq + self.ctx)


def _cparams(sem):
    return pltpu.CompilerParams(dimension_semantics=sem, vmem_limit_bytes=VMEM_LIMIT)


def _group_select(row, vals, seq):
    out = vals[-1]
    for b in reversed(range(len(vals) - 1)):
        out = jnp.where(row < (b + 1) * seq, vals[b], out)
    return out


def _pick_tile(m, candidates):
    for t in candidates:
        if m % t == 0:
            return t
    raise ValueError(f"no row tile for {m}")


def _gmm_kernel(grp_ref, nv_ref, a_ref, w_ref, *refs, epilogue, tm, seq, n_groups):
    if epilogue == "residual":
        res_ref, gate_ref, o_ref, wbf_ref = refs
    elif epilogue == "rowscale":
        rs_ref, o_ref, wbf_ref = refs
    elif epilogue == "bias":
        b_ref, o_ref, wbf_ref = refs
    else:
        o_ref, wbf_ref = refs
    i = pl.program_id(1)
    g = grp_ref[i]
    g_prev = grp_ref[jnp.maximum(i - 1, 0)]

    @pl.when((i == 0) | (g != g_prev))
    def _():
        wbf_ref[...] = w_ref[0].astype(BF16)

    @pl.when(i < nv_ref[0])
    def _():
        acc = jnp.dot(a_ref[...], wbf_ref[...], preferred_element_type=F32)
        if epilogue == "residual":
            row = i * tm + lax.broadcasted_iota(I32, (tm, 1), 0)
            gate = _group_select(row, [gate_ref[k:k + 1, :] for k in range(n_groups)], seq)
            acc = res_ref[...] + gate * acc
        elif epilogue == "rowscale":
            acc = acc * rs_ref[:, 0:1]
        elif epilogue == "bias":
            acc = acc + b_ref[0]
        o_ref[...] = acc.astype(o_ref.dtype)

    @pl.when(i >= nv_ref[0])
    def _():
        o_ref[...] = jnp.zeros(o_ref.shape, o_ref.dtype)


def gmm(a, w, grp, nvalid, *, tm, tn, n_out, out_dtype, epilogue=None, extras=(), extra_specs=(),
        a_fixed=False, seq=0, n_groups=0, name="gmm"):
    k = a.shape[1]
    nt = grp.shape[0]
    m_out = nt * tm
    a_map = (lambda j, i, g, nv: (0, 0)) if a_fixed else (lambda j, i, g, nv: (i, 0))
    in_specs = [
        pl.BlockSpec((tm, k), a_map),
        pl.BlockSpec((1, k, tn), lambda j, i, g, nv: (g[i], 0, j)),
    ] + list(extra_specs)
    kern = functools.partial(_gmm_kernel, epilogue=epilogue, tm=tm, seq=seq, n_groups=n_groups)
    return pl.pallas_call(
        kern,
        out_shape=jax.ShapeDtypeStruct((m_out, n_out), out_dtype),
        grid_spec=pltpu.PrefetchScalarGridSpec(
            num_scalar_prefetch=2,
            grid=(n_out // tn, nt),
            in_specs=in_specs,
            out_specs=pl.BlockSpec((tm, tn), lambda j, i, g, nv: (i, j)),
            scratch_shapes=[pltpu.VMEM((k, tn), BF16)],
        ),
        compiler_params=_cparams(("arbitrary", "arbitrary")),
        name=name,
    )(grp, nvalid, a, w, *extras)


def _gmm_swiglu_kernel(grp_ref, nv_ref, a_ref, wg_ref, wu_ref, o_ref, wgbf_ref, wubf_ref):
    i = pl.program_id(1)
    g = grp_ref[i]
    g_prev = grp_ref[jnp.maximum(i - 1, 0)]

    @pl.when((i == 0) | (g != g_prev))
    def _():
        wgbf_ref[...] = wg_ref[0].astype(BF16)
        wubf_ref[...] = wu_ref[0].astype(BF16)

    @pl.when(i < nv_ref[0])
    def _():
        a = a_ref[...]
        gate = jnp.dot(a, wgbf_ref[...], preferred_element_type=F32)
        up = jnp.dot(a, wubf_ref[...], preferred_element_type=F32)
        o_ref[...] = (gate * jax.nn.sigmoid(gate) * up).astype(o_ref.dtype)

    @pl.when(i >= nv_ref[0])
    def _():
        o_ref[...] = jnp.zeros(o_ref.shape, o_ref.dtype)


def gmm_swiglu(a, wg, wu, grp, nvalid, *, tm, tn, name="gmm_swiglu"):
    k = a.shape[1]
    n = wg.shape[2]
    nt = grp.shape[0]
    wspec = pl.BlockSpec((1, k, tn), lambda j, i, g, nv: (g[i], 0, j))
    return pl.pallas_call(
        _gmm_swiglu_kernel,
        out_shape=jax.ShapeDtypeStruct((nt * tm, n), BF16),
        grid_spec=pltpu.PrefetchScalarGridSpec(
            num_scalar_prefetch=2,
            grid=(n // tn, nt),
            in_specs=[pl.BlockSpec((tm, k), lambda j, i, g, nv: (i, 0)), wspec, wspec],
            out_specs=pl.BlockSpec((tm, tn), lambda j, i, g, nv: (i, j)),
            scratch_shapes=[pltpu.VMEM((k, tn), BF16), pltpu.VMEM((k, tn), BF16)],
        ),
        compiler_params=_cparams(("arbitrary", "arbitrary")),
        name=name,
    )(grp, nvalid, a, wg, wu)


def _rms(x, g):
    return x * lax.rsqrt(jnp.mean(x * x, axis=-1, keepdims=True) + EPS) * g


def _split_bf16(x):
    hi = x.astype(BF16)
    lo = (x - hi.astype(F32)).astype(BF16)
    return hi, lo


def _norm_mod_kernel(x_ref, g_ref, shift_ref, scale_ref, *refs, tm, seq, n_groups, with_router):
    if with_router:
        r_ref, o_ref, route_ref = refs
    else:
        (o_ref,) = refs
    i = pl.program_id(0)
    row0 = i * tm
    grp = jnp.int32(n_groups - 1)
    for b in reversed(range(n_groups - 1)):
        grp = jnp.where(row0 < (b + 1) * seq, b, grp)
    y = _rms(x_ref[...], g_ref[...])
    h = y * (1.0 + scale_ref[pl.ds(grp, 1), :]) + shift_ref[pl.ds(grp, 1), :]
    o_ref[...] = h.astype(o_ref.dtype)
    if with_router:
        nt_dims = (((1,), (1,)), ((), ()))
        h_hi, h_lo = _split_bf16(h)
        r_hi, r_lo = _split_bf16(r_ref[...])
        logits = (lax.dot_general(r_hi, h_hi, nt_dims, preferred_element_type=F32)
                  + lax.dot_general(r_hi, h_lo, nt_dims, preferred_element_type=F32)
                  + lax.dot_general(r_lo, h_hi, nt_dims, preferred_element_type=F32))
        ids = lax.broadcasted_iota(I32, logits.shape, 0)
        m1 = jnp.max(logits, axis=0, keepdims=True)
        i1 = jnp.min(jnp.where(logits == m1, ids, N_EXPERTS), axis=0, keepdims=True)
        rest = jnp.where(ids == i1, -jnp.inf, logits)
        m2 = jnp.max(rest, axis=0, keepdims=True)
        i2 = jnp.min(jnp.where(rest == m2, ids, N_EXPERTS), axis=0, keepdims=True)
        e = jnp.exp(m2 - m1)
        w1 = 1.0 / (1.0 + e)
        w2 = e / (1.0 + e)
        out_row = lax.broadcasted_iota(I32, (8, tm), 0)
        route_ref[...] = jnp.where(out_row == 0, i1.astype(F32), jnp.where(
            out_row == 1, i2.astype(F32), jnp.where(out_row == 2, w1, jnp.where(out_row == 3, w2, 0.0))))


def norm_mod(xs, g, mod, layer, shift_chunk, scale_chunk, dims, *, m, router_t=None):
    d = dims.d_model
    tm = _pick_tile(math.gcd(dims.seq, dims.ctx), (256, 128))
    with_router = router_t is not None
    kern = functools.partial(_norm_mod_kernel, tm=tm, seq=dims.seq, n_groups=dims.batch + 1,
                             with_router=with_router)
    in_specs = [
        pl.BlockSpec((tm, d), lambda i: (i, 0)),
        pl.BlockSpec((1, d), lambda i: (0, 0)),
        pl.BlockSpec((8, d), lambda i: (MOD_ROWS // 8 * layer, shift_chunk)),
        pl.BlockSpec((8, d), lambda i: (MOD_ROWS // 8 * layer, scale_chunk)),
    ]
    args = [xs, g.reshape(1, d), mod, mod]
    out_shape = [jax.ShapeDtypeStruct((m, d), BF16)]
    out_specs = [pl.BlockSpec((tm, d), lambda i: (i, 0))]
    if with_router:
        in_specs.append(pl.BlockSpec((N_EXPERTS, d), lambda i: (0, 0)))
        args.append(router_t)
        out_shape.append(jax.ShapeDtypeStruct((8, m), F32))
        out_specs.append(pl.BlockSpec((8, tm), lambda i: (0, i)))
    res = pl.pallas_call(
        kern, out_shape=out_shape, grid=(m // tm,), in_specs=in_specs, out_specs=out_specs,
        compiler_params=_cparams(("arbitrary",)), name="norm_mod_router" if with_router else "norm_mod",
    )(*args)
    return res if with_router else res[0]


def _qkv_prep_kernel(qk_ref, v_ref, qg_ref, kg_ref, qk_out, v_out):
    for h in range(2 * H_NA):
        g = qg_ref[...] if h < H_NA else kg_ref[...]
        sl = slice(h * DH_NA, (h + 1) * DH_NA)
        qk_out[:, sl] = _rms(qk_ref[:, sl], g).astype(BF16)
    v_out[...] = v_ref[...].astype(BF16)


def qkv_prep(p, qg, kg, dims):
    m = dims.m_all
    tm = 256
    return pl.pallas_call(
        _qkv_prep_kernel,
        out_shape=[jax.ShapeDtypeStruct((m, 2 * H_NA * DH_NA), BF16),
                   jax.ShapeDtypeStruct((m, H_NA * DH_NA), BF16)],
        grid=(m // tm,),
        in_specs=[pl.BlockSpec((tm, 2048), lambda i: (i, 0)),
                  pl.BlockSpec((tm, 1024), lambda i: (i, OFF_V // 1024)),
                  pl.BlockSpec((1, DH_NA), lambda i: (0, 0)),
                  pl.BlockSpec((1, DH_NA), lambda i: (0, 0))],
        out_specs=[pl.BlockSpec((tm, 2048), lambda i: (i, 0)),
                   pl.BlockSpec((tm, 1024), lambda i: (i, 0))],
        compiler_params=_cparams(("arbitrary",)), name="qkv_prep",
    )(p, p, qg.reshape(1, DH_NA), kg.reshape(1, DH_NA))


NT_DIMS = (((1,), (1,)), ((), ()))
NA_ROWS_PER_STEP = 8
N_BIAS_PATTERNS = 8


def _na_kernel(q_ref, k_ref, v_ref, kc_ref, vc_ref, bias_ref, o_ref, *, rows):
    qb = pl.program_id(2)
    scale = DH_NA ** -0.5
    wr = min(WIN_R, rows)
    kc = kc_ref[...]
    vc = vc_ref[...]
    for rr in range(NA_ROWS_PER_STEP):
        r = qb * NA_ROWS_PER_STEP + rr
        rs = jnp.clip(r - wr // 2, 0, rows - wr)
        pid = jnp.where(r < 4, r, jnp.where(r > rows - 4, r - (rows - 8), 4))
        start = pl.multiple_of(rs * GRID_W, GRID_W)
        q = q_ref[rr * GRID_W:(rr + 1) * GRID_W, :]
        kb = k_ref[pl.ds(start, wr * GRID_W), :]
        vb = v_ref[pl.ds(start, wr * GRID_W), :]
        s = lax.dot_general(q, kb, NT_DIMS, preferred_element_type=F32) * scale + bias_ref[0, pid]
        sc = lax.dot_general(q, kc, NT_DIMS, preferred_element_type=F32) * scale
        m = jnp.maximum(jnp.max(s, axis=-1, keepdims=True), jnp.max(sc, axis=-1, keepdims=True))
        pb = jnp.exp(s - m)
        pc = jnp.exp(sc - m)
        den = jnp.sum(pb, axis=-1, keepdims=True) + jnp.sum(pc, axis=-1, keepdims=True)
        o = (jnp.dot(pb.astype(BF16), vb, preferred_element_type=F32)
             + jnp.dot(pc.astype(BF16), vc, preferred_element_type=F32))
        o_ref[rr * GRID_W:(rr + 1) * GRID_W, :] = (o / den).astype(o_ref.dtype)


def na_bias_table(rpb, rows):
    wr = min(WIN_R, rows)
    r = jnp.array([0, 1, 2, 3, rows // 2, rows - 3, rows - 2, rows - 1])
    rs = jnp.clip(r - wr // 2, 0, rows - wr)
    dr = rs[:, None] + jnp.arange(wr)[None, :] - r[:, None] + (WIN_R - 1)
    c = jnp.arange(GRID_W)
    dc = jnp.clip(c[None, :] - c[:, None], -(WIN_C - 1), WIN_C - 1) + (WIN_C - 1)
    cs = jnp.clip(c - WIN_C // 2, 0, GRID_W - WIN_C)
    ok = (c[None, :] >= cs[:, None]) & (c[None, :] < cs[:, None] + WIN_C)
    bias = rpb[:, dr[:, None, :, None], dc[None, :, None, :]]
    bias = jnp.where(ok[None, None, :, None, :], bias.astype(F32), NEG)
    return bias.reshape(H_NA, N_BIAS_PATTERNS, GRID_W, wr * GRID_W)


def na_attention(qk, vbf, bias, dims):
    rows = dims.seq // GRID_W
    wr = min(WIN_R, rows)
    qblocks = rows // NA_ROWS_PER_STEP
    tq = NA_ROWS_PER_STEP * GRID_W
    ctx_blk0 = dims.m_x // dims.ctx
    kern = functools.partial(_na_kernel, rows=rows)
    return pl.pallas_call(
        kern,
        out_shape=jax.ShapeDtypeStruct((dims.m_x, H_NA * DH_NA), BF16),
        grid=(dims.batch, H_NA, qblocks),
        in_specs=[
            pl.BlockSpec((tq, DH_NA), lambda b, h, i: (b * qblocks + i, h)),
            pl.BlockSpec((dims.seq, DH_NA), lambda b, h, i: (b, H_NA + h)),
            pl.BlockSpec((dims.seq, DH_NA), lambda b, h, i: (b, h)),
            pl.BlockSpec((dims.ctx, DH_NA), lambda b, h, i: (ctx_blk0 + b, H_NA + h)),
            pl.BlockSpec((dims.ctx, DH_NA), lambda b, h, i: (ctx_blk0 + b, h)),
            pl.BlockSpec((1, N_BIAS_PATTERNS, GRID_W, wr * GRID_W), lambda b, h, i: (h, 0, 0, 0)),
        ],
        out_specs=pl.BlockSpec((tq, DH_NA), lambda b, h, i: (b * qblocks + i, h)),
        compiler_params=_cparams(("arbitrary", "arbitrary", "arbitrary")), name="na_attention",
    )(qk, qk, vbf, qk, vbf, bias)


def _ctx_attn_kernel(q_ref, k_ref, v_ref, o_ref):
    s = lax.dot_general(q_ref[...], k_ref[...], NT_DIMS, preferred_element_type=F32) * (DH_NA ** -0.5)
    m = jnp.max(s, axis=-1, keepdims=True)
    p = jnp.exp(s - m)
    den = jnp.sum(p, axis=-1, keepdims=True)
    o = jnp.dot(p.astype(BF16), v_ref[...], preferred_element_type=F32)
    o_ref[...] = (o / den).astype(o_ref.dtype)


def ctx_attention(qk, vbf, dims):
    ctx_blk0 = dims.m_x // dims.ctx
    return pl.pallas_call(
        _ctx_attn_kernel,
        out_shape=jax.ShapeDtypeStruct((dims.batch * dims.ctx, H_NA * DH_NA), BF16),
        grid=(dims.batch, H_NA),
        in_specs=[
            pl.BlockSpec((dims.ctx, DH_NA), lambda b, h: (ctx_blk0 + b, h)),
            pl.BlockSpec((dims.ctx, DH_NA), lambda b, h: (ctx_blk0 + b, H_NA + h)),
            pl.BlockSpec((dims.ctx, DH_NA), lambda b, h: (ctx_blk0 + b, h)),
        ],
        out_specs=pl.BlockSpec((dims.ctx, DH_NA), lambda b, h: (b, h)),
        compiler_params=_cparams(("arbitrary", "arbitrary")), name="ctx_attention",
    )(qk, qk, vbf)


def _log_sigmoid(x):
    return jnp.minimum(x, 0.0) - jnp.log(1.0 + jnp.exp(-jnp.abs(x)))


def _rope(t, cos, sin_signed, even):
    swapped = jnp.where(even, pltpu.roll(t, DK_M - 1, 1), pltpu.roll(t, 1, 1))
    return t * cos + swapped * sin_signed


def _mlstm_kernel(q_ref, k_ref, v_ref, g_ref, gt_ref, cos_ref, sin_ref, o_ref, c_ref, m_ref):
    direction = pl.program_id(1)
    t_step = pl.program_id(2)

    @pl.when(t_step == 0)
    def _():
        c_ref[...] = jnp.zeros(c_ref.shape, F32)
        m_ref[...] = jnp.zeros(m_ref.shape, F32)

    ti = lax.broadcasted_iota(I32, (CHUNK, CHUNK), 0)
    si = lax.broadcasted_iota(I32, (CHUNK, CHUNK), 1)
    mask = (si - ti) * (1 - 2 * direction) <= 0
    tri = mask.astype(BF16)
    g = g_ref[0]
    gt = gt_ref[0]
    lf_hi, lf_lo = _split_bf16(_log_sigmoid(g))
    cum_col = (jnp.dot(tri, lf_hi, preferred_element_type=F32)
               + jnp.dot(tri, lf_lo, preferred_element_type=F32))
    lft = _log_sigmoid(gt)
    lft_hi, lft_lo = _split_bf16(lft)
    cum_row = (lax.dot_general(lft_hi, tri, NT_DIMS, preferred_element_type=F32)
               + lax.dot_general(lft_lo, tri, NT_DIMS, preferred_element_type=F32))
    cos = cos_ref[...]
    sin = sin_ref[...]
    even = (lax.broadcasted_iota(I32, (CHUNK, DK_M), 1) % 2) == 0
    ones_col = (lax.broadcasted_iota(I32, (CHUNK, DV_EXT - DV_M), 1) == 0).astype(BF16)
    for h in range(H_M):
        q = _rope(q_ref[:, h * DK_M:(h + 1) * DK_M], cos, sin, even)
        k = _rope(k_ref[:, h * DK_M:(h + 1) * DK_M], cos, sin, even) * (DK_M ** -0.5)
        v_ext = jnp.concatenate([v_ref[:, h * DV_M:(h + 1) * DV_M].astype(BF16), ones_col], axis=1)
        qb = q.astype(BF16)
        b_col = cum_col[:, H_M + h:H_M + h + 1]
        i_col = g[:, h:h + 1]
        b_row = cum_row[H_M + h:H_M + h + 1, :]
        i_row = gt[h:h + 1, :]
        b_last = jnp.sum(lft[H_M + h:H_M + h + 1, :], axis=1, keepdims=True)
        m_old = m_ref[h][0:1, 0:1]
        c_old = c_ref[h]
        d_end_row = b_last - b_row + i_row
        m_new = jnp.maximum(b_last + m_old, jnp.max(d_end_row, axis=1, keepdims=True))
        w_end = jnp.exp(b_last - b_col + i_col - m_new)
        decay = jnp.exp(b_last + m_old - m_new)
        kw_t = (k * w_end).T.astype(BF16)
        c_ref[h] = decay * c_old + jnp.dot(kw_t, v_ext, preferred_element_type=F32)
        m_ref[h] = jnp.broadcast_to(m_new, m_ref.shape[1:])
        d_mat = jnp.where(mask, b_col - b_row + i_row, NEG)
        m_in = b_col + m_old
        m_t = jnp.maximum(m_in, jnp.max(d_mat, axis=1, keepdims=True))
        s = lax.dot_general(qb, k.astype(BF16), NT_DIMS, preferred_element_type=F32) * jnp.exp(d_mat - m_t)
        a_in = jnp.exp(m_in - m_t)
        numden = (jnp.dot(s.astype(BF16), v_ext, preferred_element_type=F32)
                  + a_in * jnp.dot(qb, c_old.astype(BF16), preferred_element_type=F32))
        den = numden[:, DV_M:DV_M + 1]
        o_ref[0, :, h * DV_M:(h + 1) * DV_M] = numden[:, :DV_M] / jnp.maximum(jnp.abs(den), jnp.exp(-m_t))


def mlstm_scan(p, g_dir, gt_dir, cos_t, sin_t, dims):
    ncx = dims.ctx // CHUNK
    nlx = dims.seq // CHUNK
    x_blocks = dims.m_x // CHUNK

    def row_block(b, d, t):
        in_ctx = t < ncx
        cc = jnp.where(d == 0, t, ncx - 1 - t)
        cx = jnp.where(d == 0, t - ncx, nlx - 1 - (t - ncx))
        return jnp.where(in_ctx, x_blocks + b * ncx + cc, b * nlx + cx)

    def rope_block(b, d, t):
        in_ctx = t < ncx
        cx = jnp.where(d == 0, t - ncx, nlx - 1 - (t - ncx))
        return jnp.where(in_ctx, nlx, cx)

    return pl.pallas_call(
        _mlstm_kernel,
        out_shape=jax.ShapeDtypeStruct((2, dims.m_all, H_M * DV_M), F32),
        grid=(dims.batch, 2, ncx + nlx),
        in_specs=[
            pl.BlockSpec((CHUNK, H_M * DK_M), lambda b, d, t: (row_block(b, d, t), OFF_MQ // 512)),
            pl.BlockSpec((CHUNK, H_M * DK_M), lambda b, d, t: (row_block(b, d, t), OFF_MK // 512)),
            pl.BlockSpec((CHUNK, H_M * DV_M), lambda b, d, t: (row_block(b, d, t), OFF_MV // 1024)),
            pl.BlockSpec((1, CHUNK, LANES), lambda b, d, t: (d, row_block(b, d, t), 0)),
            pl.BlockSpec((1, 8, CHUNK), lambda b, d, t: (d, 0, row_block(b, d, t))),
            pl.BlockSpec((CHUNK, DK_M), lambda b, d, t: (rope_block(b, d, t), 0)),
            pl.BlockSpec((CHUNK, DK_M), lambda b, d, t: (rope_block(b, d, t), 0)),
        ],
        out_specs=pl.BlockSpec((1, CHUNK, H_M * DV_M), lambda b, d, t: (d, row_block(b, d, t), 0)),
        scratch_shapes=[pltpu.VMEM((H_M, DK_M, DV_EXT), F32), pltpu.VMEM((H_M, 8, LANES), F32)],
        compiler_params=_cparams(("arbitrary", "arbitrary", "arbitrary")), name="mlstm_scan",
    )(p, p, p, g_dir, gt_dir, cos_t, sin_t)


def _mlstm_out_kernel(h_ref, o_ref, g_ref, out_ref):
    for h in range(H_M):
        sl = slice(h * DV_M, (h + 1) * DV_M)
        y = _rms(h_ref[0, :, sl] + h_ref[1, :, sl], g_ref[:, sl])
        out_ref[:, sl] = (y * jax.nn.sigmoid(o_ref[:, sl])).astype(out_ref.dtype)


def mlstm_out(hdir, p, g, dims):
    m = dims.m_all
    tm = 256
    n = H_M * DV_M
    return pl.pallas_call(
        _mlstm_out_kernel,
        out_shape=jax.ShapeDtypeStruct((m, n), BF16),
        grid=(m // tm,),
        in_specs=[pl.BlockSpec((2, tm, n), lambda i: (0, i, 0)),
                  pl.BlockSpec((tm, n), lambda i: (i, OFF_MO // n)),
                  pl.BlockSpec((1, n), lambda i: (0, 0))],
        out_specs=pl.BlockSpec((tm, n), lambda i: (i, 0)),
        compiler_params=_cparams(("arbitrary",)), name="mlstm_out",
    )(hdir, p, g.reshape(1, n))


def rope_tables(dims):
    t = jnp.arange(dims.seq)
    row = (t // GRID_W).astype(F32)
    col = (t % GRID_W).astype(F32)
    quarter = DK_M // 4
    inv = ROPE_BASE ** (-jnp.arange(quarter, dtype=F32) / quarter)
    ang = jnp.concatenate([row[:, None] * inv, col[:, None] * inv], axis=-1)
    cos = jnp.repeat(jnp.cos(ang), 2, axis=-1)
    sin = jnp.repeat(jnp.sin(ang), 2, axis=-1) * jnp.tile(jnp.array([-1.0, 1.0], F32), DK_M // 2)
    cos = jnp.concatenate([cos, jnp.ones((CHUNK, DK_M), F32)], axis=0)
    sin = jnp.concatenate([sin, jnp.zeros((CHUNK, DK_M), F32)], axis=0)
    return cos, sin


ROW_SUB = 16


def _gather_kernel(src_ref, h_hbm, o_ref, sem, *, tm):
    i = pl.program_id(0)

    def start(r, carry):
        pltpu.make_async_copy(h_hbm.at[src_ref[i * tm + r]], o_ref.at[r], sem).start()
        return carry

    lax.fori_loop(0, tm, start, 0)

    def wait(r, carry):
        pltpu.make_async_copy(h_hbm.at[0], o_ref.at[r], sem).wait()
        return carry

    lax.fori_loop(0, tm, wait, 0)


def gather_rows(h, src, *, tm):
    m, d = h.shape
    r_total = src.shape[0]
    h3 = h.reshape(m, ROW_SUB, d // ROW_SUB)
    out = pl.pallas_call(
        functools.partial(_gather_kernel, tm=tm),
        out_shape=jax.ShapeDtypeStruct((r_total, ROW_SUB, d // ROW_SUB), h.dtype),
        grid_spec=pltpu.PrefetchScalarGridSpec(
            num_scalar_prefetch=1,
            grid=(r_total // tm,),
            in_specs=[pl.BlockSpec(memory_space=pl.ANY)],
            out_specs=pl.BlockSpec((tm, ROW_SUB, d // ROW_SUB), lambda i, s: (i, 0, 0)),
            scratch_shapes=[pltpu.SemaphoreType.DMA],
        ),
        compiler_params=_cparams(("arbitrary",)), name="moe_gather",
    )(src, h3)
    return out.reshape(r_total, d)


def _combine_kernel(p1_ref, p2_ref, y_hbm, x_ref, gate_ref, o_ref, y1_ref, y2_ref, sem, *, tm, seq, n_groups):
    i = pl.program_id(0)

    def start(r, carry):
        t = i * tm + r
        pltpu.make_async_copy(y_hbm.at[p1_ref[t]], y1_ref.at[r], sem.at[0]).start()
        pltpu.make_async_copy(y_hbm.at[p2_ref[t]], y2_ref.at[r], sem.at[1]).start()
        return carry

    lax.fori_loop(0, tm, start, 0)

    def wait(r, carry):
        pltpu.make_async_copy(y_hbm.at[0], y1_ref.at[r], sem.at[0]).wait()
        pltpu.make_async_copy(y_hbm.at[0], y2_ref.at[r], sem.at[1]).wait()
        return carry

    lax.fori_loop(0, tm, wait, 0)
    row0 = i * tm
    grp = jnp.int32(n_groups - 1)
    for b in reversed(range(n_groups - 1)):
        grp = jnp.where(row0 < (b + 1) * seq, b, grp)
    gate = gate_ref[grp]
    o_ref[...] = x_ref[...] + gate[None] * (y1_ref[...] + y2_ref[...])


def moe_combine(xs, y, pos1, pos2, mod, layer, gate_chunk, dims, *, m):
    d = dims.d_model
    tm = _pick_tile(math.gcd(dims.seq, dims.ctx), (256, 128))
    sub = d // ROW_SUB
    x3 = xs.reshape(xs.shape[0], ROW_SUB, sub)
    y3 = y.reshape(y.shape[0], ROW_SUB, sub)
    mod3 = mod.reshape(mod.shape[0], mod.shape[1] // sub // ROW_SUB, ROW_SUB, sub)
    kern = functools.partial(_combine_kernel, tm=tm, seq=dims.seq, n_groups=dims.batch + 1)
    out = pl.pallas_call(
        kern,
        out_shape=jax.ShapeDtypeStruct((m, ROW_SUB, sub), F32),
        grid_spec=pltpu.PrefetchScalarGridSpec(
            num_scalar_prefetch=2,
            grid=(m // tm,),
            in_specs=[pl.BlockSpec(memory_space=pl.ANY),
                      pl.BlockSpec((tm, ROW_SUB, sub), lambda i, a, b: (i, 0, 0)),
                      pl.BlockSpec((8, None, ROW_SUB, sub), lambda i, a, b: (MOD_ROWS // 8 * layer, gate_chunk, 0, 0))],
            out_specs=pl.BlockSpec((tm, ROW_SUB, sub), lambda i, a, b: (i, 0, 0)),
            scratch_shapes=[pltpu.VMEM((tm, ROW_SUB, sub), F32), pltpu.VMEM((tm, ROW_SUB, sub), F32),
                            pltpu.SemaphoreType.DMA((2,))],
        ),
        compiler_params=_cparams(("arbitrary",)), name="moe_combine",
    )(pos1, pos2, y3, x3, mod3)
    return out.reshape(m, d)


def moe_dispatch_plan(route, m, tm):
    e = jnp.concatenate([route[0, :m], route[1, :m]]).astype(I32)
    w = jnp.concatenate([route[2, :m], route[3, :m]])
    tok = jnp.tile(jnp.arange(m, dtype=I32), 2)
    onehot = (e[:, None] == jnp.arange(N_EXPERTS, dtype=I32)[None, :]).astype(I32)
    rank = jnp.sum((jnp.cumsum(onehot, axis=0) - onehot) * onehot, axis=1)
    counts = jnp.sum(onehot, axis=0)
    padded = ((counts + tm - 1) // tm) * tm
    ends = jnp.cumsum(padded)
    offs = ends - padded
    pos = offs[e] + rank
    r_total = TOP_K * m + N_EXPERTS * tm
    src = jnp.zeros((r_total,), I32).at[pos].set(tok)
    scale = jnp.zeros((r_total,), F32).at[pos].set(w)
    tile_start = jnp.arange(r_total // tm, dtype=I32) * tm
    tile_e = jnp.minimum(jnp.sum((tile_start[:, None] >= ends[None, :]).astype(I32), axis=1), N_EXPERTS - 1)
    n_valid = (ends[-1] // tm).astype(I32).reshape(1)
    return src, scale, pos[:m], pos[m:], tile_e, n_valid


def _dense_tile(m, candidates):
    return _pick_tile(m, candidates)


def kernel(x, c, ctx, c_ctx, ada_w, ada_b, norm1_g, norm2_g, w_in, gate_b, na_q_g, na_k_g, na_rpb, m_norm_g, w_out, ffn_w_gate, ffn_w_up, ffn_w_down, moe_router, moe_w_gate, moe_w_up, moe_w_down):
    batch, seq, d = x.shape
    dims = Dims(batch=batch, seq=seq, ctx=ctx.shape[1], d_model=d, d_ff=ffn_w_gate.shape[2], depth=ada_w.shape[0])
    depth = dims.depth
    n_groups = batch + 1
    rows = seq // GRID_W
    assert rows >= 8 and rows % NA_ROWS_PER_STEP == 0 and batch + 1 <= 8
    assert seq % 256 == 0 and dims.ctx % CHUNK == 0 and dims.m_x % dims.ctx == 0

    xs = jnp.concatenate([x.reshape(dims.m_x, d), ctx.reshape(batch * dims.ctx, d)], axis=0)

    cond = jnp.concatenate([jax.nn.silu(c), jax.nn.silu(c_ctx)[None], jnp.zeros((MOD_ROWS - n_groups, d), F32)], axis=0)
    tn_ada = _pick_tile(6 * d, (1024, 512, 256))
    tn_d = min(512, d)
    mod = gmm(cond.astype(BF16), ada_w, jnp.arange(depth, dtype=I32), jnp.full((1,), depth, I32),
              tm=MOD_ROWS, tn=tn_ada, n_out=6 * d, out_dtype=F32, epilogue="bias",
              extras=(ada_b.reshape(depth, 1, 6 * d),),
              extra_specs=(pl.BlockSpec((1, 1, tn_ada), lambda j, i, g, nv: (g[i], 0, j)),),
              a_fixed=True, name="ada_mod")

    cos_t, sin_t = rope_tables(dims)
    w_gate_cols = jnp.pad(w_in[:, :, N_MAIN:], ((0, 0), (0, 0), (0, LANES - 4 * H_M)))
    gate_b_pad = jnp.pad(gate_b, ((0, 0), (0, LANES - 4 * H_M)))

    def dense_groups(m, tm, g):
        nt = m // tm
        return jnp.full((nt,), g, I32), jnp.full((1,), nt, I32)

    for layer in range(depth):
        last = layer == depth - 1
        m_out = dims.m_x if last else dims.m_all

        hbf = norm_mod(xs, norm1_g[layer], mod, layer, 0, 1, dims, m=dims.m_all)
        tm_a = _dense_tile(dims.m_all, (2176, 1088, 640, 512, 256))
        grp, nv = dense_groups(dims.m_all, tm_a, layer)
        p = gmm(hbf, w_in, grp, nv, tm=tm_a, tn=512, n_out=N_MAIN, out_dtype=F32, name="w_in_proj")
        gates = gmm(hbf, w_gate_cols, grp, nv, tm=tm_a, tn=LANES, n_out=LANES, out_dtype=F32, name="gate_proj")
        gates = gates + gate_b_pad[layer][None, :]
        g_dir = jnp.stack([gates, jnp.roll(gates, -2 * H_M, axis=1)])
        gt_dir = jnp.stack([gates[:, :2 * H_M].T, gates[:, 2 * H_M:4 * H_M].T])

        qk, vbf = qkv_prep(p, na_q_g[layer], na_k_g[layer], dims)
        na_x = na_attention(qk, vbf, na_bias_table(na_rpb[layer], rows), dims)
        hdir = mlstm_scan(p, g_dir, gt_dir, cos_t, sin_t, dims)
        m_all_out = mlstm_out(hdir, p, m_norm_g[layer], dims)
        if last:
            mix = jnp.concatenate([na_x, m_all_out[:dims.m_x]], axis=1)
        else:
            na_c = ctx_attention(qk, vbf, dims)
            mix = jnp.concatenate([jnp.concatenate([na_x, na_c], axis=0), m_all_out], axis=1)

        tm_o = _dense_tile(m_out, (1088, 1024, 640, 512, 256))
        grp, nv = dense_groups(m_out, tm_o, layer)
        res_specs = lambda tm, tn, chunk: (
            pl.BlockSpec((tm, tn), lambda j, i, g, nv: (i, j)),
            pl.BlockSpec((8, tn), lambda j, i, g, nv: (MOD_ROWS // 8 * layer, chunk * (d // tn) + j)))
        xs = gmm(mix, w_out, grp, nv, tm=tm_o, tn=tn_d, n_out=d, out_dtype=F32, epilogue="residual",
                 extras=(xs, mod), extra_specs=res_specs(tm_o, tn_d, 2), seq=seq, n_groups=n_groups,
                 name="w_out_proj")

        idx = layer // 2
        if layer % 2 == 0:
            h2 = norm_mod(xs, norm2_g[layer], mod, layer, 3, 4, dims, m=m_out)
            tm_u = _dense_tile(m_out, (1088, 1024, 640, 512, 256))
            grp, nv = dense_groups(m_out, tm_u, idx)
            act = gmm_swiglu(h2, ffn_w_gate, ffn_w_up, grp, nv, tm=tm_u, tn=512, name="ffn_up")
            tm_d = _dense_tile(m_out, (544, 512, 320, 256))
            grp, nv = dense_groups(m_out, tm_d, idx)
            xs = gmm(act, ffn_w_down, grp, nv, tm=tm_d, tn=tn_d, n_out=d, out_dtype=F32, epilogue="residual",
                     extras=(xs, mod), extra_specs=res_specs(tm_d, tn_d, 5), seq=seq, n_groups=n_groups,
                     name="ffn_down")
        else:
            h2, route = norm_mod(xs, norm2_g[layer], mod, layer, 3, 4, dims, m=m_out,
                                 router_t=moe_router[idx].T)
            tm_e = 512 if m_out >= 4096 else 128
            src, scale, pos1, pos2, tile_e, n_valid = moe_dispatch_plan(route, m_out, tm_e)
            hs = gather_rows(h2, src, tm=tm_e)
            grp = tile_e + idx * N_EXPERTS
            nw = moe_w_gate.shape[0] * N_EXPERTS
            act = gmm_swiglu(hs, moe_w_gate.reshape(nw, d, dims.d_ff), moe_w_up.reshape(nw, d, dims.d_ff),
                             grp, n_valid, tm=tm_e, tn=512, name="moe_up")
            scale_b = jnp.broadcast_to(scale[:, None], (scale.shape[0], LANES))
            y = gmm(act, moe_w_down.reshape(nw, dims.d_ff, d), grp, n_valid, tm=tm_e, tn=tn_d, n_out=d,
                    out_dtype=F32, epilogue="rowscale", extras=(scale_b,),
                    extra_specs=(pl.BlockSpec((tm_e, LANES), lambda j, i, g, nv: (i, 0)),), name="moe_down")
            xs = moe_combine(xs, y, pos1, pos2, mod, layer, 5, dims, m=m_out)

    return xs[:dims.m_x].reshape(batch, seq, d)
```

```python
import functools
import math
from typing import NamedTuple

import jax
import jax.numpy as jnp
import numpy as np
from jax import lax
from jax.experimental import pallas as pl
from jax.experimental.pallas import tpu as pltpu

F32 = jnp.float32
BF16 = jnp.bfloat16
I32 = jnp.int32

GRID_W = 64
H_NA = 8
DH_NA = 128
WIN_R = 8
WIN_C = 16
H_M = 4
DK_M = 128
DV_M = 256
CHUNK = 128
ROPE_BASE = 10000.0
N_EXPERTS = 8
TOP_K = 2
EPS = 1e-6
D_MIX = 2048
NEG = -1e30

OFF_Q, OFF_K, OFF_V = 0, 1024, 2048
OFF_MQ, OFF_MK, OFF_MV, OFF_MO = 3072, 3584, 4096, 5120
N_MAIN = 6144
DV_EXT = DV_M + 128

LANES = 128
VMEM_LIMIT = 56 * 1024 * 1024
MOD_ROWS = 16


class Dims(NamedTuple):
    batch: int
    seq: int
    ctx: int
    d_model: int
    d_ff: int
    depth: int

    @property
    def m_x(self):
        return self.batch * self.seq

    @property
    def m_all(self):
        return self.batch * (self.seq + self.ctx)


def _cparams(sem):
    return pltpu.CompilerParams(dimension_semantics=sem, vmem_limit_bytes=VMEM_LIMIT)


def _group_select(row, vals, seq):
    out = vals[-1]
    for b in reversed(range(len(vals) - 1)):
        out = jnp.where(row < (b + 1) * seq, vals[b], out)
    return out


def _pick_tile(m, candidates):
    for t in candidates:
        if m % t == 0:
            return t
    raise ValueError(f"no row tile for {m}")


def _gmm_kernel(grp_ref, nv_ref, a_ref, w_ref, *refs, epilogue, tm, seq, n_groups, rows_outer):
    if epilogue == "residual":
        res_ref, gate_ref, o_ref, wbf_ref = refs
    elif epilogue == "rowscale":
        rs_ref, o_ref, wbf_ref = refs
    elif epilogue == "bias":
        b_ref, o_ref, wbf_ref = refs
    else:
        o_ref, wbf_ref = refs
    if rows_outer:
        i = pl.program_id(0)
        wbf_ref[...] = w_ref[0].astype(BF16)
    else:
        i = pl.program_id(1)
        g = grp_ref[i]
        g_prev = grp_ref[jnp.maximum(i - 1, 0)]

        @pl.when((i == 0) | (g != g_prev))
        def _():
            wbf_ref[...] = w_ref[0].astype(BF16)

    @pl.when(i < nv_ref[0])
    def _():
        acc = jnp.dot(a_ref[...], wbf_ref[...], preferred_element_type=F32)
        if epilogue == "residual":
            row = i * tm + lax.broadcasted_iota(I32, (tm, 1), 0)
            gate = _group_select(row, [gate_ref[k:k + 1, :] for k in range(n_groups)], seq)
            acc = res_ref[...] + gate * acc
        elif epilogue == "rowscale":
            acc = acc * rs_ref[:, 0:1]
        elif epilogue == "bias":
            acc = acc + b_ref[0]
        o_ref[...] = acc.astype(o_ref.dtype)

    @pl.when(i >= nv_ref[0])
    def _():
        o_ref[...] = jnp.zeros(o_ref.shape, o_ref.dtype)


def gmm(a, w, grp, nvalid, *, tm, tn, n_out, out_dtype, epilogue=None, extras=(), extra_specs=(),
        a_fixed=False, seq=0, n_groups=0, rows_outer=False, name="gmm"):
    k = a.shape[1]
    nt = grp.shape[0]
    m_out = nt * tm

    def spec(shape, fn):
        return pl.BlockSpec(shape, (lambda i, j, g, nv: fn(j, i, g, nv)) if rows_outer else fn)

    a_map = (lambda j, i, g, nv: (0, 0)) if a_fixed else (lambda j, i, g, nv: (i, 0))
    in_specs = [spec((tm, k), a_map), spec((1, k, tn), lambda j, i, g, nv: (g[i], 0, j))]
    in_specs += [spec(shape, fn) for shape, fn in extra_specs]
    kern = functools.partial(_gmm_kernel, epilogue=epilogue, tm=tm, seq=seq, n_groups=n_groups,
                             rows_outer=rows_outer)
    nj = n_out // tn
    return pl.pallas_call(
        kern,
        out_shape=jax.ShapeDtypeStruct((m_out, n_out), out_dtype),
        grid_spec=pltpu.PrefetchScalarGridSpec(
            num_scalar_prefetch=2,
            grid=(nt, nj) if rows_outer else (nj, nt),
            in_specs=in_specs,
            out_specs=spec((tm, tn), lambda j, i, g, nv: (i, j)),
            scratch_shapes=[pltpu.VMEM((k, tn), BF16)],
        ),
        compiler_params=_cparams(("arbitrary", "arbitrary")),
        name=name,
    )(grp, nvalid, a, w, *extras)


def _gmm_swiglu_kernel(grp_ref, nv_ref, a_ref, wg_ref, wu_ref, o_ref, wgbf_ref, wubf_ref):
    i = pl.program_id(1)
    g = grp_ref[i]
    g_prev = grp_ref[jnp.maximum(i - 1, 0)]

    @pl.when((i == 0) | (g != g_prev))
    def _():
        wgbf_ref[...] = wg_ref[0].astype(BF16)
        wubf_ref[...] = wu_ref[0].astype(BF16)

    @pl.when(i < nv_ref[0])
    def _():
        a = a_ref[...]
        gate = jnp.dot(a, wgbf_ref[...], preferred_element_type=F32)
        up = jnp.dot(a, wubf_ref[...], preferred_element_type=F32)
        o_ref[...] = (gate * jax.nn.sigmoid(gate) * up).astype(o_ref.dtype)

    @pl.when(i >= nv_ref[0])
    def _():
        o_ref[...] = jnp.zeros(o_ref.shape, o_ref.dtype)


def gmm_swiglu(a, wg, wu, grp, nvalid, *, tm, tn, name="gmm_swiglu"):
    k = a.shape[1]
    n = wg.shape[2]
    nt = grp.shape[0]
    wspec = pl.BlockSpec((1, k, tn), lambda j, i, g, nv: (g[i], 0, j))
    return pl.pallas_call(
        _gmm_swiglu_kernel,
        out_shape=jax.ShapeDtypeStruct((nt * tm, n), BF16),
        grid_spec=pltpu.PrefetchScalarGridSpec(
            num_scalar_prefetch=2,
            grid=(n // tn, nt),
            in_specs=[pl.BlockSpec((tm, k), lambda j, i, g, nv: (i, 0)), wspec, wspec],
            out_specs=pl.BlockSpec((tm, tn), lambda j, i, g, nv: (i, j)),
            scratch_shapes=[pltpu.VMEM((k, tn), BF16), pltpu.VMEM((k, tn), BF16)],
        ),
        compiler_params=_cparams(("arbitrary", "arbitrary")),
        name=name,
    )(grp, nvalid, a, wg, wu)


def _rms(x, g):
    return x * lax.rsqrt(jnp.mean(x * x, axis=-1, keepdims=True) + EPS) * g


def _split_bf16(x):
    hi = x.astype(BF16)
    lo = (x - hi.astype(F32)).astype(BF16)
    return hi, lo


def _norm_mod_kernel(x_ref, g_ref, shift_ref, scale_ref, *refs, tm, seq, n_groups, with_router):
    if with_router:
        r_ref, o_ref, route_ref = refs
    else:
        (o_ref,) = refs
    i = pl.program_id(0)
    row0 = i * tm
    grp = jnp.int32(n_groups - 1)
    for b in reversed(range(n_groups - 1)):
        grp = jnp.where(row0 < (b + 1) * seq, b, grp)
    y = _rms(x_ref[...], g_ref[...])
    h = y * (1.0 + scale_ref[pl.ds(grp, 1), :]) + shift_ref[pl.ds(grp, 1), :]
    o_ref[...] = h.astype(o_ref.dtype)
    if with_router:
        nt_dims = (((1,), (1,)), ((), ()))
        h_hi, h_lo = _split_bf16(h)
        r_hi, r_lo = _split_bf16(r_ref[...])
        logits = (lax.dot_general(r_hi, h_hi, nt_dims, preferred_element_type=F32)
                  + lax.dot_general(r_hi, h_lo, nt_dims, preferred_element_type=F32)
                  + lax.dot_general(r_lo, h_hi, nt_dims, preferred_element_type=F32))
        ids = lax.broadcasted_iota(I32, logits.shape, 0)
        m1 = jnp.max(logits, axis=0, keepdims=True)
        i1 = jnp.min(jnp.where(logits == m1, ids, N_EXPERTS), axis=0, keepdims=True)
        rest = jnp.where(ids == i1, -jnp.inf, logits)
        m2 = jnp.max(rest, axis=0, keepdims=True)
        i2 = jnp.min(jnp.where(rest == m2, ids, N_EXPERTS), axis=0, keepdims=True)
        e = jnp.exp(m2 - m1)
        w1 = 1.0 / (1.0 + e)
        w2 = e / (1.0 + e)
        out_row = lax.broadcasted_iota(I32, (8, tm), 0)
        route_ref[...] = jnp.where(out_row == 0, i1.astype(F32), jnp.where(
            out_row == 1, i2.astype(F32), jnp.where(out_row == 2, w1, jnp.where(out_row == 3, w2, 0.0))))


def norm_mod(xs, g, mod, layer, shift_chunk, scale_chunk, dims, *, m, router_t=None):
    d = dims.d_model
    tm = _pick_tile(math.gcd(dims.seq, dims.ctx), (256, 128))
    with_router = router_t is not None
    kern = functools.partial(_norm_mod_kernel, tm=tm, seq=dims.seq, n_groups=dims.batch + 1,
                             with_router=with_router)
    in_specs = [
        pl.BlockSpec((tm, d), lambda i: (i, 0)),
        pl.BlockSpec((1, d), lambda i: (0, 0)),
        pl.BlockSpec((8, d), lambda i: (MOD_ROWS // 8 * layer, shift_chunk)),
        pl.BlockSpec((8, d), lambda i: (MOD_ROWS // 8 * layer, scale_chunk)),
    ]
    args = [xs, g.reshape(1, d), mod, mod]
    out_shape = [jax.ShapeDtypeStruct((m, d), F32 if with_router else BF16)]
    out_specs = [pl.BlockSpec((tm, d), lambda i: (i, 0))]
    if with_router:
        in_specs.append(pl.BlockSpec((N_EXPERTS, d), lambda i: (0, 0)))
        args.append(router_t)
        out_shape.append(jax.ShapeDtypeStruct((8, m), F32))
        out_specs.append(pl.BlockSpec((8, tm), lambda i: (0, i)))
    res = pl.pallas_call(
        kern, out_shape=out_shape, grid=(m // tm,), in_specs=in_specs, out_specs=out_specs,
        compiler_params=_cparams(("arbitrary",)), name="norm_mod_router" if with_router else "norm_mod",
    )(*args)
    return res if with_router else res[0]


def _qkv_prep_kernel(qk_ref, v_ref, qg_ref, kg_ref, qk_out, v_out):
    for h in range(2 * H_NA):
        g = qg_ref[...] if h < H_NA else kg_ref[...]
        sl = slice(h * DH_NA, (h + 1) * DH_NA)
        qk_out[:, sl] = _rms(qk_ref[:, sl], g).astype(BF16)
    v_out[...] = v_ref[...].astype(BF16)


def qkv_prep(p, qg, kg, dims):
    m = dims.m_all
    tm = 256
    return pl.pallas_call(
        _qkv_prep_kernel,
        out_shape=[jax.ShapeDtypeStruct((m, 2 * H_NA * DH_NA), BF16),
                   jax.ShapeDtypeStruct((m, H_NA * DH_NA), BF16)],
        grid=(m // tm,),
        in_specs=[pl.BlockSpec((tm, 2048), lambda i: (i, 0)),
                  pl.BlockSpec((tm, 1024), lambda i: (i, OFF_V // 1024)),
                  pl.BlockSpec((1, DH_NA), lambda i: (0, 0)),
                  pl.BlockSpec((1, DH_NA), lambda i: (0, 0))],
        out_specs=[pl.BlockSpec((tm, 2048), lambda i: (i, 0)),
                   pl.BlockSpec((tm, 1024), lambda i: (i, 0))],
        compiler_params=_cparams(("arbitrary",)), name="qkv_prep",
    )(p, p, qg.reshape(1, DH_NA), kg.reshape(1, DH_NA))


NT_DIMS = (((1,), (1,)), ((), ()))
NA_ROWS_PER_STEP = 8
N_BIAS_PATTERNS = 8


def _na_kernel(q_ref, k_ref, v_ref, kc_ref, vc_ref, bias_ref, o_ref, *, rows):
    qb = pl.program_id(2)
    scale = DH_NA ** -0.5
    wr = min(WIN_R, rows)
    kc = kc_ref[...]
    vc = vc_ref[...]
    for rr in range(NA_ROWS_PER_STEP):
        r = qb * NA_ROWS_PER_STEP + rr
        rs = jnp.clip(r - wr // 2, 0, rows - wr)
        pid = jnp.where(r < 4, r, jnp.where(r > rows - 4, r - (rows - 8), 4))
        start = pl.multiple_of(rs * GRID_W, GRID_W)
        q = q_ref[rr * GRID_W:(rr + 1) * GRID_W, :]
        kb = k_ref[pl.ds(start, wr * GRID_W), :]
        vb = v_ref[pl.ds(start, wr * GRID_W), :]
        s = lax.dot_general(q, kb, NT_DIMS, preferred_element_type=F32) * scale + bias_ref[0, pid]
        sc = lax.dot_general(q, kc, NT_DIMS, preferred_element_type=F32) * scale
        m = jnp.maximum(jnp.max(s, axis=-1, keepdims=True), jnp.max(sc, axis=-1, keepdims=True))
        pb = jnp.exp(s - m)
        pc = jnp.exp(sc - m)
        den = jnp.sum(pb, axis=-1, keepdims=True) + jnp.sum(pc, axis=-1, keepdims=True)
        o = (jnp.dot(pb.astype(BF16), vb, preferred_element_type=F32)
             + jnp.dot(pc.astype(BF16), vc, preferred_element_type=F32))
        o_ref[rr * GRID_W:(rr + 1) * GRID_W, :] = (o / den).astype(o_ref.dtype)


def na_bias_table(rpb, rows):
    wr = min(WIN_R, rows)
    r = np.array([0, 1, 2, 3, rows // 2, rows - 3, rows - 2, rows - 1])
    rs = np.clip(r - wr // 2, 0, rows - wr)
    dr = rs[:, None] + np.arange(wr)[None, :] - r[:, None] + (WIN_R - 1)
    c = np.arange(GRID_W)
    dc = np.clip(c[None, :] - c[:, None], -(WIN_C - 1), WIN_C - 1) + (WIN_C - 1)
    cs = np.clip(c - WIN_C // 2, 0, GRID_W - WIN_C)
    ok = (c[None, :] >= cs[:, None]) & (c[None, :] < cs[:, None] + WIN_C)
    sel_r = jnp.asarray(dr[..., None] == np.arange(2 * WIN_R - 1), F32)
    sel_c = jnp.asarray(dc[..., None] == np.arange(2 * WIN_C - 1), F32)
    hi = lax.Precision.HIGHEST
    t = jnp.einsum("lhab,pja->lhpjb", rpb.astype(F32), sel_r, precision=hi)
    bias = jnp.einsum("lhpjb,qkb->lhpqjk", t, sel_c, precision=hi)
    bias = jnp.where(jnp.asarray(ok)[None, None, None, :, None, :], bias, NEG)
    return bias.reshape(rpb.shape[0], H_NA, N_BIAS_PATTERNS, GRID_W, wr * GRID_W)


def na_attention(qk, vbf, bias, layer, dims):
    rows = dims.seq // GRID_W
    wr = min(WIN_R, rows)
    qblocks = rows // NA_ROWS_PER_STEP
    tq = NA_ROWS_PER_STEP * GRID_W
    ctx_blk0 = dims.m_x // dims.ctx
    kern = functools.partial(_na_kernel, rows=rows)
    return pl.pallas_call(
        kern,
        out_shape=jax.ShapeDtypeStruct((dims.m_x, H_NA * DH_NA), BF16),
        grid=(dims.batch, H_NA, qblocks),
        in_specs=[
            pl.BlockSpec((tq, DH_NA), lambda b, h, i: (b * qblocks + i, h)),
            pl.BlockSpec((dims.seq, DH_NA), lambda b, h, i: (b, H_NA + h)),
            pl.BlockSpec((dims.seq, DH_NA), lambda b, h, i: (b, h)),
            pl.BlockSpec((dims.ctx, DH_NA), lambda b, h, i: (ctx_blk0 + b, H_NA + h)),
            pl.BlockSpec((dims.ctx, DH_NA), lambda b, h, i: (ctx_blk0 + b, h)),
            pl.BlockSpec((None, 1, N_BIAS_PATTERNS, GRID_W, wr * GRID_W), lambda b, h, i: (layer, h, 0, 0, 0)),
        ],
        out_specs=pl.BlockSpec((tq, DH_NA), lambda b, h, i: (b * qblocks + i, h)),
        compiler_params=_cparams(("arbitrary", "arbitrary", "arbitrary")), name="na_attention",
    )(qk, qk, vbf, qk, vbf, bias)


def _ctx_attn_kernel(q_ref, k_ref, v_ref, o_ref):
    s = lax.dot_general(q_ref[...], k_ref[...], NT_DIMS, preferred_element_type=F32) * (DH_NA ** -0.5)
    m = jnp.max(s, axis=-1, keepdims=True)
    p = jnp.exp(s - m)
    den = jnp.sum(p, axis=-1, keepdims=True)
    o = jnp.dot(p.astype(BF16), v_ref[...], preferred_element_type=F32)
    o_ref[...] = (o / den).astype(o_ref.dtype)


def ctx_attention(qk, vbf, dims):
    ctx_blk0 = dims.m_x // dims.ctx
    return pl.pallas_call(
        _ctx_attn_kernel,
        out_shape=jax.ShapeDtypeStruct((dims.batch * dims.ctx, H_NA * DH_NA), BF16),
        grid=(dims.batch, H_NA),
        in_specs=[
            pl.BlockSpec((dims.ctx, DH_NA), lambda b, h: (ctx_blk0 + b, h)),
            pl.BlockSpec((dims.ctx, DH_NA), lambda b, h: (ctx_blk0 + b, H_NA + h)),
            pl.BlockSpec((dims.ctx, DH_NA), lambda b, h: (ctx_blk0 + b, h)),
        ],
        out_specs=pl.BlockSpec((dims.ctx, DH_NA), lambda b, h: (b, h)),
        compiler_params=_cparams(("arbitrary", "arbitrary")), name="ctx_attention",
    )(qk, qk, vbf)


def _log_sigmoid(x):
    return jnp.minimum(x, 0.0) - jnp.log(1.0 + jnp.exp(-jnp.abs(x)))


def _rope(t, cos, sin_signed, even):
    swapped = jnp.where(even, pltpu.roll(t, DK_M - 1, 1), pltpu.roll(t, 1, 1))
    return t * cos + swapped * sin_signed


def _mlstm_kernel(q_ref, k_ref, v_ref, g_ref, gt_ref, cos_ref, sin_ref, o_ref, c_ref, m_ref):
    direction = pl.program_id(1)
    t_step = pl.program_id(2)

    @pl.when(t_step == 0)
    def _():
        c_ref[...] = jnp.zeros(c_ref.shape, F32)
        m_ref[...] = jnp.zeros(m_ref.shape, F32)

    ti = lax.broadcasted_iota(I32, (CHUNK, CHUNK), 0)
    si = lax.broadcasted_iota(I32, (CHUNK, CHUNK), 1)
    mask = (si - ti) * (1 - 2 * direction) <= 0
    tri = mask.astype(BF16)
    g = g_ref[0]
    gt = gt_ref[0]
    lf_hi, lf_lo = _split_bf16(_log_sigmoid(g))
    cum_col = (jnp.dot(tri, lf_hi, preferred_element_type=F32)
               + jnp.dot(tri, lf_lo, preferred_element_type=F32))
    lft = _log_sigmoid(gt)
    lft_hi, lft_lo = _split_bf16(lft)
    cum_row = (lax.dot_general(lft_hi, tri, NT_DIMS, preferred_element_type=F32)
               + lax.dot_general(lft_lo, tri, NT_DIMS, preferred_element_type=F32))
    cos = cos_ref[...]
    sin = sin_ref[...]
    even = (lax.broadcasted_iota(I32, (CHUNK, DK_M), 1) % 2) == 0
    ones_col = (lax.broadcasted_iota(I32, (CHUNK, DV_EXT - DV_M), 1) == 0).astype(BF16)
    for h in range(H_M):
        q = _rope(q_ref[:, h * DK_M:(h + 1) * DK_M], cos, sin, even)
        k = _rope(k_ref[:, h * DK_M:(h + 1) * DK_M], cos, sin, even) * (DK_M ** -0.5)
        v_ext = jnp.concatenate([v_ref[:, h * DV_M:(h + 1) * DV_M].astype(BF16), ones_col], axis=1)
        qb = q.astype(BF16)
        b_col = cum_col[:, H_M + h:H_M + h + 1]
        i_col = g[:, h:h + 1]
        b_row = cum_row[H_M + h:H_M + h + 1, :]
        i_row = gt[h:h + 1, :]
        b_last = jnp.sum(lft[H_M + h:H_M + h + 1, :], axis=1, keepdims=True)
        m_old = m_ref[h][0:1, 0:1]
        c_old = c_ref[h]
        d_end_row = b_last - b_row + i_row
        m_new = jnp.maximum(b_last + m_old, jnp.max(d_end_row, axis=1, keepdims=True))
        w_end = jnp.exp(b_last - b_col + i_col - m_new)
        decay = jnp.exp(b_last + m_old - m_new)
        kw_t = (k * w_end).T.astype(BF16)
        c_ref[h] = decay * c_old + jnp.dot(kw_t, v_ext, preferred_element_type=F32)
        m_ref[h] = jnp.broadcast_to(m_new, m_ref.shape[1:])
        d_mat = jnp.where(mask, b_col - b_row + i_row, NEG)
        m_in = b_col + m_old
        m_t = jnp.maximum(m_in, jnp.max(d_mat, axis=1, keepdims=True))
        s = lax.dot_general(qb, k.astype(BF16), NT_DIMS, preferred_element_type=F32) * jnp.exp(d_mat - m_t)
        a_in = jnp.exp(m_in - m_t)
        numden = (jnp.dot(s.astype(BF16), v_ext, preferred_element_type=F32)
                  + a_in * jnp.dot(qb, c_old.astype(BF16), preferred_element_type=F32))
        den = numden[:, DV_M:DV_M + 1]
        o_ref[0, :, h * DV_M:(h + 1) * DV_M] = numden[:, :DV_M] / jnp.maximum(jnp.abs(den), jnp.exp(-m_t))


def mlstm_scan(p, g_dir, gt_dir, cos_t, sin_t, dims):
    ncx = dims.ctx // CHUNK
    nlx = dims.seq // CHUNK
    x_blocks = dims.m_x // CHUNK

    def row_block(b, d, t):
        in_ctx = t < ncx
        cc = jnp.where(d == 0, t, ncx - 1 - t)
        cx = jnp.where(d == 0, t - ncx, nlx - 1 - (t - ncx))
        return jnp.where(in_ctx, x_blocks + b * ncx + cc, b * nlx + cx)

    def rope_block(b, d, t):
        in_ctx = t < ncx
        cx = jnp.where(d == 0, t - ncx, nlx - 1 - (t - ncx))
        return jnp.where(in_ctx, nlx, cx)

    return pl.pallas_call(
        _mlstm_kernel,
        out_shape=jax.ShapeDtypeStruct((2, dims.m_all, H_M * DV_M), F32),
        grid=(dims.batch, 2, ncx + nlx),
        in_specs=[
            pl.BlockSpec((CHUNK, H_M * DK_M), lambda b, d, t: (row_block(b, d, t), OFF_MQ // 512)),
            pl.BlockSpec((CHUNK, H_M * DK_M), lambda b, d, t: (row_block(b, d, t), OFF_MK // 512)),
            pl.BlockSpec((CHUNK, H_M * DV_M), lambda b, d, t: (row_block(b, d, t), OFF_MV // 1024)),
            pl.BlockSpec((1, CHUNK, LANES), lambda b, d, t: (d, row_block(b, d, t), 0)),
            pl.BlockSpec((1, 8, CHUNK), lambda b, d, t: (d, 0, row_block(b, d, t))),
            pl.BlockSpec((CHUNK, DK_M), lambda b, d, t: (rope_block(b, d, t), 0)),
            pl.BlockSpec((CHUNK, DK_M), lambda b, d, t: (rope_block(b, d, t), 0)),
        ],
        out_specs=pl.BlockSpec((1, CHUNK, H_M * DV_M), lambda b, d, t: (d, row_block(b, d, t), 0)),
        scratch_shapes=[pltpu.VMEM((H_M, DK_M, DV_EXT), F32), pltpu.VMEM((H_M, 8, LANES), F32)],
        compiler_params=_cparams(("arbitrary", "arbitrary", "arbitrary")), name="mlstm_scan",
    )(p, p, p, g_dir, gt_dir, cos_t, sin_t)


def _mlstm_out_kernel(h_ref, o_ref, g_ref, out_ref):
    for h in range(H_M):
        sl = slice(h * DV_M, (h + 1) * DV_M)
        y = _rms(h_ref[0, :, sl] + h_ref[1, :, sl], g_ref[:, sl])
        out_ref[:, sl] = (y * jax.nn.sigmoid(o_ref[:, sl])).astype(out_ref.dtype)


def mlstm_out(hdir, p, g, dims):
    m = dims.m_all
    tm = 256
    n = H_M * DV_M
    return pl.pallas_call(
        _mlstm_out_kernel,
        out_shape=jax.ShapeDtypeStruct((m, n), BF16),
        grid=(m // tm,),
        in_specs=[pl.BlockSpec((2, tm, n), lambda i: (0, i, 0)),
                  pl.BlockSpec((tm, n), lambda i: (i, OFF_MO // n)),
                  pl.BlockSpec((1, n), lambda i: (0, 0))],
        out_specs=pl.BlockSpec((tm, n), lambda i: (i, 0)),
        compiler_params=_cparams(("arbitrary",)), name="mlstm_out",
    )(hdir, p, g.reshape(1, n))


def rope_tables(dims):
    t = jnp.arange(dims.seq)
    row = (t // GRID_W).astype(F32)
    col = (t % GRID_W).astype(F32)
    quarter = DK_M // 4
    inv = ROPE_BASE ** (-jnp.arange(quarter, dtype=F32) / quarter)
    ang = jnp.concatenate([row[:, None] * inv, col[:, None] * inv], axis=-1)
    cos = jnp.repeat(jnp.cos(ang), 2, axis=-1)
    sin = jnp.repeat(jnp.sin(ang), 2, axis=-1) * jnp.tile(jnp.array([-1.0, 1.0], F32), DK_M // 2)
    cos = jnp.concatenate([cos, jnp.ones((CHUNK, DK_M), F32)], axis=0)
    sin = jnp.concatenate([sin, jnp.zeros((CHUNK, DK_M), F32)], axis=0)
    return cos, sin


def _gather_kernel(src_ref, h_hbm, o_ref, buf_ref, sem, *, tm):
    i = pl.program_id(0)

    def start(r, carry):
        pltpu.make_async_copy(h_hbm.at[pl.ds(src_ref[i * tm + r], 1)], buf_ref.at[pl.ds(r, 1)], sem).start()
        return carry

    lax.fori_loop(0, tm, start, 0)

    def wait(r, carry):
        pltpu.make_async_copy(h_hbm.at[pl.ds(0, 1)], buf_ref.at[pl.ds(r, 1)], sem).wait()
        return carry

    lax.fori_loop(0, tm, wait, 0)
    o_ref[...] = buf_ref[...].astype(o_ref.dtype)


def gather_rows(h, src, *, tm):
    m, d = h.shape
    r_total = src.shape[0]
    return pl.pallas_call(
        functools.partial(_gather_kernel, tm=tm),
        out_shape=jax.ShapeDtypeStruct((r_total, d), BF16),
        grid_spec=pltpu.PrefetchScalarGridSpec(
            num_scalar_prefetch=1,
            grid=(r_total // tm,),
            in_specs=[pl.BlockSpec(memory_space=pl.ANY)],
            out_specs=pl.BlockSpec((tm, d), lambda i, s: (i, 0)),
            scratch_shapes=[pltpu.VMEM((tm, d), F32), pltpu.SemaphoreType.DMA],
        ),
        compiler_params=_cparams(("arbitrary",)), name="moe_gather",
    )(src, h)


def _combine_kernel(p1_ref, p2_ref, y_hbm, x_ref, gate_ref, o_ref, y1_ref, y2_ref, sem, *, tm, seq, n_groups):
    i = pl.program_id(0)

    def start(r, carry):
        t = i * tm + r
        pltpu.make_async_copy(y_hbm.at[pl.ds(p1_ref[t], 1)], y1_ref.at[pl.ds(r, 1)], sem.at[0]).start()
        pltpu.make_async_copy(y_hbm.at[pl.ds(p2_ref[t], 1)], y2_ref.at[pl.ds(r, 1)], sem.at[1]).start()
        return carry

    lax.fori_loop(0, tm, start, 0)

    def wait(r, carry):
        pltpu.make_async_copy(y_hbm.at[pl.ds(0, 1)], y1_ref.at[pl.ds(r, 1)], sem.at[0]).wait()
        pltpu.make_async_copy(y_hbm.at[pl.ds(0, 1)], y2_ref.at[pl.ds(r, 1)], sem.at[1]).wait()
        return carry

    lax.fori_loop(0, tm, wait, 0)
    row0 = i * tm
    grp = jnp.int32(n_groups - 1)
    for b in reversed(range(n_groups - 1)):
        grp = jnp.where(row0 < (b + 1) * seq, b, grp)
    gate = gate_ref[pl.ds(grp, 1), :]
    o_ref[...] = x_ref[...] + gate * (y1_ref[...] + y2_ref[...])


def moe_combine(xs, y, pos1, pos2, mod, layer, gate_chunk, dims, *, m):
    d = dims.d_model
    tm = _pick_tile(math.gcd(dims.seq, dims.ctx), (256, 128))
    kern = functools.partial(_combine_kernel, tm=tm, seq=dims.seq, n_groups=dims.batch + 1)
    return pl.pallas_call(
        kern,
        out_shape=jax.ShapeDtypeStruct((m, d), F32),
        grid_spec=pltpu.PrefetchScalarGridSpec(
            num_scalar_prefetch=2,
            grid=(m // tm,),
            in_specs=[pl.BlockSpec(memory_space=pl.ANY),
                      pl.BlockSpec((tm, d), lambda i, a, b: (i, 0)),
                      pl.BlockSpec((8, d), lambda i, a, b: (MOD_ROWS // 8 * layer, gate_chunk))],
            out_specs=pl.BlockSpec((tm, d), lambda i, a, b: (i, 0)),
            scratch_shapes=[pltpu.VMEM((tm, d), F32), pltpu.VMEM((tm, d), F32),
                            pltpu.SemaphoreType.DMA((2,))],
        ),
        compiler_params=_cparams(("arbitrary",)), name="moe_combine",
    )(pos1, pos2, y, xs, mod)


def moe_dispatch_plan(route, m, tm):
    e = jnp.concatenate([route[0, :m], route[1, :m]]).astype(I32)
    w = jnp.concatenate([route[2, :m], route[3, :m]])
    tok = jnp.tile(jnp.arange(m, dtype=I32), 2)
    onehot = (e[:, None] == jnp.arange(N_EXPERTS, dtype=I32)[None, :]).astype(I32)
    rank = jnp.sum((jnp.cumsum(onehot, axis=0) - onehot) * onehot, axis=1)
    counts = jnp.sum(onehot, axis=0)
    padded = ((counts + tm - 1) // tm) * tm
    ends = jnp.cumsum(padded)
    offs = ends - padded
    pos = offs[e] + rank
    r_total = TOP_K * m + N_EXPERTS * tm
    src = jnp.zeros((r_total,), I32).at[pos].set(tok)
    scale = jnp.zeros((r_total,), F32).at[pos].set(w)
    tile_start = jnp.arange(r_total // tm, dtype=I32) * tm
    tile_e = jnp.minimum(jnp.sum((tile_start[:, None] >= ends[None, :]).astype(I32), axis=1), N_EXPERTS - 1)
    n_valid = (ends[-1] // tm).astype(I32).reshape(1)
    return src, scale, pos[:m], pos[m:], tile_e, n_valid


def _dense_tile(m, candidates):
    return _pick_tile(m, candidates)


def kernel(x, c, ctx, c_ctx, ada_w, ada_b, norm1_g, norm2_g, w_in, gate_b, na_q_g, na_k_g, na_rpb, m_norm_g, w_out, ffn_w_gate, ffn_w_up, ffn_w_down, moe_router, moe_w_gate, moe_w_up, moe_w_down):
    batch, seq, d = x.shape
    dims = Dims(batch=batch, seq=seq, ctx=ctx.shape[1], d_model=d, d_ff=ffn_w_gate.shape[2], depth=ada_w.shape[0])
    depth = dims.depth
    n_groups = batch + 1
    rows = seq // GRID_W
    assert rows >= 8 and rows % NA_ROWS_PER_STEP == 0 and batch + 1 <= 8
    assert seq % 256 == 0 and dims.ctx % CHUNK == 0 and dims.m_x % dims.ctx == 0

    xs = jnp.concatenate([x.reshape(dims.m_x, d), ctx.reshape(batch * dims.ctx, d)], axis=0)

    cond = jnp.concatenate([jax.nn.silu(c), jax.nn.silu(c_ctx)[None], jnp.zeros((MOD_ROWS - n_groups, d), F32)], axis=0)
    tn_ada = _pick_tile(6 * d, (1024, 512, 256))
    tn_d = min(512, d)
    mod = gmm(cond.astype(BF16), ada_w, jnp.arange(depth, dtype=I32), jnp.full((1,), depth, I32),
              tm=MOD_ROWS, tn=tn_ada, n_out=6 * d, out_dtype=F32, epilogue="bias",
              extras=(ada_b.reshape(depth, 1, 6 * d),),
              extra_specs=(((1, 1, tn_ada), lambda j, i, g, nv: (g[i], 0, j)),),
              a_fixed=True, name="ada_mod")

    cos_t, sin_t = rope_tables(dims)
    na_bias = na_bias_table(na_rpb, rows)
    w_gate_cols = jnp.pad(w_in[:, :, N_MAIN:], ((0, 0), (0, 0), (0, LANES - 4 * H_M)))
    gate_b_pad = jnp.pad(gate_b, ((0, 0), (0, LANES - 4 * H_M)))

    def dense_groups(m, tm, g):
        nt = m // tm
        return jnp.full((nt,), g, I32), jnp.full((1,), nt, I32)

    for layer in range(depth):
        last = layer == depth - 1
        m_out = dims.m_x if last else dims.m_all

        hbf = norm_mod(xs, norm1_g[layer], mod, layer, 0, 1, dims, m=dims.m_all)
        tm_a = _dense_tile(dims.m_all, (2176, 1088, 640, 512, 256))
        grp, nv = dense_groups(dims.m_all, tm_a, layer)
        p = gmm(hbf, w_in, grp, nv, tm=tm_a, tn=512, n_out=N_MAIN, out_dtype=F32, rows_outer=True,
                name="w_in_proj")
        gates = gmm(hbf, w_gate_cols, grp, nv, tm=tm_a, tn=LANES, n_out=LANES, out_dtype=F32, name="gate_proj")
        gates = gates + gate_b_pad[layer][None, :]
        g_dir = jnp.stack([gates, jnp.roll(gates, -2 * H_M, axis=1)])
        gt_dir = jnp.stack([gates[:, :2 * H_M].T, gates[:, 2 * H_M:4 * H_M].T])

        qk, vbf = qkv_prep(p, na_q_g[layer], na_k_g[layer], dims)
        na_x = na_attention(qk, vbf, na_bias, layer, dims)
        hdir = mlstm_scan(p, g_dir, gt_dir, cos_t, sin_t, dims)
        m_all_out = mlstm_out(hdir, p, m_norm_g[layer], dims)
        if last:
            mix = jnp.concatenate([na_x, m_all_out[:dims.m_x]], axis=1)
        else:
            na_c = ctx_attention(qk, vbf, dims)
            mix = jnp.concatenate([jnp.concatenate([na_x, na_c], axis=0), m_all_out], axis=1)

        tm_o = _dense_tile(m_out, (1088, 1024, 640, 512, 256))
        grp, nv = dense_groups(m_out, tm_o, layer)
        res_specs = lambda tm, tn, chunk: (
            ((tm, tn), lambda j, i, g, nv: (i, j)),
            ((8, tn), lambda j, i, g, nv: (MOD_ROWS // 8 * layer, chunk * (d // tn) + j)))
        xs = gmm(mix, w_out, grp, nv, tm=tm_o, tn=tn_d, n_out=d, out_dtype=F32, epilogue="residual",
                 extras=(xs, mod), extra_specs=res_specs(tm_o, tn_d, 2), seq=seq, n_groups=n_groups,
                 name="w_out_proj")

        idx = layer // 2
        if layer % 2 == 0:
            h2 = norm_mod(xs, norm2_g[layer], mod, layer, 3, 4, dims, m=m_out)
            tm_u = _dense_tile(m_out, (1088, 1024, 640, 512, 256))
            grp, nv = dense_groups(m_out, tm_u, idx)
            act = gmm_swiglu(h2, ffn_w_gate, ffn_w_up, grp, nv, tm=tm_u, tn=512, name="ffn_up")
            tm_d = _dense_tile(m_out, (544, 512, 320, 256))
            grp, nv = dense_groups(m_out, tm_d, idx)
            xs = gmm(act, ffn_w_down, grp, nv, tm=tm_d, tn=tn_d, n_out=d, out_dtype=F32, epilogue="residual",
                     extras=(xs, mod), extra_specs=res_specs(tm_d, tn_d, 5), seq=seq, n_groups=n_groups,
                     name="ffn_down")
        else:
            h2, route = norm_mod(xs, norm2_g[layer], mod, layer, 3, 4, dims, m=m_out,
                                 router_t=moe_router[idx].T)
            tm_e = 512 if m_out >= 4096 else 128
            src, scale, pos1, pos2, tile_e, n_valid = moe_dispatch_plan(route, m_out, tm_e)
            hs = gather_rows(h2, src, tm=tm_e)
            grp = tile_e + idx * N_EXPERTS
            nw = moe_w_gate.shape[0] * N_EXPERTS
            act = gmm_swiglu(hs, moe_w_gate.reshape(nw, d, dims.d_ff), moe_w_up.reshape(nw, d, dims.d_ff),
                             grp, n_valid, tm=tm_e, tn=512, name="moe_up")
            scale_b = jnp.broadcast_to(scale[:, None], (scale.shape[0], LANES))
            y = gmm(act, moe_w_down.reshape(nw, dims.d_ff, d), grp, n_valid, tm=tm_e, tn=tn_d, n_out=d,
                    out_dtype=F32, epilogue="rowscale", extras=(scale_b,),
                    extra_specs=(((tm_e, LANES), lambda j, i, g, nv: (i, 0)),), name="moe_down")
            xs = moe_combine(xs, y, pos1, pos2, mod, layer, 5, dims, m=m_out)

    return xs[:dims.m_x].reshape(batch, seq, d)
```

```python
import functools
import math
from typing import NamedTuple

import jax
import jax.numpy as jnp
import numpy as np
from jax import lax
from jax.experimental import pallas as pl
from jax.experimental.pallas import tpu as pltpu

F32 = jnp.float32
BF16 = jnp.bfloat16
I32 = jnp.int32

GRID_W = 64
H_NA = 8
DH_NA = 128
WIN_R = 8
WIN_C = 16
H_M = 4
DK_M = 128
DV_M = 256
CHUNK = 128
ROPE_BASE = 10000.0
N_EXPERTS = 8
TOP_K = 2
EPS = 1e-6
D_MIX = 2048
NEG = -1e30

OFF_Q, OFF_K, OFF_V = 0, 1024, 2048
OFF_MQ, OFF_MK, OFF_MV, OFF_MO = 3072, 3584, 4096, 5120
N_MAIN = 6144
DV_EXT = DV_M + 128

LANES = 128
VMEM_LIMIT = 56 * 1024 * 1024
MOD_ROWS = 16


class Dims(NamedTuple):
    batch: int
    seq: int
    ctx: int
    d_model: int
    d_ff: int
    depth: int

    @property
    def m_x(self):
        return self.batch * self.seq

    @property
    def m_all(self):
        return self.batch * (self.seq + self.ctx)


def _cparams(sem):
    return pltpu.CompilerParams(dimension_semantics=sem, vmem_limit_bytes=VMEM_LIMIT)


def _group_select(row, vals, seq):
    out = vals[-1]
    for b in reversed(range(len(vals) - 1)):
        out = jnp.where(row < (b + 1) * seq, vals[b], out)
    return out


def _pick_tile(m, candidates):
    for t in candidates:
        if m % t == 0:
            return t
    raise ValueError(f"no row tile for {m}")


N_EXTRA = {"residual": 2, "rowscale": 1, "bias": 1, "swiglu": 0, None: 0}


def _gmm_kernel(grp_ref, nv_ref, rid_ref, rgrp_ref, nruns_ref, a_ref, *refs,
                n_w, epilogue, tm, tn, nj, seq, n_groups, rows_outer):
    w_refs, refs = refs[:n_w], refs[n_w:]
    x_refs, refs = refs[:N_EXTRA[epilogue]], refs[N_EXTRA[epilogue]:]
    o_ref, refs = refs[0], refs[1:]
    wbf_refs, refs = refs[:n_w], refs[n_w:]
    if rows_outer:
        i = pl.program_id(0)
        for w_ref, wbf_ref in zip(w_refs, wbf_refs):
            wbf_ref[...] = w_ref[0].astype(BF16)
    else:
        wst_refs, sem = refs[:n_w], refs[n_w]
        j = pl.program_id(0)
        i = pl.program_id(1)
        rid = rid_ref[i]
        nruns = nruns_ref[0]

        def tile_copy(widx, g, jj, slot):
            src = w_refs[widx].at[g, :, pl.ds(pl.multiple_of(jj * tn, tn), tn)]
            return pltpu.make_async_copy(src, wst_refs[widx].at[slot], sem.at[widx, slot])

        @pl.when((i == 0) | (rid != rid_ref[jnp.maximum(i - 1, 0)]))
        def _():
            q = j * nruns + rid
            slot = q % 2

            @pl.when(q == 0)
            def _():
                for widx in range(n_w):
                    tile_copy(widx, rgrp_ref[0], 0, 0).start()

            for widx in range(n_w):
                tile_copy(widx, 0, 0, slot).wait()
            wrap = rid + 1 == nruns
            next_run = jnp.where(wrap, 0, rid + 1)
            next_j = j + wrap.astype(I32)

            @pl.when(next_j < nj)
            def _():
                for widx in range(n_w):
                    tile_copy(widx, rgrp_ref[next_run], next_j, 1 - slot).start()

            for widx in range(n_w):
                wbf_refs[widx][...] = wst_refs[widx][slot].astype(BF16)

    @pl.when(i < nv_ref[0])
    def _():
        a = a_ref[...]
        acc = jnp.dot(a, wbf_refs[0][...], preferred_element_type=F32)
        if epilogue == "residual":
            res_ref, gate_ref = x_refs
            row = i * tm + lax.broadcasted_iota(I32, (tm, 1), 0)
            gate = _group_select(row, [gate_ref[k:k + 1, :] for k in range(n_groups)], seq)
            acc = res_ref[...] + gate * acc
        elif epilogue == "rowscale":
            acc = acc * x_refs[0][:, 0:1]
        elif epilogue == "bias":
            acc = acc + x_refs[0][0]
        elif epilogue == "swiglu":
            acc = acc * jax.nn.sigmoid(acc) * jnp.dot(a, wbf_refs[1][...], preferred_element_type=F32)
        o_ref[...] = acc.astype(o_ref.dtype)

    @pl.when(i >= nv_ref[0])
    def _():
        o_ref[...] = jnp.zeros(o_ref.shape, o_ref.dtype)


def _runs(grp):
    nt = grp.shape[0]
    first = jnp.concatenate([jnp.ones((1,), I32), (grp[1:] != grp[:-1]).astype(I32)])
    run_id = jnp.cumsum(first) - 1
    sel = (run_id[:, None] == jnp.arange(nt, dtype=I32)[None, :]).astype(I32) * first[:, None]
    run_grp = jnp.sum(sel * grp[:, None], axis=0)
    return run_id.astype(I32), run_grp.astype(I32), (run_id[-1:] + 1).astype(I32)


def gmm(a, ws, grp, nvalid, *, tm, tn, n_out, out_dtype, epilogue=None, extras=(), extra_specs=(),
        a_fixed=False, seq=0, n_groups=0, rows_outer=False, name="gmm"):
    k = a.shape[1]
    nt = grp.shape[0]
    n_w = len(ws)
    nj = n_out // tn

    def spec(shape, fn):
        return pl.BlockSpec(shape, (lambda i, j, *s: fn(j, i, *s)) if rows_outer else fn)

    a_map = (lambda j, i, *s: (0, 0)) if a_fixed else (lambda j, i, *s: (i, 0))
    if rows_outer:
        w_specs = [spec((1, k, tn), lambda j, i, g, *s: (g[i], 0, j))] * n_w
        stream_scratch = []
    else:
        w_specs = [pl.BlockSpec(memory_space=pl.ANY)] * n_w
        stream_scratch = [pltpu.VMEM((2, k, tn), F32)] * n_w + [pltpu.SemaphoreType.DMA((n_w, 2))]
    in_specs = [spec((tm, k), a_map)] + w_specs + [spec(shape, fn) for shape, fn in extra_specs]
    kern = functools.partial(_gmm_kernel, n_w=n_w, epilogue=epilogue, tm=tm, tn=tn, nj=nj, seq=seq,
                             n_groups=n_groups, rows_outer=rows_outer)
    run_id, run_grp, nruns = _runs(grp)
    return pl.pallas_call(
        kern,
        out_shape=jax.ShapeDtypeStruct((nt * tm, n_out), out_dtype),
        grid_spec=pltpu.PrefetchScalarGridSpec(
            num_scalar_prefetch=5,
            grid=(nt, nj) if rows_outer else (nj, nt),
            in_specs=in_specs,
            out_specs=spec((tm, tn), lambda j, i, *s: (i, j)),
            scratch_shapes=[pltpu.VMEM((k, tn), BF16)] * n_w + stream_scratch,
        ),
        compiler_params=_cparams(("arbitrary", "arbitrary")),
        name=name,
    )(grp, nvalid, run_id, run_grp, nruns, a, *ws, *extras)


def gmm_swiglu(a, wg, wu, grp, nvalid, *, tm, tn, name="gmm_swiglu"):
    return gmm(a, (wg, wu), grp, nvalid, tm=tm, tn=tn, n_out=wg.shape[2], out_dtype=BF16,
               epilogue="swiglu", name=name)


def _rms(x, g):
    return x * lax.rsqrt(jnp.mean(x * x, axis=-1, keepdims=True) + EPS) * g


def _split_bf16(x):
    hi = x.astype(BF16)
    lo = (x - hi.astype(F32)).astype(BF16)
    return hi, lo


def _norm_mod_kernel(x_ref, g_ref, shift_ref, scale_ref, *refs, tm, seq, n_groups, with_router):
    if with_router:
        r_ref, o_ref, route_ref = refs
    else:
        (o_ref,) = refs
    i = pl.program_id(0)
    row0 = i * tm
    grp = jnp.int32(n_groups - 1)
    for b in reversed(range(n_groups - 1)):
        grp = jnp.where(row0 < (b + 1) * seq, b, grp)
    y = _rms(x_ref[...], g_ref[...])
    h = y * (1.0 + scale_ref[pl.ds(grp, 1), :]) + shift_ref[pl.ds(grp, 1), :]
    o_ref[...] = h.astype(o_ref.dtype)
    if with_router:
        nt_dims = (((1,), (1,)), ((), ()))
        h_hi, h_lo = _split_bf16(h)
        r_hi, r_lo = _split_bf16(r_ref[...])
        logits = (lax.dot_general(r_hi, h_hi, nt_dims, preferred_element_type=F32)
                  + lax.dot_general(r_hi, h_lo, nt_dims, preferred_element_type=F32)
                  + lax.dot_general(r_lo, h_hi, nt_dims, preferred_element_type=F32))
        ids = lax.broadcasted_iota(I32, logits.shape, 0)
        m1 = jnp.max(logits, axis=0, keepdims=True)
        i1 = jnp.min(jnp.where(logits == m1, ids, N_EXPERTS), axis=0, keepdims=True)
        rest = jnp.where(ids == i1, -jnp.inf, logits)
        m2 = jnp.max(rest, axis=0, keepdims=True)
        i2 = jnp.min(jnp.where(rest == m2, ids, N_EXPERTS), axis=0, keepdims=True)
        e = jnp.exp(m2 - m1)
        w1 = 1.0 / (1.0 + e)
        w2 = e / (1.0 + e)
        out_row = lax.broadcasted_iota(I32, (8, tm), 0)
        route_ref[...] = jnp.where(out_row == 0, i1.astype(F32), jnp.where(
            out_row == 1, i2.astype(F32), jnp.where(out_row == 2, w1, jnp.where(out_row == 3, w2, 0.0))))


def norm_mod(xs, g, mod, layer, shift_chunk, scale_chunk, dims, *, m, router_t=None):
    d = dims.d_model
    tm = _pick_tile(math.gcd(dims.seq, dims.ctx), (256, 128))
    with_router = router_t is not None
    kern = functools.partial(_norm_mod_kernel, tm=tm, seq=dims.seq, n_groups=dims.batch + 1,
                             with_router=with_router)
    in_specs = [
        pl.BlockSpec((tm, d), lambda i: (i, 0)),
        pl.BlockSpec((1, d), lambda i: (0, 0)),
        pl.BlockSpec((8, d), lambda i: (MOD_ROWS // 8 * layer, shift_chunk)),
        pl.BlockSpec((8, d), lambda i: (MOD_ROWS // 8 * layer, scale_chunk)),
    ]
    args = [xs, g.reshape(1, d), mod, mod]
    out_shape = [jax.ShapeDtypeStruct((m, d), F32 if with_router else BF16)]
    out_specs = [pl.BlockSpec((tm, d), lambda i: (i, 0))]
    if with_router:
        in_specs.append(pl.BlockSpec((N_EXPERTS, d), lambda i: (0, 0)))
        args.append(router_t)
        out_shape.append(jax.ShapeDtypeStruct((8, m), F32))
        out_specs.append(pl.BlockSpec((8, tm), lambda i: (0, i)))
    res = pl.pallas_call(
        kern, out_shape=out_shape, grid=(m // tm,), in_specs=in_specs, out_specs=out_specs,
        compiler_params=_cparams(("arbitrary",)), name="norm_mod_router" if with_router else "norm_mod",
    )(*args)
    return res if with_router else res[0]


def _qkv_prep_kernel(qk_ref, v_ref, qg_ref, kg_ref, qk_out, v_out):
    for h in range(2 * H_NA):
        g = qg_ref[...] * (DH_NA ** -0.5) if h < H_NA else kg_ref[...]
        sl = slice(h * DH_NA, (h + 1) * DH_NA)
        qk_out[:, sl] = _rms(qk_ref[:, sl], g).astype(BF16)
    v_out[...] = v_ref[...].astype(BF16)


def qkv_prep(p, qg, kg, dims):
    m = dims.m_all
    tm = 256
    return pl.pallas_call(
        _qkv_prep_kernel,
        out_shape=[jax.ShapeDtypeStruct((m, 2 * H_NA * DH_NA), BF16),
                   jax.ShapeDtypeStruct((m, H_NA * DH_NA), BF16)],
        grid=(m // tm,),
        in_specs=[pl.BlockSpec((tm, 2048), lambda i: (i, 0)),
                  pl.BlockSpec((tm, 1024), lambda i: (i, OFF_V // 1024)),
                  pl.BlockSpec((1, DH_NA), lambda i: (0, 0)),
                  pl.BlockSpec((1, DH_NA), lambda i: (0, 0))],
        out_specs=[pl.BlockSpec((tm, 2048), lambda i: (i, 0)),
                   pl.BlockSpec((tm, 1024), lambda i: (i, 0))],
        compiler_params=_cparams(("arbitrary",)), name="qkv_prep",
    )(p, p, qg.reshape(1, DH_NA), kg.reshape(1, DH_NA))


NT_DIMS = (((1,), (1,)), ((), ()))
NA_QROWS = 4
NA_KROWS = NA_QROWS + WIN_R
N_BIAS_PATTERNS = 3


def _na_block_start(qb, rows):
    return jnp.clip(qb * NA_QROWS - WIN_R // 2, 0, rows - NA_KROWS)


def _na_kernel(q_ref, k_ref, v_ref, kc_ref, vc_ref, bias_ref, o_ref, *, rows):
    qb = pl.program_id(2)
    nblk = rows // NA_QROWS
    pid = jnp.where(qb == 0, 0, jnp.where(qb == nblk - 1, 2, 1))
    start = pl.multiple_of(_na_block_start(qb, rows) * GRID_W, GRID_W)
    q = q_ref[...]
    kb = k_ref[pl.ds(start, NA_KROWS * GRID_W), :]
    vb = v_ref[pl.ds(start, NA_KROWS * GRID_W), :]
    s = lax.dot_general(q, kb, NT_DIMS, preferred_element_type=F32) + bias_ref[0, pid]
    sc = lax.dot_general(q, kc_ref[...], NT_DIMS, preferred_element_type=F32)
    m = jnp.maximum(jnp.max(s, axis=-1, keepdims=True), jnp.max(sc, axis=-1, keepdims=True))
    pb = jnp.exp(s - m)
    pc = jnp.exp(sc - m)
    den = jnp.sum(pb, axis=-1, keepdims=True) + jnp.sum(pc, axis=-1, keepdims=True)
    o = (jnp.dot(pb.astype(BF16), vb, preferred_element_type=F32)
         + jnp.dot(pc.astype(BF16), vc_ref[...], preferred_element_type=F32))
    o_ref[...] = (o / den).astype(o_ref.dtype)


def na_bias_table(rpb, rows):
    nblk = rows // NA_QROWS
    qb = np.array([0, 2, nblk - 1])
    ks = np.clip(qb * NA_QROWS - WIN_R // 2, 0, rows - NA_KROWS)
    r = qb[:, None] * NA_QROWS + np.arange(NA_QROWS)[None, :]
    rs = np.clip(r - WIN_R // 2, 0, rows - WIN_R)
    kr = ks[:, None] + np.arange(NA_KROWS)[None, :]
    in_band = (kr[:, None, :] >= rs[:, :, None]) & (kr[:, None, :] < rs[:, :, None] + WIN_R)
    dr = kr[:, None, :] - r[:, :, None] + (WIN_R - 1)
    c = np.arange(GRID_W)
    dc = np.clip(c[None, :] - c[:, None], -(WIN_C - 1), WIN_C - 1) + (WIN_C - 1)
    cs = np.clip(c - WIN_C // 2, 0, GRID_W - WIN_C)
    col_ok = (c[None, :] >= cs[:, None]) & (c[None, :] < cs[:, None] + WIN_C)
    sel_r = jnp.asarray((dr[..., None] == np.arange(2 * WIN_R - 1)) & in_band[..., None], F32)
    sel_c = jnp.asarray(dc[..., None] == np.arange(2 * WIN_C - 1), F32)
    hi = lax.Precision.HIGHEST
    t = jnp.einsum("lhab,pxja->lhpxjb", rpb.astype(F32), sel_r, precision=hi)
    bias = jnp.einsum("lhpxjb,qkb->lhpxqjk", t, sel_c, precision=hi)
    ok = in_band[:, :, None, :, None] & col_ok[None, None, :, None, :]
    bias = jnp.where(jnp.asarray(ok)[None, None], bias, NEG)
    return bias.reshape(rpb.shape[0], H_NA, N_BIAS_PATTERNS, NA_QROWS * GRID_W, NA_KROWS * GRID_W)


def na_attention(qk, vbf, bias, layer, dims):
    rows = dims.seq // GRID_W
    qblocks = rows // NA_QROWS
    tq = NA_QROWS * GRID_W
    ctx_blk0 = dims.m_x // dims.ctx
    kern = functools.partial(_na_kernel, rows=rows)
    return pl.pallas_call(
        kern,
        out_shape=jax.ShapeDtypeStruct((dims.m_x, H_NA * DH_NA), BF16),
        grid=(dims.batch, H_NA, qblocks),
        in_specs=[
            pl.BlockSpec((tq, DH_NA), lambda b, h, i: (b * qblocks + i, h)),
            pl.BlockSpec((dims.seq, DH_NA), lambda b, h, i: (b, H_NA + h)),
            pl.BlockSpec((dims.seq, DH_NA), lambda b, h, i: (b, h)),
            pl.BlockSpec((dims.ctx, DH_NA), lambda b, h, i: (ctx_blk0 + b, H_NA + h)),
            pl.BlockSpec((dims.ctx, DH_NA), lambda b, h, i: (ctx_blk0 + b, h)),
            pl.BlockSpec((None, 1, N_BIAS_PATTERNS, tq, NA_KROWS * GRID_W), lambda b, h, i: (layer, h, 0, 0, 0)),
        ],
        out_specs=pl.BlockSpec((tq, DH_NA), lambda b, h, i: (b * qblocks + i, h)),
        compiler_params=_cparams(("arbitrary", "arbitrary", "arbitrary")), name="na_attention",
    )(qk, qk, vbf, qk, vbf, bias)


def _ctx_attn_kernel(q_ref, k_ref, v_ref, o_ref):
    s = lax.dot_general(q_ref[...], k_ref[...], NT_DIMS, preferred_element_type=F32)
    m = jnp.max(s, axis=-1, keepdims=True)
    p = jnp.exp(s - m)
    den = jnp.sum(p, axis=-1, keepdims=True)
    o = jnp.dot(p.astype(BF16), v_ref[...], preferred_element_type=F32)
    o_ref[...] = (o / den).astype(o_ref.dtype)


def ctx_attention(qk, vbf, dims):
    ctx_blk0 = dims.m_x // dims.ctx
    return pl.pallas_call(
        _ctx_attn_kernel,
        out_shape=jax.ShapeDtypeStruct((dims.batch * dims.ctx, H_NA * DH_NA), BF16),
        grid=(dims.batch, H_NA),
        in_specs=[
            pl.BlockSpec((dims.ctx, DH_NA), lambda b, h: (ctx_blk0 + b, h)),
            pl.BlockSpec((dims.ctx, DH_NA), lambda b, h: (ctx_blk0 + b, H_NA + h)),
            pl.BlockSpec((dims.ctx, DH_NA), lambda b, h: (ctx_blk0 + b, h)),
        ],
        out_specs=pl.BlockSpec((dims.ctx, DH_NA), lambda b, h: (b, h)),
        compiler_params=_cparams(("arbitrary", "arbitrary")), name="ctx_attention",
    )(qk, qk, vbf)


def _log_sigmoid(x):
    return jnp.minimum(x, 0.0) - jnp.log(1.0 + jnp.exp(-jnp.abs(x)))


def _rope(t, cos, sin_signed, even):
    swapped = jnp.where(even, pltpu.roll(t, DK_M - 1, 1), pltpu.roll(t, 1, 1))
    return t * cos + swapped * sin_signed


def _mlstm_kernel(q_ref, k_ref, v_ref, g_ref, gt_ref, cos_ref, sin_ref, o_ref, c_ref, m_ref):
    direction = pl.program_id(1)
    t_step = pl.program_id(2)

    @pl.when(t_step == 0)
    def _():
        c_ref[...] = jnp.zeros(c_ref.shape, F32)
        m_ref[...] = jnp.zeros(m_ref.shape, F32)

    ti = lax.broadcasted_iota(I32, (CHUNK, CHUNK), 0)
    si = lax.broadcasted_iota(I32, (CHUNK, CHUNK), 1)
    mask = (si - ti) * (1 - 2 * direction) <= 0
    tri = mask.astype(BF16)
    g = g_ref[0]
    gt = gt_ref[0]
    lf_hi, lf_lo = _split_bf16(_log_sigmoid(g))
    cum_col = (jnp.dot(tri, lf_hi, preferred_element_type=F32)
               + jnp.dot(tri, lf_lo, preferred_element_type=F32))
    lft = _log_sigmoid(gt)
    lft_hi, lft_lo = _split_bf16(lft)
    cum_row = (lax.dot_general(lft_hi, tri, NT_DIMS, preferred_element_type=F32)
               + lax.dot_general(lft_lo, tri, NT_DIMS, preferred_element_type=F32))
    cos = cos_ref[...]
    sin = sin_ref[...]
    even = (lax.broadcasted_iota(I32, (CHUNK, DK_M), 1) % 2) == 0
    ones_col = (lax.broadcasted_iota(I32, (CHUNK, DV_EXT - DV_M), 1) == 0).astype(BF16)
    for h in range(H_M):
        q = _rope(q_ref[:, h * DK_M:(h + 1) * DK_M], cos, sin, even)
        k = _rope(k_ref[:, h * DK_M:(h + 1) * DK_M], cos, sin, even) * (DK_M ** -0.5)
        v_ext = jnp.concatenate([v_ref[:, h * DV_M:(h + 1) * DV_M].astype(BF16), ones_col], axis=1)
        qb = q.astype(BF16)
        b_col = cum_col[:, H_M + h:H_M + h + 1]
        i_col = g[:, h:h + 1]
        b_row = cum_row[H_M + h:H_M + h + 1, :]
        i_row = gt[h:h + 1, :]
        b_last = jnp.sum(lft[H_M + h:H_M + h + 1, :], axis=1, keepdims=True)
        m_old = m_ref[h][0:1, 0:1]
        c_old = c_ref[h]
        d_end_row = b_last - b_row + i_row
        m_new = jnp.maximum(b_last + m_old, jnp.max(d_end_row, axis=1, keepdims=True))
        w_end = jnp.exp(b_last - b_col + i_col - m_new)
        decay = jnp.exp(b_last + m_old - m_new)
        kw_t = (k * w_end).T.astype(BF16)
        c_ref[h] = decay * c_old + jnp.dot(kw_t, v_ext, preferred_element_type=F32)
        m_ref[h] = jnp.broadcast_to(m_new, m_ref.shape[1:])
        d_mat = jnp.where(mask, b_col - b_row + i_row, NEG)
        m_in = b_col + m_old
        m_t = jnp.maximum(m_in, jnp.max(d_mat, axis=1, keepdims=True))
        s = lax.dot_general(qb, k.astype(BF16), NT_DIMS, preferred_element_type=F32) * jnp.exp(d_mat - m_t)
        a_in = jnp.exp(m_in - m_t)
        numden = (jnp.dot(s.astype(BF16), v_ext, preferred_element_type=F32)
                  + a_in * jnp.dot(qb, c_old.astype(BF16), preferred_element_type=F32))
        den = numden[:, DV_M:DV_M + 1]
        o_ref[0, :, h * DV_M:(h + 1) * DV_M] = numden[:, :DV_M] / jnp.maximum(jnp.abs(den), jnp.exp(-m_t))


def mlstm_scan(p, g_dir, gt_dir, cos_t, sin_t, dims):
    ncx = dims.ctx // CHUNK
    nlx = dims.seq // CHUNK
    x_blocks = dims.m_x // CHUNK

    def row_block(b, d, t):
        in_ctx = t < ncx
        cc = jnp.where(d == 0, t, ncx - 1 - t)
        cx = jnp.where(d == 0, t - ncx, nlx - 1 - (t - ncx))
        return jnp.where(in_ctx, x_blocks + b * ncx + cc, b * nlx + cx)

    def rope_block(b, d, t):
        in_ctx = t < ncx
        cx = jnp.where(d == 0, t - ncx, nlx - 1 - (t - ncx))
        return jnp.where(in_ctx, nlx, cx)

    return pl.pallas_call(
        _mlstm_kernel,
        out_shape=jax.ShapeDtypeStruct((2, dims.m_all, H_M * DV_M), F32),
        grid=(dims.batch, 2, ncx + nlx),
        in_specs=[
            pl.BlockSpec((CHUNK, H_M * DK_M), lambda b, d, t: (row_block(b, d, t), OFF_MQ // 512)),
            pl.BlockSpec((CHUNK, H_M * DK_M), lambda b, d, t: (row_block(b, d, t), OFF_MK // 512)),
            pl.BlockSpec((CHUNK, H_M * DV_M), lambda b, d, t: (row_block(b, d, t), OFF_MV // 1024)),
            pl.BlockSpec((1, CHUNK, LANES), lambda b, d, t: (d, row_block(b, d, t), 0)),
            pl.BlockSpec((1, 8, CHUNK), lambda b, d, t: (d, 0, row_block(b, d, t))),
            pl.BlockSpec((CHUNK, DK_M), lambda b, d, t: (rope_block(b, d, t), 0)),
            pl.BlockSpec((CHUNK, DK_M), lambda b, d, t: (rope_block(b, d, t), 0)),
        ],
        out_specs=pl.BlockSpec((1, CHUNK, H_M * DV_M), lambda b, d, t: (d, row_block(b, d, t), 0)),
        scratch_shapes=[pltpu.VMEM((H_M, DK_M, DV_EXT), F32), pltpu.VMEM((H_M, 8, LANES), F32)],
        compiler_params=_cparams(("arbitrary", "arbitrary", "arbitrary")), name="mlstm_scan",
    )(p, p, p, g_dir, gt_dir, cos_t, sin_t)


def _mlstm_out_kernel(h_ref, o_ref, g_ref, out_ref):
    for h in range(H_M):
        sl = slice(h * DV_M, (h + 1) * DV_M)
        y = _rms(h_ref[0, :, sl] + h_ref[1, :, sl], g_ref[:, sl])
        out_ref[:, sl] = (y * jax.nn.sigmoid(o_ref[:, sl])).astype(out_ref.dtype)


def mlstm_out(hdir, p, g, dims):
    m = dims.m_all
    tm = 256
    n = H_M * DV_M
    return pl.pallas_call(
        _mlstm_out_kernel,
        out_shape=jax.ShapeDtypeStruct((m, n), BF16),
        grid=(m // tm,),
        in_specs=[pl.BlockSpec((2, tm, n), lambda i: (0, i, 0)),
                  pl.BlockSpec((tm, n), lambda i: (i, OFF_MO // n)),
                  pl.BlockSpec((1, n), lambda i: (0, 0))],
        out_specs=pl.BlockSpec((tm, n), lambda i: (i, 0)),
        compiler_params=_cparams(("arbitrary",)), name="mlstm_out",
    )(hdir, p, g.reshape(1, n))


def rope_tables(dims):
    t = jnp.arange(dims.seq)
    row = (t // GRID_W).astype(F32)
    col = (t % GRID_W).astype(F32)
    quarter = DK_M // 4
    inv = ROPE_BASE ** (-jnp.arange(quarter, dtype=F32) / quarter)
    ang = jnp.concatenate([row[:, None] * inv, col[:, None] * inv], axis=-1)
    cos = jnp.repeat(jnp.cos(ang), 2, axis=-1)
    sin = jnp.repeat(jnp.sin(ang), 2, axis=-1) * jnp.tile(jnp.array([-1.0, 1.0], F32), DK_M // 2)
    cos = jnp.concatenate([cos, jnp.ones((CHUNK, DK_M), F32)], axis=0)
    sin = jnp.concatenate([sin, jnp.zeros((CHUNK, DK_M), F32)], axis=0)
    return cos, sin


def _gather_kernel(src_ref, h_hbm, o_ref, buf_ref, sem, *, tm):
    i = pl.program_id(0)

    def start(r, carry):
        pltpu.make_async_copy(h_hbm.at[pl.ds(src_ref[i * tm + r], 1)], buf_ref.at[pl.ds(r, 1)], sem).start()
        return carry

    lax.fori_loop(0, tm, start, 0)

    def wait(r, carry):
        pltpu.make_async_copy(h_hbm.at[pl.ds(0, 1)], buf_ref.at[pl.ds(r, 1)], sem).wait()
        return carry

    lax.fori_loop(0, tm, wait, 0)
    o_ref[...] = buf_ref[...].astype(o_ref.dtype)


def gather_rows(h, src, *, tm):
    m, d = h.shape
    r_total = src.shape[0]
    return pl.pallas_call(
        functools.partial(_gather_kernel, tm=tm),
        out_shape=jax.ShapeDtypeStruct((r_total, d), BF16),
        grid_spec=pltpu.PrefetchScalarGridSpec(
            num_scalar_prefetch=1,
            grid=(r_total // tm,),
            in_specs=[pl.BlockSpec(memory_space=pl.ANY)],
            out_specs=pl.BlockSpec((tm, d), lambda i, s: (i, 0)),
            scratch_shapes=[pltpu.VMEM((tm, d), F32), pltpu.SemaphoreType.DMA],
        ),
        compiler_params=_cparams(("arbitrary",)), name="moe_gather",
    )(src, h)


def _combine_kernel(p1_ref, p2_ref, y_hbm, x_ref, gate_ref, o_ref, y1_ref, y2_ref, sem, *, tm, seq, n_groups):
    i = pl.program_id(0)

    def start(r, carry):
        t = i * tm + r
        pltpu.make_async_copy(y_hbm.at[pl.ds(p1_ref[t], 1)], y1_ref.at[pl.ds(r, 1)], sem.at[0]).start()
        pltpu.make_async_copy(y_hbm.at[pl.ds(p2_ref[t], 1)], y2_ref.at[pl.ds(r, 1)], sem.at[1]).start()
        return carry

    lax.fori_loop(0, tm, start, 0)

    def wait(r, carry):
        pltpu.make_async_copy(y_hbm.at[pl.ds(0, 1)], y1_ref.at[pl.ds(r, 1)], sem.at[0]).wait()
        pltpu.make_async_copy(y_hbm.at[pl.ds(0, 1)], y2_ref.at[pl.ds(r, 1)], sem.at[1]).wait()
        return carry

    lax.fori_loop(0, tm, wait, 0)
    row0 = i * tm
    grp = jnp.int32(n_groups - 1)
    for b in reversed(range(n_groups - 1)):
        grp = jnp.where(row0 < (b + 1) * seq, b, grp)
    gate = gate_ref[pl.ds(grp, 1), :]
    o_ref[...] = x_ref[...] + gate * (y1_ref[...] + y2_ref[...])


def moe_combine(xs, y, pos1, pos2, mod, layer, gate_chunk, dims, *, m):
    d = dims.d_model
    tm = _pick_tile(math.gcd(dims.seq, dims.ctx), (256, 128))
    kern = functools.partial(_combine_kernel, tm=tm, seq=dims.seq, n_groups=dims.batch + 1)
    return pl.pallas_call(
        kern,
        out_shape=jax.ShapeDtypeStruct((m, d), F32),
        grid_spec=pltpu.PrefetchScalarGridSpec(
            num_scalar_prefetch=2,
            grid=(m // tm,),
            in_specs=[pl.BlockSpec(memory_space=pl.ANY),
                      pl.BlockSpec((tm, d), lambda i, a, b: (i, 0)),
                      pl.BlockSpec((8, d), lambda i, a, b: (MOD_ROWS // 8 * layer, gate_chunk))],
            out_specs=pl.BlockSpec((tm, d), lambda i, a, b: (i, 0)),
            scratch_shapes=[pltpu.VMEM((tm, d), F32), pltpu.VMEM((tm, d), F32),
                            pltpu.SemaphoreType.DMA((2,))],
        ),
        compiler_params=_cparams(("arbitrary",)), name="moe_combine",
    )(pos1, pos2, y, xs, mod)


def moe_dispatch_plan(route, m, tm):
    e = jnp.concatenate([route[0, :m], route[1, :m]]).astype(I32)
    w = jnp.concatenate([route[2, :m], route[3, :m]])
    tok = jnp.tile(jnp.arange(m, dtype=I32), 2)
    onehot = (e[:, None] == jnp.arange(N_EXPERTS, dtype=I32)[None, :]).astype(I32)
    rank = jnp.sum((jnp.cumsum(onehot, axis=0) - onehot) * onehot, axis=1)
    counts = jnp.sum(onehot, axis=0)
    padded = ((counts + tm - 1) // tm) * tm
    ends = jnp.cumsum(padded)
    offs = ends - padded
    pos = offs[e] + rank
    r_total = TOP_K * m + N_EXPERTS * tm
    src = jnp.zeros((r_total,), I32).at[pos].set(tok)
    scale = jnp.zeros((r_total,), F32).at[pos].set(w)
    tile_start = jnp.arange(r_total // tm, dtype=I32) * tm
    last_e = jnp.max(jnp.where(counts > 0, jnp.arange(N_EXPERTS, dtype=I32), 0))
    tile_e = jnp.minimum(jnp.sum((tile_start[:, None] >= ends[None, :]).astype(I32), axis=1), last_e)
    n_valid = (ends[-1] // tm).astype(I32).reshape(1)
    return src, scale, pos[:m], pos[m:], tile_e, n_valid


def _dense_tile(m, candidates):
    return _pick_tile(m, candidates)


def kernel(x, c, ctx, c_ctx, ada_w, ada_b, norm1_g, norm2_g, w_in, gate_b, na_q_g, na_k_g, na_rpb, m_norm_g, w_out, ffn_w_gate, ffn_w_up, ffn_w_down, moe_router, moe_w_gate, moe_w_up, moe_w_down):
    batch, seq, d = x.shape
    dims = Dims(batch=batch, seq=seq, ctx=ctx.shape[1], d_model=d, d_ff=ffn_w_gate.shape[2], depth=ada_w.shape[0])
    depth = dims.depth
    n_groups = batch + 1
    rows = seq // GRID_W
    assert rows >= 4 * NA_QROWS and rows % NA_QROWS == 0 and batch + 1 <= 8
    assert seq % 256 == 0 and dims.ctx % CHUNK == 0 and dims.m_x % dims.ctx == 0

    xs = jnp.concatenate([x.reshape(dims.m_x, d), ctx.reshape(batch * dims.ctx, d)], axis=0)

    cond = jnp.concatenate([jax.nn.silu(c), jax.nn.silu(c_ctx)[None], jnp.zeros((MOD_ROWS - n_groups, d), F32)], axis=0)
    tn_ada = _pick_tile(6 * d, (1024, 512, 256))
    tn_d = min(512, d)
    mod = gmm(cond.astype(BF16), (ada_w,), jnp.arange(depth, dtype=I32), jnp.full((1,), depth, I32),
              tm=MOD_ROWS, tn=tn_ada, n_out=6 * d, out_dtype=F32, epilogue="bias",
              extras=(ada_b.reshape(depth, 1, 6 * d),),
              extra_specs=(((1, 1, tn_ada), lambda j, i, g, *s: (g[i], 0, j)),),
              a_fixed=True, name="ada_mod")

    cos_t, sin_t = rope_tables(dims)
    na_bias = na_bias_table(na_rpb, rows)
    w_gate_cols = jnp.pad(w_in[:, :, N_MAIN:], ((0, 0), (0, 0), (0, LANES - 4 * H_M)))
    gate_b_pad = jnp.pad(gate_b, ((0, 0), (0, LANES - 4 * H_M)))

    def dense_groups(m, tm, g):
        nt = m // tm
        return jnp.full((nt,), g, I32), jnp.full((1,), nt, I32)

    for layer in range(depth):
        last = layer == depth - 1
        m_out = dims.m_x if last else dims.m_all

        hbf = norm_mod(xs, norm1_g[layer], mod, layer, 0, 1, dims, m=dims.m_all)
        tm_a = _dense_tile(dims.m_all, (2176, 1088, 640, 512, 256))
        grp, nv = dense_groups(dims.m_all, tm_a, layer)
        p = gmm(hbf, (w_in,), grp, nv, tm=tm_a, tn=512, n_out=N_MAIN, out_dtype=F32, rows_outer=True,
                name="w_in_proj")
        gates = gmm(hbf, (w_gate_cols,), grp, nv, tm=tm_a, tn=LANES, n_out=LANES, out_dtype=F32,
                    name="gate_proj")
        gates = gates + gate_b_pad[layer][None, :]
        g_dir = jnp.stack([gates, jnp.roll(gates, -2 * H_M, axis=1)])
        gt_dir = jnp.stack([gates[:, :2 * H_M].T, gates[:, 2 * H_M:4 * H_M].T])

        qk, vbf = qkv_prep(p, na_q_g[layer], na_k_g[layer], dims)
        na_x = na_attention(qk, vbf, na_bias, layer, dims)
        hdir = mlstm_scan(p, g_dir, gt_dir, cos_t, sin_t, dims)
        m_all_out = mlstm_out(hdir, p, m_norm_g[layer], dims)
        if last:
            mix = jnp.concatenate([na_x, m_all_out[:dims.m_x]], axis=1)
        else:
            na_c = ctx_attention(qk, vbf, dims)
            mix = jnp.concatenate([jnp.concatenate([na_x, na_c], axis=0), m_all_out], axis=1)

        tm_o = _dense_tile(m_out, (1088, 1024, 640, 512, 256))
        grp, nv = dense_groups(m_out, tm_o, layer)
        res_specs = lambda tm, tn, chunk: (
            ((tm, tn), lambda j, i, *s: (i, j)),
            ((8, tn), lambda j, i, *s: (MOD_ROWS // 8 * layer, chunk * (d // tn) + j)))
        tn_o = min(1024, d)
        xs = gmm(mix, (w_out,), grp, nv, tm=tm_o, tn=tn_o, n_out=d, out_dtype=F32, epilogue="residual",
                 extras=(xs, mod), extra_specs=res_specs(tm_o, tn_o, 2), seq=seq, n_groups=n_groups,
                 name="w_out_proj")

        idx = layer // 2
        if layer % 2 == 0:
            h2 = norm_mod(xs, norm2_g[layer], mod, layer, 3, 4, dims, m=m_out)
            tm_u = _dense_tile(m_out, (1088, 1024, 640, 512, 256))
            grp, nv = dense_groups(m_out, tm_u, idx)
            act = gmm_swiglu(h2, ffn_w_gate, ffn_w_up, grp, nv, tm=tm_u, tn=512, name="ffn_up")
            tm_d = _dense_tile(m_out, (544, 512, 320, 256))
            grp, nv = dense_groups(m_out, tm_d, idx)
            xs = gmm(act, (ffn_w_down,), grp, nv, tm=tm_d, tn=tn_d, n_out=d, out_dtype=F32, epilogue="residual",
                     extras=(xs, mod), extra_specs=res_specs(tm_d, tn_d, 5), seq=seq, n_groups=n_groups,
                     name="ffn_down")
        else:
            h2, route = norm_mod(xs, norm2_g[layer], mod, layer, 3, 4, dims, m=m_out,
                                 router_t=moe_router[idx].T)
            tm_e = 512 if m_out >= 4096 else 128
            src, scale, pos1, pos2, tile_e, n_valid = moe_dispatch_plan(route, m_out, tm_e)
            hs = gather_rows(h2, src, tm=tm_e)
            grp = tile_e + idx * N_EXPERTS
            nw = moe_w_gate.shape[0] * N_EXPERTS
            act = gmm_swiglu(hs, moe_w_gate.reshape(nw, d, dims.d_ff), moe_w_up.reshape(nw, d, dims.d_ff),
                             grp, n_valid, tm=tm_e, tn=512, name="moe_up")
            scale_b = jnp.broadcast_to(scale[:, None], (scale.shape[0], LANES))
            y = gmm(act, (moe_w_down.reshape(nw, dims.d_ff, d),), grp, n_valid, tm=tm_e, tn=tn_d, n_out=d,
                    out_dtype=F32, epilogue="rowscale", extras=(scale_b,),
                    extra_specs=(((tm_e, LANES), lambda j, i, *s: (i, 0)),), name="moe_down")
            xs = moe_combine(xs, y, pos1, pos2, mod, layer, 5, dims, m=m_out)

    return xs[:dims.m_x].reshape(batch, seq, d)
```

```python
import functools
import math
from typing import NamedTuple

import jax
import jax.numpy as jnp
import numpy as np
from jax import lax
from jax.experimental import pallas as pl
from jax.experimental.pallas import tpu as pltpu

F32 = jnp.float32
BF16 = jnp.bfloat16
I32 = jnp.int32

GRID_W = 64
H_NA = 8
DH_NA = 128
WIN_R = 8
WIN_C = 16
H_M = 4
DK_M = 128
DV_M = 256
CHUNK = 128
ROPE_BASE = 10000.0
N_EXPERTS = 8
TOP_K = 2
EPS = 1e-6
D_MIX = 2048
NEG = -1e30

OFF_MQ = 3 * H_NA * DH_NA
N_MAIN = 6144
PM_Q, PM_K, PM_V, PM_O = 0, H_M * DK_M, 2 * H_M * DK_M, 2 * H_M * DK_M + H_M * DV_M
DV_EXT = DV_M + 128

LANES = 128
VMEM_LIMIT = 56 * 1024 * 1024
MOD_ROWS = 16


class Dims(NamedTuple):
    batch: int
    seq: int
    ctx: int
    d_model: int
    d_ff: int
    depth: int

    @property
    def m_x(self):
        return self.batch * self.seq

    @property
    def m_all(self):
        return self.batch * (self.seq + self.ctx)


def _cparams(sem):
    return pltpu.CompilerParams(dimension_semantics=sem, vmem_limit_bytes=VMEM_LIMIT)


def _group_select(row, vals, seq):
    out = vals[-1]
    for b in reversed(range(len(vals) - 1)):
        out = jnp.where(row < (b + 1) * seq, vals[b], out)
    return out


def _pick_tile(m, candidates):
    for t in candidates:
        if m % t == 0:
            return t
    raise ValueError(f"no row tile for {m}")


N_EXTRA = {"residual": 2, "rowscale": 1, "bias": 1, "swiglu": 0, None: 0}


def _gmm_kernel(grp_ref, nv_ref, rid_ref, rgrp_ref, nruns_ref, *refs,
                n_a, n_w, epilogue, tm, tn, nj, seq, n_groups, rows_outer):
    a_refs, refs = refs[:n_a], refs[n_a:]
    w_refs, refs = refs[:n_w], refs[n_w:]
    x_refs, refs = refs[:N_EXTRA[epilogue]], refs[N_EXTRA[epilogue]:]
    o_ref, refs = refs[0], refs[1:]
    wbf_refs, refs = refs[:n_w], refs[n_w:]
    if rows_outer:
        i = pl.program_id(0)
        for w_ref, wbf_ref in zip(w_refs, wbf_refs):
            wbf_ref[...] = w_ref[0].astype(BF16)
    else:
        wst_refs, sem = refs[:n_w], refs[n_w]
        j = pl.program_id(0)
        i = pl.program_id(1)
        rid = rid_ref[i]
        nruns = nruns_ref[0]

        def tile_copy(widx, g, jj, slot):
            src = w_refs[widx].at[g, :, pl.ds(pl.multiple_of(jj * tn, tn), tn)]
            return pltpu.make_async_copy(src, wst_refs[widx].at[slot], sem.at[widx, slot])

        @pl.when((i == 0) | (rid != rid_ref[jnp.maximum(i - 1, 0)]))
        def _():
            q = j * nruns + rid
            slot = q % 2

            @pl.when(q == 0)
            def _():
                for widx in range(n_w):
                    tile_copy(widx, rgrp_ref[0], 0, 0).start()

            for widx in range(n_w):
                tile_copy(widx, 0, 0, slot).wait()
            wrap = rid + 1 == nruns
            next_run = jnp.where(wrap, 0, rid + 1)
            next_j = j + wrap.astype(I32)

            @pl.when(next_j < nj)
            def _():
                for widx in range(n_w):
                    tile_copy(widx, rgrp_ref[next_run], next_j, 1 - slot).start()

            for widx in range(n_w):
                wbf_refs[widx][...] = wst_refs[widx][slot].astype(BF16)

    @pl.when(i < nv_ref[0])
    def _():
        def matmul(wbf_ref):
            out, k0 = None, 0
            for a_ref in a_refs:
                kp = a_ref.shape[1]
                part = jnp.dot(a_ref[...], wbf_ref[k0:k0 + kp, :], preferred_element_type=F32)
                out = part if out is None else out + part
                k0 += kp
            return out

        acc = matmul(wbf_refs[0])
        if epilogue == "residual":
            res_ref, gate_ref = x_refs
            row = i * tm + lax.broadcasted_iota(I32, (tm, 1), 0)
            gate = _group_select(row, [gate_ref[k:k + 1, :] for k in range(n_groups)], seq)
            acc = res_ref[...] + gate * acc
        elif epilogue == "rowscale":
            acc = acc * x_refs[0][:, 0:1]
        elif epilogue == "bias":
            acc = acc + x_refs[0][0]
        elif epilogue == "swiglu":
            acc = acc * jax.nn.sigmoid(acc) * matmul(wbf_refs[1])
        o_ref[...] = acc.astype(o_ref.dtype)

    @pl.when(i >= nv_ref[0])
    def _():
        o_ref[...] = jnp.zeros(o_ref.shape, o_ref.dtype)


def _runs(grp):
    nt = grp.shape[0]
    first = jnp.concatenate([jnp.ones((1,), I32), (grp[1:] != grp[:-1]).astype(I32)])
    run_id = jnp.cumsum(first) - 1
    sel = (run_id[:, None] == jnp.arange(nt, dtype=I32)[None, :]).astype(I32) * first[:, None]
    run_grp = jnp.sum(sel * grp[:, None], axis=0)
    return run_id.astype(I32), run_grp.astype(I32), (run_id[-1:] + 1).astype(I32)


def gmm(a_parts, ws, grp, nvalid, *, tm, tn, n_out, out_dtype, epilogue=None, extras=(), extra_specs=(),
        a_fixed=False, seq=0, n_groups=0, rows_outer=False, name="gmm"):
    k = sum(a.shape[1] for a in a_parts)
    nt = grp.shape[0]
    n_w = len(ws)
    nj = n_out // tn

    def spec(shape, fn):
        return pl.BlockSpec(shape, (lambda i, j, *s: fn(j, i, *s)) if rows_outer else fn)

    a_map = (lambda j, i, *s: (0, 0)) if a_fixed else (lambda j, i, *s: (i, 0))
    if rows_outer:
        w_specs = [spec((1, k, tn), lambda j, i, g, *s: (g[i], 0, j))] * n_w
        stream_scratch = []
    else:
        w_specs = [pl.BlockSpec(memory_space=pl.ANY)] * n_w
        stream_scratch = [pltpu.VMEM((2, k, tn), F32)] * n_w + [pltpu.SemaphoreType.DMA((n_w, 2))]
    in_specs = ([spec((tm, a.shape[1]), a_map) for a in a_parts] + w_specs
                + [spec(shape, fn) for shape, fn in extra_specs])
    kern = functools.partial(_gmm_kernel, n_a=len(a_parts), n_w=n_w, epilogue=epilogue, tm=tm, tn=tn, nj=nj,
                             seq=seq, n_groups=n_groups, rows_outer=rows_outer)
    run_id, run_grp, nruns = _runs(grp)
    return pl.pallas_call(
        kern,
        out_shape=jax.ShapeDtypeStruct((nt * tm, n_out), out_dtype),
        grid_spec=pltpu.PrefetchScalarGridSpec(
            num_scalar_prefetch=5,
            grid=(nt, nj) if rows_outer else (nj, nt),
            in_specs=in_specs,
            out_specs=spec((tm, tn), lambda j, i, *s: (i, j)),
            scratch_shapes=[pltpu.VMEM((k, tn), BF16)] * n_w + stream_scratch,
        ),
        compiler_params=_cparams(("arbitrary", "arbitrary")),
        name=name,
    )(grp, nvalid, run_id, run_grp, nruns, *a_parts, *ws, *extras)


def gmm_swiglu(a, wg, wu, grp, nvalid, *, tm, tn, name="gmm_swiglu"):
    return gmm((a,), (wg, wu), grp, nvalid, tm=tm, tn=tn, n_out=wg.shape[2], out_dtype=BF16,
               epilogue="swiglu", name=name)


def _rms(x, g):
    return x * lax.rsqrt(jnp.mean(x * x, axis=-1, keepdims=True) + EPS) * g


def _split_bf16(x):
    hi = x.astype(BF16)
    lo = (x - hi.astype(F32)).astype(BF16)
    return hi, lo


def _norm_mod_kernel(x_ref, g_ref, shift_ref, scale_ref, *refs, tm, seq, n_groups, with_router):
    if with_router:
        r_ref, o_ref, route_ref = refs
    else:
        (o_ref,) = refs
    i = pl.program_id(0)
    row0 = i * tm
    grp = jnp.int32(n_groups - 1)
    for b in reversed(range(n_groups - 1)):
        grp = jnp.where(row0 < (b + 1) * seq, b, grp)
    y = _rms(x_ref[...], g_ref[...])
    h = y * (1.0 + scale_ref[pl.ds(grp, 1), :]) + shift_ref[pl.ds(grp, 1), :]
    o_ref[...] = h.astype(o_ref.dtype)
    if with_router:
        nt_dims = (((1,), (1,)), ((), ()))
        h_hi, h_lo = _split_bf16(h)
        r_hi, r_lo = _split_bf16(r_ref[...])
        logits = (lax.dot_general(r_hi, h_hi, nt_dims, preferred_element_type=F32)
                  + lax.dot_general(r_hi, h_lo, nt_dims, preferred_element_type=F32)
                  + lax.dot_general(r_lo, h_hi, nt_dims, preferred_element_type=F32))
        ids = lax.broadcasted_iota(I32, logits.shape, 0)
        m1 = jnp.max(logits, axis=0, keepdims=True)
        i1 = jnp.min(jnp.where(logits == m1, ids, N_EXPERTS), axis=0, keepdims=True)
        rest = jnp.where(ids == i1, -jnp.inf, logits)
        m2 = jnp.max(rest, axis=0, keepdims=True)
        i2 = jnp.min(jnp.where(rest == m2, ids, N_EXPERTS), axis=0, keepdims=True)
        e = jnp.exp(m2 - m1)
        w1 = 1.0 / (1.0 + e)
        w2 = e / (1.0 + e)
        out_row = lax.broadcasted_iota(I32, (8, tm), 0)
        route_ref[...] = jnp.where(out_row == 0, i1.astype(F32), jnp.where(
            out_row == 1, i2.astype(F32), jnp.where(out_row == 2, w1, jnp.where(out_row == 3, w2, 0.0))))


def norm_mod(xs, g, mod, layer, shift_chunk, scale_chunk, dims, *, m, router_t=None):
    d = dims.d_model
    tm = _pick_tile(math.gcd(dims.seq, dims.ctx), (256, 128))
    with_router = router_t is not None
    kern = functools.partial(_norm_mod_kernel, tm=tm, seq=dims.seq, n_groups=dims.batch + 1,
                             with_router=with_router)
    in_specs = [
        pl.BlockSpec((tm, d), lambda i: (i, 0)),
        pl.BlockSpec((1, d), lambda i: (0, 0)),
        pl.BlockSpec((8, d), lambda i: (MOD_ROWS // 8 * layer, shift_chunk)),
        pl.BlockSpec((8, d), lambda i: (MOD_ROWS // 8 * layer, scale_chunk)),
    ]
    args = [xs, g.reshape(1, d), mod, mod]
    out_shape = [jax.ShapeDtypeStruct((m, d), F32 if with_router else BF16)]
    out_specs = [pl.BlockSpec((tm, d), lambda i: (i, 0))]
    if with_router:
        in_specs.append(pl.BlockSpec((N_EXPERTS, d), lambda i: (0, 0)))
        args.append(router_t)
        out_shape.append(jax.ShapeDtypeStruct((8, m), F32))
        out_specs.append(pl.BlockSpec((8, tm), lambda i: (0, i)))
    res = pl.pallas_call(
        kern, out_shape=out_shape, grid=(m // tm,), in_specs=in_specs, out_specs=out_specs,
        compiler_params=_cparams(("arbitrary",)), name="norm_mod_router" if with_router else "norm_mod",
    )(*args)
    return res if with_router else res[0]


IN_TN = 512
J_QK = 2 * H_NA * DH_NA // IN_TN
J_V = H_NA * DH_NA // IN_TN
N_MLSTM = N_MAIN - OFF_MQ


def _in_proj_kernel(a_ref, w_ref, qg_ref, kg_ref, qk_ref, v_ref, pm_ref, wbf_ref):
    j = pl.program_id(1)
    wbf_ref[...] = w_ref[0].astype(BF16)
    acc = jnp.dot(a_ref[...], wbf_ref[...], preferred_element_type=F32)

    @pl.when(j < J_QK)
    def _():
        g = jnp.where(j < J_QK // 2, qg_ref[...] * (DH_NA ** -0.5), kg_ref[...])
        for h in range(IN_TN // DH_NA):
            sl = slice(h * DH_NA, (h + 1) * DH_NA)
            qk_ref[:, sl] = _rms(acc[:, sl], g).astype(BF16)

    @pl.when((j >= J_QK) & (j < J_QK + J_V))
    def _():
        v_ref[...] = acc.astype(BF16)

    @pl.when(j >= J_QK + J_V)
    def _():
        pm_ref[...] = acc


def in_proj(hbf, w_in, layer, qg, kg, dims):
    m, d = hbf.shape
    tm = _pick_tile(m, (2176, 1088, 640, 512, 256))
    nj = N_MAIN // IN_TN
    return pl.pallas_call(
        _in_proj_kernel,
        out_shape=[jax.ShapeDtypeStruct((m, 2 * H_NA * DH_NA), BF16),
                   jax.ShapeDtypeStruct((m, H_NA * DH_NA), BF16),
                   jax.ShapeDtypeStruct((m, N_MLSTM), F32)],
        grid=(m // tm, nj),
        in_specs=[pl.BlockSpec((tm, d), lambda i, j: (i, 0)),
                  pl.BlockSpec((1, d, IN_TN), lambda i, j: (layer, 0, j)),
                  pl.BlockSpec((1, DH_NA), lambda i, j: (0, 0)),
                  pl.BlockSpec((1, DH_NA), lambda i, j: (0, 0))],
        out_specs=[pl.BlockSpec((tm, IN_TN), lambda i, j: (i, jnp.minimum(j, J_QK - 1))),
                   pl.BlockSpec((tm, IN_TN), lambda i, j: (i, jnp.clip(j - J_QK, 0, J_V - 1))),
                   pl.BlockSpec((tm, IN_TN), lambda i, j: (i, jnp.maximum(j - (J_QK + J_V), 0)))],
        scratch_shapes=[pltpu.VMEM((d, IN_TN), BF16)],
        compiler_params=_cparams(("arbitrary", "arbitrary")), name="in_proj",
    )(hbf, w_in, qg.reshape(1, DH_NA), kg.reshape(1, DH_NA))


NT_DIMS = (((1,), (1,)), ((), ()))
TN_DIMS = (((0,), (0,)), ((), ()))
NA_QROWS = 4
NA_KROWS = NA_QROWS + WIN_R
N_BIAS_PATTERNS = 3


def _na_block_start(qb, rows):
    return jnp.clip(qb * NA_QROWS - WIN_R // 2, 0, rows - NA_KROWS)


def _na_kernel(q_ref, k_ref, v_ref, kc_ref, vc_ref, bias_ref, o_ref, *, rows):
    qb = pl.program_id(2)
    nblk = rows // NA_QROWS
    pid = jnp.where(qb == 0, 0, jnp.where(qb == nblk - 1, 2, 1))
    start = pl.multiple_of(_na_block_start(qb, rows) * GRID_W, GRID_W)
    q = q_ref[...]
    kb = k_ref[pl.ds(start, NA_KROWS * GRID_W), :]
    vb = v_ref[pl.ds(start, NA_KROWS * GRID_W), :]
    s = lax.dot_general(q, kb, NT_DIMS, preferred_element_type=F32) + bias_ref[0, pid]
    sc = lax.dot_general(q, kc_ref[...], NT_DIMS, preferred_element_type=F32)
    m = jnp.maximum(jnp.max(s, axis=-1, keepdims=True), jnp.max(sc, axis=-1, keepdims=True))
    pb = jnp.exp(s - m)
    pc = jnp.exp(sc - m)
    den = jnp.sum(pb, axis=-1, keepdims=True) + jnp.sum(pc, axis=-1, keepdims=True)
    o = (jnp.dot(pb.astype(BF16), vb, preferred_element_type=F32)
         + jnp.dot(pc.astype(BF16), vc_ref[...], preferred_element_type=F32))
    o_ref[...] = (o / den).astype(o_ref.dtype)


def na_bias_table(rpb, rows):
    nblk = rows // NA_QROWS
    qb = np.array([0, 2, nblk - 1])
    ks = np.clip(qb * NA_QROWS - WIN_R // 2, 0, rows - NA_KROWS)
    r = qb[:, None] * NA_QROWS + np.arange(NA_QROWS)[None, :]
    rs = np.clip(r - WIN_R // 2, 0, rows - WIN_R)
    kr = ks[:, None] + np.arange(NA_KROWS)[None, :]
    in_band = (kr[:, None, :] >= rs[:, :, None]) & (kr[:, None, :] < rs[:, :, None] + WIN_R)
    dr = kr[:, None, :] - r[:, :, None] + (WIN_R - 1)
    c = np.arange(GRID_W)
    dc = np.clip(c[None, :] - c[:, None], -(WIN_C - 1), WIN_C - 1) + (WIN_C - 1)
    cs = np.clip(c - WIN_C // 2, 0, GRID_W - WIN_C)
    col_ok = (c[None, :] >= cs[:, None]) & (c[None, :] < cs[:, None] + WIN_C)
    sel_r = jnp.asarray((dr[..., None] == np.arange(2 * WIN_R - 1)) & in_band[..., None], F32)
    sel_c = jnp.asarray(dc[..., None] == np.arange(2 * WIN_C - 1), F32)
    hi = lax.Precision.HIGHEST
    t = jnp.einsum("lhab,pxja->lhpxjb", rpb.astype(F32), sel_r, precision=hi)
    bias = jnp.einsum("lhpxjb,qkb->lhpxqjk", t, sel_c, precision=hi)
    ok = in_band[:, :, None, :, None] & col_ok[None, None, :, None, :]
    bias = jnp.where(jnp.asarray(ok)[None, None], bias, NEG)
    return bias.reshape(rpb.shape[0], H_NA, N_BIAS_PATTERNS, NA_QROWS * GRID_W, NA_KROWS * GRID_W)


def na_attention(qk, vbf, bias, layer, dims):
    rows = dims.seq // GRID_W
    qblocks = rows // NA_QROWS
    tq = NA_QROWS * GRID_W
    ctx_blk0 = dims.m_x // dims.ctx
    kern = functools.partial(_na_kernel, rows=rows)
    return pl.pallas_call(
        kern,
        out_shape=jax.ShapeDtypeStruct((dims.m_x, H_NA * DH_NA), BF16),
        grid=(dims.batch, H_NA, qblocks),
        in_specs=[
            pl.BlockSpec((tq, DH_NA), lambda b, h, i: (b * qblocks + i, h)),
            pl.BlockSpec((dims.seq, DH_NA), lambda b, h, i: (b, H_NA + h)),
            pl.BlockSpec((dims.seq, DH_NA), lambda b, h, i: (b, h)),
            pl.BlockSpec((dims.ctx, DH_NA), lambda b, h, i: (ctx_blk0 + b, H_NA + h)),
            pl.BlockSpec((dims.ctx, DH_NA), lambda b, h, i: (ctx_blk0 + b, h)),
            pl.BlockSpec((None, 1, N_BIAS_PATTERNS, tq, NA_KROWS * GRID_W), lambda b, h, i: (layer, h, 0, 0, 0)),
        ],
        out_specs=pl.BlockSpec((tq, DH_NA), lambda b, h, i: (b * qblocks + i, h)),
        compiler_params=_cparams(("arbitrary", "arbitrary", "arbitrary")), name="na_attention",
    )(qk, qk, vbf, qk, vbf, bias)


def _ctx_attn_kernel(q_ref, k_ref, v_ref, o_ref):
    s = lax.dot_general(q_ref[...], k_ref[...], NT_DIMS, preferred_element_type=F32)
    m = jnp.max(s, axis=-1, keepdims=True)
    p = jnp.exp(s - m)
    den = jnp.sum(p, axis=-1, keepdims=True)
    o = jnp.dot(p.astype(BF16), v_ref[...], preferred_element_type=F32)
    o_ref[...] = (o / den).astype(o_ref.dtype)


def ctx_attention(qk, vbf, dims):
    ctx_blk0 = dims.m_x // dims.ctx
    return pl.pallas_call(
        _ctx_attn_kernel,
        out_shape=jax.ShapeDtypeStruct((dims.batch * dims.ctx, H_NA * DH_NA), BF16),
        grid=(dims.batch, H_NA),
        in_specs=[
            pl.BlockSpec((dims.ctx, DH_NA), lambda b, h: (ctx_blk0 + b, h)),
            pl.BlockSpec((dims.ctx, DH_NA), lambda b, h: (ctx_blk0 + b, H_NA + h)),
            pl.BlockSpec((dims.ctx, DH_NA), lambda b, h: (ctx_blk0 + b, h)),
        ],
        out_specs=pl.BlockSpec((dims.ctx, DH_NA), lambda b, h: (b, h)),
        compiler_params=_cparams(("arbitrary", "arbitrary")), name="ctx_attention",
    )(qk, qk, vbf)


def _log_sigmoid(x):
    return jnp.minimum(x, 0.0) - jnp.log(1.0 + jnp.exp(-jnp.abs(x)))


def _rope(t, cos, sin_signed, even):
    swapped = jnp.where(even, pltpu.roll(t, DK_M - 1, 1), pltpu.roll(t, 1, 1))
    return t * cos + swapped * sin_signed


def _mlstm_kernel(q_ref, k_ref, v_ref, g_ref, gt_ref, cos_ref, sin_ref, o_ref, c_ref, m_ref):
    direction = pl.program_id(1)
    t_step = pl.program_id(2)

    @pl.when(t_step == 0)
    def _():
        c_ref[...] = jnp.zeros(c_ref.shape, F32)
        m_ref[...] = jnp.zeros(m_ref.shape, F32)

    ti = lax.broadcasted_iota(I32, (CHUNK, CHUNK), 0)
    si = lax.broadcasted_iota(I32, (CHUNK, CHUNK), 1)
    mask = (si - ti) * (1 - 2 * direction) <= 0
    tri = mask.astype(BF16)
    g = g_ref[0]
    gt = gt_ref[0]
    lf_hi, lf_lo = _split_bf16(_log_sigmoid(g))
    cum_col = (jnp.dot(tri, lf_hi, preferred_element_type=F32)
               + jnp.dot(tri, lf_lo, preferred_element_type=F32))
    lft = _log_sigmoid(gt)
    lft_hi, lft_lo = _split_bf16(lft)
    cum_row = (lax.dot_general(lft_hi, tri, NT_DIMS, preferred_element_type=F32)
               + lax.dot_general(lft_lo, tri, NT_DIMS, preferred_element_type=F32))
    cos = cos_ref[...]
    sin = sin_ref[...]
    even = (lax.broadcasted_iota(I32, (CHUNK, DK_M), 1) % 2) == 0
    ones_col = (lax.broadcasted_iota(I32, (CHUNK, DV_EXT - DV_M), 1) == 0).astype(BF16)
    for h in range(H_M):
        q = _rope(q_ref[:, h * DK_M:(h + 1) * DK_M], cos, sin, even)
        k = _rope(k_ref[:, h * DK_M:(h + 1) * DK_M], cos, sin, even) * (DK_M ** -0.5)
        v_ext = jnp.concatenate([v_ref[:, h * DV_M:(h + 1) * DV_M].astype(BF16), ones_col], axis=1)
        qb = q.astype(BF16)
        b_col = cum_col[:, H_M + h:H_M + h + 1]
        i_col = g[:, h:h + 1]
        b_row = cum_row[H_M + h:H_M + h + 1, :]
        i_row = gt[h:h + 1, :]
        b_last = jnp.sum(lft[H_M + h:H_M + h + 1, :], axis=1, keepdims=True)
        m_old = m_ref[h][0:1, 0:1]
        c_old = c_ref[h]
        d_end_row = b_last - b_row + i_row
        m_new = jnp.maximum(b_last + m_old, jnp.max(d_end_row, axis=1, keepdims=True))
        w_end = jnp.exp(b_last - b_col + i_col - m_new)
        decay = jnp.exp(b_last + m_old - m_new)
        kw = (k * w_end).astype(BF16)
        c_ref[h] = decay * c_old + lax.dot_general(kw, v_ext, TN_DIMS, preferred_element_type=F32)
        m_ref[h] = jnp.broadcast_to(m_new, m_ref.shape[1:])
        d_mat = jnp.where(mask, b_col - b_row + i_row, NEG)
        m_in = b_col + m_old
        m_t = jnp.maximum(m_in, jnp.max(d_mat, axis=1, keepdims=True))
        s = lax.dot_general(qb, k.astype(BF16), NT_DIMS, preferred_element_type=F32) * jnp.exp(d_mat - m_t)
        a_in = jnp.exp(m_in - m_t)
        numden = (jnp.dot(s.astype(BF16), v_ext, preferred_element_type=F32)
                  + a_in * jnp.dot(qb, c_old.astype(BF16), preferred_element_type=F32))
        den = numden[:, DV_M:DV_M + 1]
        o_ref[0, :, h * DV_M:(h + 1) * DV_M] = numden[:, :DV_M] / jnp.maximum(jnp.abs(den), jnp.exp(-m_t))


def mlstm_scan(p, g_dir, gt_dir, cos_t, sin_t, dims):
    ncx = dims.ctx // CHUNK
    nlx = dims.seq // CHUNK
    x_blocks = dims.m_x // CHUNK

    def row_block(b, d, t):
        in_ctx = t < ncx
        cc = jnp.where(d == 0, t, ncx - 1 - t)
        cx = jnp.where(d == 0, t - ncx, nlx - 1 - (t - ncx))
        return jnp.where(in_ctx, x_blocks + b * ncx + cc, b * nlx + cx)

    def rope_block(b, d, t):
        in_ctx = t < ncx
        cx = jnp.where(d == 0, t - ncx, nlx - 1 - (t - ncx))
        return jnp.where(in_ctx, nlx, cx)

    return pl.pallas_call(
        _mlstm_kernel,
        out_shape=jax.ShapeDtypeStruct((2, dims.m_all, H_M * DV_M), F32),
        grid=(dims.batch, 2, ncx + nlx),
        in_specs=[
            pl.BlockSpec((CHUNK, H_M * DK_M), lambda b, d, t: (row_block(b, d, t), PM_Q // (H_M * DK_M))),
            pl.BlockSpec((CHUNK, H_M * DK_M), lambda b, d, t: (row_block(b, d, t), PM_K // (H_M * DK_M))),
            pl.BlockSpec((CHUNK, H_M * DV_M), lambda b, d, t: (row_block(b, d, t), PM_V // (H_M * DV_M))),
            pl.BlockSpec((1, CHUNK, LANES), lambda b, d, t: (d, row_block(b, d, t), 0)),
            pl.BlockSpec((1, 8, CHUNK), lambda b, d, t: (d, 0, row_block(b, d, t))),
            pl.BlockSpec((CHUNK, DK_M), lambda b, d, t: (rope_block(b, d, t), 0)),
            pl.BlockSpec((CHUNK, DK_M), lambda b, d, t: (rope_block(b, d, t), 0)),
        ],
        out_specs=pl.BlockSpec((1, CHUNK, H_M * DV_M), lambda b, d, t: (d, row_block(b, d, t), 0)),
        scratch_shapes=[pltpu.VMEM((H_M, DK_M, DV_EXT), F32), pltpu.VMEM((H_M, 8, LANES), F32)],
        compiler_params=_cparams(("arbitrary", "arbitrary", "arbitrary")), name="mlstm_scan",
    )(p, p, p, g_dir, gt_dir, cos_t, sin_t)


def _mlstm_out_kernel(h_ref, o_ref, g_ref, out_ref):
    for h in range(H_M):
        sl = slice(h * DV_M, (h + 1) * DV_M)
        y = _rms(h_ref[0, :, sl] + h_ref[1, :, sl], g_ref[:, sl])
        out_ref[:, sl] = (y * jax.nn.sigmoid(o_ref[:, sl])).astype(out_ref.dtype)


def mlstm_out(hdir, p, g, dims):
    m = dims.m_all
    tm = 256
    n = H_M * DV_M
    return pl.pallas_call(
        _mlstm_out_kernel,
        out_shape=jax.ShapeDtypeStruct((m, n), BF16),
        grid=(m // tm,),
        in_specs=[pl.BlockSpec((2, tm, n), lambda i: (0, i, 0)),
                  pl.BlockSpec((tm, n), lambda i: (i, PM_O // n)),
                  pl.BlockSpec((1, n), lambda i: (0, 0))],
        out_specs=pl.BlockSpec((tm, n), lambda i: (i, 0)),
        compiler_params=_cparams(("arbitrary",)), name="mlstm_out",
    )(hdir, p, g.reshape(1, n))


def rope_tables(dims):
    t = jnp.arange(dims.seq)
    row = (t // GRID_W).astype(F32)
    col = (t % GRID_W).astype(F32)
    quarter = DK_M // 4
    inv = ROPE_BASE ** (-jnp.arange(quarter, dtype=F32) / quarter)
    ang = jnp.concatenate([row[:, None] * inv, col[:, None] * inv], axis=-1)
    cos = jnp.repeat(jnp.cos(ang), 2, axis=-1)
    sin = jnp.repeat(jnp.sin(ang), 2, axis=-1) * jnp.tile(jnp.array([-1.0, 1.0], F32), DK_M // 2)
    cos = jnp.concatenate([cos, jnp.ones((CHUNK, DK_M), F32)], axis=0)
    sin = jnp.concatenate([sin, jnp.zeros((CHUNK, DK_M), F32)], axis=0)
    return cos, sin


def _gather_kernel(src_ref, nrows_ref, h_hbm, o_ref, buf_ref, sem, *, tm):
    i = pl.program_id(0)
    n = nrows_ref[i]

    @pl.when(n < tm)
    def _():
        buf_ref[...] = jnp.zeros(buf_ref.shape, buf_ref.dtype)

    def start(r, carry):
        pltpu.make_async_copy(h_hbm.at[pl.ds(src_ref[i * tm + r], 1)], buf_ref.at[pl.ds(r, 1)], sem).start()
        return carry

    lax.fori_loop(0, n, start, 0)

    def wait(r, carry):
        pltpu.make_async_copy(h_hbm.at[pl.ds(0, 1)], buf_ref.at[pl.ds(r, 1)], sem).wait()
        return carry

    lax.fori_loop(0, n, wait, 0)
    o_ref[...] = buf_ref[...].astype(o_ref.dtype)


def gather_rows(h, src, tile_rows, *, tm):
    m, d = h.shape
    r_total = src.shape[0]
    return pl.pallas_call(
        functools.partial(_gather_kernel, tm=tm),
        out_shape=jax.ShapeDtypeStruct((r_total, d), BF16),
        grid_spec=pltpu.PrefetchScalarGridSpec(
            num_scalar_prefetch=2,
            grid=(r_total // tm,),
            in_specs=[pl.BlockSpec(memory_space=pl.ANY)],
            out_specs=pl.BlockSpec((tm, d), lambda i, *s: (i, 0)),
            scratch_shapes=[pltpu.VMEM((tm, d), F32), pltpu.SemaphoreType.DMA],
        ),
        compiler_params=_cparams(("arbitrary",)), name="moe_gather",
    )(src, tile_rows, h)


def _combine_kernel(p1_ref, p2_ref, y_hbm, x_ref, gate_ref, o_ref, y1_ref, y2_ref, sem, *, tm, seq, n_groups):
    i = pl.program_id(0)

    def start(r, carry):
        t = i * tm + r
        pltpu.make_async_copy(y_hbm.at[pl.ds(p1_ref[t], 1)], y1_ref.at[pl.ds(r, 1)], sem.at[0]).start()
        pltpu.make_async_copy(y_hbm.at[pl.ds(p2_ref[t], 1)], y2_ref.at[pl.ds(r, 1)], sem.at[1]).start()
        return carry

    lax.fori_loop(0, tm, start, 0)

    def wait(r, carry):
        pltpu.make_async_copy(y_hbm.at[pl.ds(0, 1)], y1_ref.at[pl.ds(r, 1)], sem.at[0]).wait()
        pltpu.make_async_copy(y_hbm.at[pl.ds(0, 1)], y2_ref.at[pl.ds(r, 1)], sem.at[1]).wait()
        return carry

    lax.fori_loop(0, tm, wait, 0)
    row0 = i * tm
    grp = jnp.int32(n_groups - 1)
    for b in reversed(range(n_groups - 1)):
        grp = jnp.where(row0 < (b + 1) * seq, b, grp)
    gate = gate_ref[pl.ds(grp, 1), :]
    o_ref[...] = x_ref[...] + gate * (y1_ref[...] + y2_ref[...])


def moe_combine(xs, y, pos1, pos2, mod, layer, gate_chunk, dims, *, m):
    d = dims.d_model
    tm = _pick_tile(math.gcd(dims.seq, dims.ctx), (256, 128))
    kern = functools.partial(_combine_kernel, tm=tm, seq=dims.seq, n_groups=dims.batch + 1)
    return pl.pallas_call(
        kern,
        out_shape=jax.ShapeDtypeStruct((m, d), F32),
        grid_spec=pltpu.PrefetchScalarGridSpec(
            num_scalar_prefetch=2,
            grid=(m // tm,),
            in_specs=[pl.BlockSpec(memory_space=pl.ANY),
                      pl.BlockSpec((tm, d), lambda i, a, b: (i, 0)),
                      pl.BlockSpec((8, d), lambda i, a, b: (MOD_ROWS // 8 * layer, gate_chunk))],
            out_specs=pl.BlockSpec((tm, d), lambda i, a, b: (i, 0)),
            scratch_shapes=[pltpu.VMEM((tm, d), F32), pltpu.VMEM((tm, d), F32),
                            pltpu.SemaphoreType.DMA((2,))],
        ),
        compiler_params=_cparams(("arbitrary",)), name="moe_combine",
    )(pos1, pos2, y, xs, mod)


def moe_dispatch_plan(route, m, tm):
    e = jnp.concatenate([route[0, :m], route[1, :m]]).astype(I32)
    w = jnp.concatenate([route[2, :m], route[3, :m]])
    tok = jnp.tile(jnp.arange(m, dtype=I32), 2)
    onehot = (e[:, None] == jnp.arange(N_EXPERTS, dtype=I32)[None, :]).astype(I32)
    rank = jnp.sum((jnp.cumsum(onehot, axis=0) - onehot) * onehot, axis=1)
    counts = jnp.sum(onehot, axis=0)
    padded = ((counts + tm - 1) // tm) * tm
    ends = jnp.cumsum(padded)
    offs = ends - padded
    pos = offs[e] + rank
    r_total = TOP_K * m + N_EXPERTS * tm
    src = jnp.zeros((r_total,), I32).at[pos].set(tok)
    scale = jnp.zeros((r_total,), F32).at[pos].set(w)
    tile_start = jnp.arange(r_total // tm, dtype=I32) * tm
    last_e = jnp.max(jnp.where(counts > 0, jnp.arange(N_EXPERTS, dtype=I32), 0))
    tile_e = jnp.minimum(jnp.sum((tile_start[:, None] >= ends[None, :]).astype(I32), axis=1), last_e)
    n_valid = (ends[-1] // tm).astype(I32).reshape(1)
    tile_rows = jnp.clip((offs + counts)[tile_e] - tile_start, 0, tm).astype(I32)
    return src, scale, pos[:m], pos[m:], tile_e, n_valid, tile_rows


def _dense_tile(m, candidates):
    return _pick_tile(m, candidates)


def kernel(x, c, ctx, c_ctx, ada_w, ada_b, norm1_g, norm2_g, w_in, gate_b, na_q_g, na_k_g, na_rpb, m_norm_g, w_out, ffn_w_gate, ffn_w_up, ffn_w_down, moe_router, moe_w_gate, moe_w_up, moe_w_down):
    batch, seq, d = x.shape
    dims = Dims(batch=batch, seq=seq, ctx=ctx.shape[1], d_model=d, d_ff=ffn_w_gate.shape[2], depth=ada_w.shape[0])
    depth = dims.depth
    n_groups = batch + 1
    rows = seq // GRID_W
    assert rows >= 4 * NA_QROWS and rows % NA_QROWS == 0 and batch + 1 <= 8
    assert seq % 256 == 0 and dims.ctx % CHUNK == 0 and dims.m_x % dims.ctx == 0

    xs = jnp.concatenate([x.reshape(dims.m_x, d), ctx.reshape(batch * dims.ctx, d)], axis=0)

    cond = jnp.concatenate([jax.nn.silu(c), jax.nn.silu(c_ctx)[None], jnp.zeros((MOD_ROWS - n_groups, d), F32)], axis=0)
    tn_ada = _pick_tile(6 * d, (1024, 512, 256))
    tn_d = min(512, d)
    mod = gmm((cond.astype(BF16),), (ada_w,), jnp.arange(depth, dtype=I32), jnp.full((1,), depth, I32),
              tm=MOD_ROWS, tn=tn_ada, n_out=6 * d, out_dtype=F32, epilogue="bias",
              extras=(ada_b.reshape(depth, 1, 6 * d),),
              extra_specs=(((1, 1, tn_ada), lambda j, i, g, *s: (g[i], 0, j)),),
              a_fixed=True, name="ada_mod")

    cos_t, sin_t = rope_tables(dims)
    na_bias = na_bias_table(na_rpb, rows)
    w_gate_cols = jnp.pad(w_in[:, :, N_MAIN:], ((0, 0), (0, 0), (0, LANES - 4 * H_M)))
    gate_b_pad = jnp.pad(gate_b, ((0, 0), (0, LANES - 4 * H_M)))

    def dense_groups(m, tm, g):
        nt = m // tm
        return jnp.full((nt,), g, I32), jnp.full((1,), nt, I32)

    for layer in range(depth):
        last = layer == depth - 1
        m_out = dims.m_x if last else dims.m_all

        hbf = norm_mod(xs, norm1_g[layer], mod, layer, 0, 1, dims, m=dims.m_all)
        tm_a = _dense_tile(dims.m_all, (2176, 1088, 640, 512, 256))
        grp, nv = dense_groups(dims.m_all, tm_a, layer)
        qk, vbf, pm = in_proj(hbf, w_in, layer, na_q_g[layer], na_k_g[layer], dims)
        gates = gmm((hbf,), (w_gate_cols,), grp, nv, tm=tm_a, tn=LANES, n_out=LANES, out_dtype=F32,
                    name="gate_proj")
        gates = gates + gate_b_pad[layer][None, :]
        g_dir = jnp.stack([gates, jnp.roll(gates, -2 * H_M, axis=1)])
        gt_dir = jnp.stack([gates[:, :2 * H_M].T, gates[:, 2 * H_M:4 * H_M].T])

        na_out = na_attention(qk, vbf, na_bias, layer, dims)
        hdir = mlstm_scan(pm, g_dir, gt_dir, cos_t, sin_t, dims)
        m_mix = mlstm_out(hdir, pm, m_norm_g[layer], dims)
        if not last:
            na_out = jnp.concatenate([na_out, ctx_attention(qk, vbf, dims)], axis=0)

        tm_o = _dense_tile(m_out, (1088, 1024, 640, 512, 256))
        grp, nv = dense_groups(m_out, tm_o, layer)
        res_specs = lambda tm, tn, chunk: (
            ((tm, tn), lambda j, i, *s: (i, j)),
            ((8, tn), lambda j, i, *s: (MOD_ROWS // 8 * layer, chunk * (d // tn) + j)))
        tn_o = min(1024, d)
        xs = gmm((na_out, m_mix), (w_out,), grp, nv, tm=tm_o, tn=tn_o, n_out=d, out_dtype=F32, epilogue="residual",
                 extras=(xs, mod), extra_specs=res_specs(tm_o, tn_o, 2), seq=seq, n_groups=n_groups,
                 name="w_out_proj")

        idx = layer // 2
        if layer % 2 == 0:
            h2 = norm_mod(xs, norm2_g[layer], mod, layer, 3, 4, dims, m=m_out)
            tm_u = _dense_tile(m_out, (1088, 1024, 640, 512, 256))
            grp, nv = dense_groups(m_out, tm_u, idx)
            act = gmm_swiglu(h2, ffn_w_gate, ffn_w_up, grp, nv, tm=tm_u, tn=512, name="ffn_up")
            tm_d = _dense_tile(m_out, (544, 512, 320, 256))
            grp, nv = dense_groups(m_out, tm_d, idx)
            xs = gmm((act,), (ffn_w_down,), grp, nv, tm=tm_d, tn=tn_d, n_out=d, out_dtype=F32, epilogue="residual",
                     extras=(xs, mod), extra_specs=res_specs(tm_d, tn_d, 5), seq=seq, n_groups=n_groups,
                     name="ffn_down")
        else:
            h2, route = norm_mod(xs, norm2_g[layer], mod, layer, 3, 4, dims, m=m_out,
                                 router_t=moe_router[idx].T)
            tm_e = 512 if m_out >= 4096 else 128
            src, scale, pos1, pos2, tile_e, n_valid, tile_rows = moe_dispatch_plan(route, m_out, tm_e)
            hs = gather_rows(h2, src, tile_rows, tm=tm_e)
            grp = tile_e + idx * N_EXPERTS
            nw = moe_w_gate.shape[0] * N_EXPERTS
            act = gmm_swiglu(hs, moe_w_gate.reshape(nw, d, dims.d_ff), moe_w_up.reshape(nw, d, dims.d_ff),
                             grp, n_valid, tm=tm_e, tn=512, name="moe_up")
            scale_b = jnp.broadcast_to(scale[:, None], (scale.shape[0], LANES))
            y = gmm((act,), (moe_w_down.reshape(nw, dims.d_ff, d),), grp, n_valid, tm=tm_e, tn=tn_d, n_out=d,
                    out_dtype=F32, epilogue="rowscale", extras=(scale_b,),
                    extra_specs=(((tm_e, LANES), lambda j, i, *s: (i, 0)),), name="moe_down")
            xs = moe_combine(xs, y, pos1, pos2, mod, layer, 5, dims, m=m_out)

    return xs[:dims.m_x].reshape(batch, seq, d)
```

```python
import functools
import math
from typing import NamedTuple

import jax
import jax.numpy as jnp
import numpy as np
from jax import lax
from jax.experimental import pallas as pl
from jax.experimental.pallas import tpu as pltpu

F32 = jnp.float32
BF16 = jnp.bfloat16
I32 = jnp.int32

GRID_W = 64
H_NA = 8
DH_NA = 128
WIN_R = 8
WIN_C = 16
H_M = 4
DK_M = 128
DV_M = 256
CHUNK = 128
ROPE_BASE = 10000.0
N_EXPERTS = 8
TOP_K = 2
EPS = 1e-6
D_MIX = 2048
NEG = -1e30

OFF_MQ = 3 * H_NA * DH_NA
N_MAIN = 6144
PM_Q, PM_K, PM_V, PM_O = 0, H_M * DK_M, 2 * H_M * DK_M, 2 * H_M * DK_M + H_M * DV_M
DV_EXT = DV_M + 128

LANES = 128
VMEM_LIMIT = 56 * 1024 * 1024
MOD_ROWS = 16


class Dims(NamedTuple):
    batch: int
    seq: int
    ctx: int
    d_model: int
    d_ff: int
    depth: int

    @property
    def m_x(self):
        return self.batch * self.seq

    @property
    def m_all(self):
        return self.batch * (self.seq + self.ctx)


def _cparams(sem):
    return pltpu.CompilerParams(dimension_semantics=sem, vmem_limit_bytes=VMEM_LIMIT)


def _group_select(row, vals, seq):
    out = vals[-1]
    for b in reversed(range(len(vals) - 1)):
        out = jnp.where(row < (b + 1) * seq, vals[b], out)
    return out


def _pick_tile(m, candidates):
    for t in candidates:
        if m % t == 0:
            return t
    raise ValueError(f"no row tile for {m}")


N_EXTRA = {"residual": 2, "rowscale": 1, "bias": 1, "swiglu": 0, None: 0}


def _gmm_kernel(grp_ref, nv_ref, rid_ref, rgrp_ref, nruns_ref, *refs,
                n_a, n_w, epilogue, tm, tn, nj, seq, n_groups, rows_outer):
    a_refs, refs = refs[:n_a], refs[n_a:]
    w_refs, refs = refs[:n_w], refs[n_w:]
    x_refs, refs = refs[:N_EXTRA[epilogue]], refs[N_EXTRA[epilogue]:]
    o_ref, refs = refs[0], refs[1:]
    wbf_refs, refs = refs[:n_w], refs[n_w:]
    if rows_outer:
        i = pl.program_id(0)
        for w_ref, wbf_ref in zip(w_refs, wbf_refs):
            wbf_ref[...] = w_ref[0].astype(BF16)
    else:
        wst_refs, sem = refs[:n_w], refs[n_w]
        j = pl.program_id(0)
        i = pl.program_id(1)
        rid = rid_ref[i]
        nruns = nruns_ref[0]

        def tile_copy(widx, g, jj, slot):
            src = w_refs[widx].at[g, :, pl.ds(pl.multiple_of(jj * tn, tn), tn)]
            return pltpu.make_async_copy(src, wst_refs[widx].at[slot], sem.at[widx, slot])

        @pl.when((i == 0) | (rid != rid_ref[jnp.maximum(i - 1, 0)]))
        def _():
            q = j * nruns + rid
            slot = q % 2

            @pl.when(q == 0)
            def _():
                for widx in range(n_w):
                    tile_copy(widx, rgrp_ref[0], 0, 0).start()

            for widx in range(n_w):
                tile_copy(widx, 0, 0, slot).wait()
            wrap = rid + 1 == nruns
            next_run = jnp.where(wrap, 0, rid + 1)
            next_j = j + wrap.astype(I32)

            @pl.when(next_j < nj)
            def _():
                for widx in range(n_w):
                    tile_copy(widx, rgrp_ref[next_run], next_j, 1 - slot).start()

            for widx in range(n_w):
                wbf_refs[widx][...] = wst_refs[widx][slot].astype(BF16)

    @pl.when(i < nv_ref[0])
    def _():
        def matmul(wbf_ref):
            out, k0 = None, 0
            for a_ref in a_refs:
                kp = a_ref.shape[1]
                part = jnp.dot(a_ref[...], wbf_ref[k0:k0 + kp, :], preferred_element_type=F32)
                out = part if out is None else out + part
                k0 += kp
            return out

        acc = matmul(wbf_refs[0])
        if epilogue == "residual":
            res_ref, gate_ref = x_refs
            row = i * tm + lax.broadcasted_iota(I32, (tm, 1), 0)
            gate = _group_select(row, [gate_ref[k:k + 1, :] for k in range(n_groups)], seq)
            acc = res_ref[...] + gate * acc
        elif epilogue == "rowscale":
            acc = acc * x_refs[0][:, 0:1]
        elif epilogue == "bias":
            acc = acc + x_refs[0][0]
        elif epilogue == "swiglu":
            acc = acc * jax.nn.sigmoid(acc) * matmul(wbf_refs[1])
        o_ref[...] = acc.astype(o_ref.dtype)

    @pl.when(i >= nv_ref[0])
    def _():
        o_ref[...] = jnp.zeros(o_ref.shape, o_ref.dtype)


def _runs(grp):
    nt = grp.shape[0]
    first = jnp.concatenate([jnp.ones((1,), I32), (grp[1:] != grp[:-1]).astype(I32)])
    run_id = jnp.cumsum(first) - 1
    sel = (run_id[:, None] == jnp.arange(nt, dtype=I32)[None, :]).astype(I32) * first[:, None]
    run_grp = jnp.sum(sel * grp[:, None], axis=0)
    return run_id.astype(I32), run_grp.astype(I32), (run_id[-1:] + 1).astype(I32)


def gmm(a_parts, ws, grp, nvalid, *, tm, tn, n_out, out_dtype, epilogue=None, extras=(), extra_specs=(),
        a_fixed=False, seq=0, n_groups=0, rows_outer=False, name="gmm"):
    k = sum(a.shape[1] for a in a_parts)
    nt = grp.shape[0]
    n_w = len(ws)
    nj = n_out // tn

    def spec(shape, fn):
        return pl.BlockSpec(shape, (lambda i, j, *s: fn(j, i, *s)) if rows_outer else fn)

    a_map = (lambda j, i, *s: (0, 0)) if a_fixed else (lambda j, i, *s: (i, 0))
    if rows_outer:
        w_specs = [spec((1, k, tn), lambda j, i, g, *s: (g[i], 0, j))] * n_w
        stream_scratch = []
    else:
        w_specs = [pl.BlockSpec(memory_space=pl.ANY)] * n_w
        stream_scratch = [pltpu.VMEM((2, k, tn), F32)] * n_w + [pltpu.SemaphoreType.DMA((n_w, 2))]
    in_specs = ([spec((tm, a.shape[1]), a_map) for a in a_parts] + w_specs
                + [spec(shape, fn) for shape, fn in extra_specs])
    kern = functools.partial(_gmm_kernel, n_a=len(a_parts), n_w=n_w, epilogue=epilogue, tm=tm, tn=tn, nj=nj,
                             seq=seq, n_groups=n_groups, rows_outer=rows_outer)
    run_id, run_grp, nruns = _runs(grp)
    return pl.pallas_call(
        kern,
        out_shape=jax.ShapeDtypeStruct((nt * tm, n_out), out_dtype),
        grid_spec=pltpu.PrefetchScalarGridSpec(
            num_scalar_prefetch=5,
            grid=(nt, nj) if rows_outer else (nj, nt),
            in_specs=in_specs,
            out_specs=spec((tm, tn), lambda j, i, *s: (i, j)),
            scratch_shapes=[pltpu.VMEM((k, tn), BF16)] * n_w + stream_scratch,
        ),
        compiler_params=_cparams(("arbitrary", "arbitrary")),
        name=name,
    )(grp, nvalid, run_id, run_grp, nruns, *a_parts, *ws, *extras)


def gmm_swiglu(a, wg, wu, grp, nvalid, *, tm, tn, name="gmm_swiglu"):
    return gmm((a,), (wg, wu), grp, nvalid, tm=tm, tn=tn, n_out=wg.shape[2], out_dtype=BF16,
               epilogue="swiglu", name=name)


def _rms(x, g):
    return x * lax.rsqrt(jnp.mean(x * x, axis=-1, keepdims=True) + EPS) * g


def _split_bf16(x):
    hi = x.astype(BF16)
    lo = (x - hi.astype(F32)).astype(BF16)
    return hi, lo


def _norm_mod_kernel(x_ref, g_ref, shift_ref, scale_ref, *refs, tm, seq, n_groups, with_router):
    if with_router:
        r_ref, o_ref, route_ref = refs
    else:
        (o_ref,) = refs
    i = pl.program_id(0)
    row0 = i * tm
    grp = jnp.int32(n_groups - 1)
    for b in reversed(range(n_groups - 1)):
        grp = jnp.where(row0 < (b + 1) * seq, b, grp)
    y = _rms(x_ref[...], g_ref[...])
    h = y * (1.0 + scale_ref[pl.ds(grp, 1), :]) + shift_ref[pl.ds(grp, 1), :]
    o_ref[...] = h.astype(o_ref.dtype)
    if with_router:
        nt_dims = (((1,), (1,)), ((), ()))
        h_hi, h_lo = _split_bf16(h)
        r_hi, r_lo = _split_bf16(r_ref[...])
        logits = (lax.dot_general(r_hi, h_hi, nt_dims, preferred_element_type=F32)
                  + lax.dot_general(r_hi, h_lo, nt_dims, preferred_element_type=F32)
                  + lax.dot_general(r_lo, h_hi, nt_dims, preferred_element_type=F32))
        ids = lax.broadcasted_iota(I32, logits.shape, 0)
        m1 = jnp.max(logits, axis=0, keepdims=True)
        i1 = jnp.min(jnp.where(logits == m1, ids, N_EXPERTS), axis=0, keepdims=True)
        rest = jnp.where(ids == i1, -jnp.inf, logits)
        m2 = jnp.max(rest, axis=0, keepdims=True)
        i2 = jnp.min(jnp.where(rest == m2, ids, N_EXPERTS), axis=0, keepdims=True)
        e = jnp.exp(m2 - m1)
        w1 = 1.0 / (1.0 + e)
        w2 = e / (1.0 + e)
        out_row = lax.broadcasted_iota(I32, (8, tm), 0)
        route_ref[...] = jnp.where(out_row == 0, i1.astype(F32), jnp.where(
            out_row == 1, i2.astype(F32), jnp.where(out_row == 2, w1, jnp.where(out_row == 3, w2, 0.0))))


def norm_mod(xs, g, mod, layer, shift_chunk, scale_chunk, dims, *, m, router_t=None):
    d = dims.d_model
    tm = _pick_tile(math.gcd(dims.seq, dims.ctx), (256, 128))
    with_router = router_t is not None
    kern = functools.partial(_norm_mod_kernel, tm=tm, seq=dims.seq, n_groups=dims.batch + 1,
                             with_router=with_router)
    in_specs = [
        pl.BlockSpec((tm, d), lambda i: (i, 0)),
        pl.BlockSpec((1, d), lambda i: (0, 0)),
        pl.BlockSpec((8, d), lambda i: (MOD_ROWS // 8 * layer, shift_chunk)),
        pl.BlockSpec((8, d), lambda i: (MOD_ROWS // 8 * layer, scale_chunk)),
    ]
    args = [xs, g.reshape(1, d), mod, mod]
    out_shape = [jax.ShapeDtypeStruct((m, d), F32 if with_router else BF16)]
    out_specs = [pl.BlockSpec((tm, d), lambda i: (i, 0))]
    if with_router:
        in_specs.append(pl.BlockSpec((N_EXPERTS, d), lambda i: (0, 0)))
        args.append(router_t)
        out_shape.append(jax.ShapeDtypeStruct((8, m), F32))
        out_specs.append(pl.BlockSpec((8, tm), lambda i: (0, i)))
    res = pl.pallas_call(
        kern, out_shape=out_shape, grid=(m // tm,), in_specs=in_specs, out_specs=out_specs,
        compiler_params=_cparams(("arbitrary",)), name="norm_mod_router" if with_router else "norm_mod",
    )(*args)
    return res if with_router else res[0]


IN_TN = 512
J_QK = 2 * H_NA * DH_NA // IN_TN
J_V = H_NA * DH_NA // IN_TN
N_MLSTM = N_MAIN - OFF_MQ


def _in_proj_kernel(x_ref, g1_ref, shift_ref, scale_ref, w_ref, wgate_ref, gb_ref, qg_ref, kg_ref,
                    qk_ref, v_ref, pm_ref, gates_ref, a_ref, wbf_ref, *, tm, seq, n_groups):
    i = pl.program_id(0)
    j = pl.program_id(1)

    @pl.when(j == 0)
    def _():
        row = i * tm + lax.broadcasted_iota(I32, (tm, 1), 0)
        shift = _group_select(row, [shift_ref[k:k + 1, :] for k in range(n_groups)], seq)
        scale = _group_select(row, [scale_ref[k:k + 1, :] for k in range(n_groups)], seq)
        a = (_rms(x_ref[...], g1_ref[...]) * (1.0 + scale) + shift).astype(BF16)
        a_ref[...] = a
        gates_ref[...] = jnp.dot(a, wgate_ref[0].astype(BF16), preferred_element_type=F32) + gb_ref[...]

    wbf_ref[...] = w_ref[0].astype(BF16)
    acc = jnp.dot(a_ref[...], wbf_ref[...], preferred_element_type=F32)

    @pl.when(j < J_QK)
    def _():
        g = jnp.where(j < J_QK // 2, qg_ref[...] * (DH_NA ** -0.5), kg_ref[...])
        for h in range(IN_TN // DH_NA):
            sl = slice(h * DH_NA, (h + 1) * DH_NA)
            qk_ref[:, sl] = _rms(acc[:, sl], g).astype(BF16)

    @pl.when((j >= J_QK) & (j < J_QK + J_V))
    def _():
        v_ref[...] = acc.astype(BF16)

    @pl.when(j >= J_QK + J_V)
    def _():
        pm_ref[...] = acc


def in_proj(xs, norm_g, mod, w_in, w_gate_cols, gate_b_pad, layer, qg, kg, dims):
    m, d = xs.shape
    tm = _pick_tile(m, (1088, 640, 512, 256))
    nj = N_MAIN // IN_TN
    mod_row = MOD_ROWS // 8 * layer
    kern = functools.partial(_in_proj_kernel, tm=tm, seq=dims.seq, n_groups=dims.batch + 1)
    return pl.pallas_call(
        kern,
        out_shape=[jax.ShapeDtypeStruct((m, 2 * H_NA * DH_NA), BF16),
                   jax.ShapeDtypeStruct((m, H_NA * DH_NA), BF16),
                   jax.ShapeDtypeStruct((m, N_MLSTM), F32),
                   jax.ShapeDtypeStruct((m, LANES), F32)],
        grid=(m // tm, nj),
        in_specs=[pl.BlockSpec((tm, d), lambda i, j: (i, 0)),
                  pl.BlockSpec((1, d), lambda i, j: (0, 0)),
                  pl.BlockSpec((8, d), lambda i, j: (mod_row, 0)),
                  pl.BlockSpec((8, d), lambda i, j: (mod_row, 1)),
                  pl.BlockSpec((1, d, IN_TN), lambda i, j: (layer, 0, j)),
                  pl.BlockSpec((1, d, LANES), lambda i, j: (layer, 0, 0)),
                  pl.BlockSpec((None, 1, LANES), lambda i, j: (layer, 0, 0)),
                  pl.BlockSpec((1, DH_NA), lambda i, j: (0, 0)),
                  pl.BlockSpec((1, DH_NA), lambda i, j: (0, 0))],
        out_specs=[pl.BlockSpec((tm, IN_TN), lambda i, j: (i, jnp.minimum(j, J_QK - 1))),
                   pl.BlockSpec((tm, IN_TN), lambda i, j: (i, jnp.clip(j - J_QK, 0, J_V - 1))),
                   pl.BlockSpec((tm, IN_TN), lambda i, j: (i, jnp.maximum(j - (J_QK + J_V), 0))),
                   pl.BlockSpec((tm, LANES), lambda i, j: (i, 0))],
        scratch_shapes=[pltpu.VMEM((tm, d), BF16), pltpu.VMEM((d, IN_TN), BF16)],
        compiler_params=_cparams(("arbitrary", "arbitrary")), name="in_proj",
    )(xs, norm_g.reshape(1, d), mod, mod, w_in, w_gate_cols, gate_b_pad.reshape(-1, 1, LANES),
      qg.reshape(1, DH_NA), kg.reshape(1, DH_NA))


NT_DIMS = (((1,), (1,)), ((), ()))
TN_DIMS = (((0,), (0,)), ((), ()))
NA_QROWS = 4
NA_HEADS = 2
NA_KROWS = NA_QROWS + WIN_R
N_BIAS_PATTERNS = 3


def _na_block_start(qb, rows):
    return jnp.clip(qb * NA_QROWS - WIN_R // 2, 0, rows - NA_KROWS)


def _na_kernel(q_ref, k_ref, v_ref, kc_ref, vc_ref, bias_ref, o_ref, *, rows):
    qb = pl.program_id(2)
    nblk = rows // NA_QROWS
    pid = jnp.where(qb == 0, 0, jnp.where(qb == nblk - 1, 2, 1))
    start = pl.multiple_of(_na_block_start(qb, rows) * GRID_W, GRID_W)
    for hh in range(NA_HEADS):
        sl = slice(hh * DH_NA, (hh + 1) * DH_NA)
        q = q_ref[:, sl]
        kb = k_ref[pl.ds(start, NA_KROWS * GRID_W), sl]
        vb = v_ref[pl.ds(start, NA_KROWS * GRID_W), sl]
        s = lax.dot_general(q, kb, NT_DIMS, preferred_element_type=F32) + bias_ref[hh, pid]
        sc = lax.dot_general(q, kc_ref[:, sl], NT_DIMS, preferred_element_type=F32)
        m = jnp.maximum(jnp.max(s, axis=-1, keepdims=True), jnp.max(sc, axis=-1, keepdims=True))
        pb = jnp.exp(s - m)
        pc = jnp.exp(sc - m)
        den = jnp.sum(pb, axis=-1, keepdims=True) + jnp.sum(pc, axis=-1, keepdims=True)
        o = (jnp.dot(pb.astype(BF16), vb, preferred_element_type=F32)
             + jnp.dot(pc.astype(BF16), vc_ref[:, sl], preferred_element_type=F32))
        o_ref[:, sl] = (o / den).astype(o_ref.dtype)


def na_bias_table(rpb, rows):
    nblk = rows // NA_QROWS
    qb = np.array([0, 2, nblk - 1])
    ks = np.clip(qb * NA_QROWS - WIN_R // 2, 0, rows - NA_KROWS)
    r = qb[:, None] * NA_QROWS + np.arange(NA_QROWS)[None, :]
    rs = np.clip(r - WIN_R // 2, 0, rows - WIN_R)
    kr = ks[:, None] + np.arange(NA_KROWS)[None, :]
    in_band = (kr[:, None, :] >= rs[:, :, None]) & (kr[:, None, :] < rs[:, :, None] + WIN_R)
    dr = kr[:, None, :] - r[:, :, None] + (WIN_R - 1)
    c = np.arange(GRID_W)
    dc = np.clip(c[None, :] - c[:, None], -(WIN_C - 1), WIN_C - 1) + (WIN_C - 1)
    cs = np.clip(c - WIN_C // 2, 0, GRID_W - WIN_C)
    col_ok = (c[None, :] >= cs[:, None]) & (c[None, :] < cs[:, None] + WIN_C)
    sel_r = jnp.asarray((dr[..., None] == np.arange(2 * WIN_R - 1)) & in_band[..., None], F32)
    sel_c = jnp.asarray(dc[..., None] == np.arange(2 * WIN_C - 1), F32)
    hi = lax.Precision.HIGHEST
    t = jnp.einsum("lhab,pxja->lhpxjb", rpb.astype(F32), sel_r, precision=hi)
    bias = jnp.einsum("lhpxjb,qkb->lhpxqjk", t, sel_c, precision=hi)
    ok = in_band[:, :, None, :, None] & col_ok[None, None, :, None, :]
    bias = jnp.where(jnp.asarray(ok)[None, None], bias, NEG)
    return bias.reshape(rpb.shape[0], H_NA, N_BIAS_PATTERNS, NA_QROWS * GRID_W, NA_KROWS * GRID_W)


def na_attention(qk, vbf, bias, layer, dims):
    rows = dims.seq // GRID_W
    qblocks = rows // NA_QROWS
    tq = NA_QROWS * GRID_W
    ctx_blk0 = dims.m_x // dims.ctx
    hgroups = H_NA // NA_HEADS
    hw = NA_HEADS * DH_NA
    kern = functools.partial(_na_kernel, rows=rows)
    return pl.pallas_call(
        kern,
        out_shape=jax.ShapeDtypeStruct((dims.m_x, H_NA * DH_NA), BF16),
        grid=(dims.batch, hgroups, qblocks),
        in_specs=[
            pl.BlockSpec((tq, hw), lambda b, h, i: (b * qblocks + i, h)),
            pl.BlockSpec((dims.seq, hw), lambda b, h, i: (b, hgroups + h)),
            pl.BlockSpec((dims.seq, hw), lambda b, h, i: (b, h)),
            pl.BlockSpec((dims.ctx, hw), lambda b, h, i: (ctx_blk0 + b, hgroups + h)),
            pl.BlockSpec((dims.ctx, hw), lambda b, h, i: (ctx_blk0 + b, h)),
            pl.BlockSpec((None, NA_HEADS, N_BIAS_PATTERNS, tq, NA_KROWS * GRID_W),
                         lambda b, h, i: (layer, h, 0, 0, 0)),
        ],
        out_specs=pl.BlockSpec((tq, hw), lambda b, h, i: (b * qblocks + i, h)),
        compiler_params=_cparams(("arbitrary", "arbitrary", "arbitrary")), name="na_attention",
    )(qk, qk, vbf, qk, vbf, bias)


def _ctx_attn_kernel(q_ref, k_ref, v_ref, o_ref):
    s = lax.dot_general(q_ref[...], k_ref[...], NT_DIMS, preferred_element_type=F32)
    m = jnp.max(s, axis=-1, keepdims=True)
    p = jnp.exp(s - m)
    den = jnp.sum(p, axis=-1, keepdims=True)
    o = jnp.dot(p.astype(BF16), v_ref[...], preferred_element_type=F32)
    o_ref[...] = (o / den).astype(o_ref.dtype)


def ctx_attention(qk, vbf, dims):
    ctx_blk0 = dims.m_x // dims.ctx
    return pl.pallas_call(
        _ctx_attn_kernel,
        out_shape=jax.ShapeDtypeStruct((dims.batch * dims.ctx, H_NA * DH_NA), BF16),
        grid=(dims.batch, H_NA),
        in_specs=[
            pl.BlockSpec((dims.ctx, DH_NA), lambda b, h: (ctx_blk0 + b, h)),
            pl.BlockSpec((dims.ctx, DH_NA), lambda b, h: (ctx_blk0 + b, H_NA + h)),
            pl.BlockSpec((dims.ctx, DH_NA), lambda b, h: (ctx_blk0 + b, h)),
        ],
        out_specs=pl.BlockSpec((dims.ctx, DH_NA), lambda b, h: (b, h)),
        compiler_params=_cparams(("arbitrary", "arbitrary")), name="ctx_attention",
    )(qk, qk, vbf)


def _log_sigmoid(x):
    return jnp.minimum(x, 0.0) - jnp.log(1.0 + jnp.exp(-jnp.abs(x)))


def _rope(t, cos, sin_signed, even):
    swapped = jnp.where(even, pltpu.roll(t, DK_M - 1, 1), pltpu.roll(t, 1, 1))
    return t * cos + swapped * sin_signed


def _mlstm_kernel(q_ref, k_ref, v_ref, g_ref, gt_ref, cos_ref, sin_ref, o_ref, c_ref, m_ref):
    direction = pl.program_id(1)
    t_step = pl.program_id(2)

    @pl.when(t_step == 0)
    def _():
        c_ref[...] = jnp.zeros(c_ref.shape, F32)
        m_ref[...] = jnp.zeros(m_ref.shape, F32)

    ti = lax.broadcasted_iota(I32, (CHUNK, CHUNK), 0)
    si = lax.broadcasted_iota(I32, (CHUNK, CHUNK), 1)
    mask = (si - ti) * (1 - 2 * direction) <= 0
    tri = mask.astype(BF16)
    g = g_ref[0]
    gt = gt_ref[0]
    lf_hi, lf_lo = _split_bf16(_log_sigmoid(g))
    cum_col = (jnp.dot(tri, lf_hi, preferred_element_type=F32)
               + jnp.dot(tri, lf_lo, preferred_element_type=F32))
    lft = _log_sigmoid(gt)
    lft_hi, lft_lo = _split_bf16(lft)
    cum_row = (lax.dot_general(lft_hi, tri, NT_DIMS, preferred_element_type=F32)
               + lax.dot_general(lft_lo, tri, NT_DIMS, preferred_element_type=F32))
    cos = cos_ref[...]
    sin = sin_ref[...]
    even = (lax.broadcasted_iota(I32, (CHUNK, DK_M), 1) % 2) == 0
    ones_col = (lax.broadcasted_iota(I32, (CHUNK, DV_EXT - DV_M), 1) == 0).astype(BF16)
    for h in range(H_M):
        q = _rope(q_ref[:, h * DK_M:(h + 1) * DK_M], cos, sin, even)
        k = _rope(k_ref[:, h * DK_M:(h + 1) * DK_M], cos, sin, even) * (DK_M ** -0.5)
        v_ext = jnp.concatenate([v_ref[:, h * DV_M:(h + 1) * DV_M].astype(BF16), ones_col], axis=1)
        qb = q.astype(BF16)
        b_col = cum_col[:, H_M + h:H_M + h + 1]
        i_col = g[:, h:h + 1]
        b_row = cum_row[H_M + h:H_M + h + 1, :]
        i_row = gt[h:h + 1, :]
        b_last = jnp.sum(lft[H_M + h:H_M + h + 1, :], axis=1, keepdims=True)
        m_old = m_ref[h][0:1, 0:1]
        c_old = c_ref[h]
        d_end_row = b_last - b_row + i_row
        m_new = jnp.maximum(b_last + m_old, jnp.max(d_end_row, axis=1, keepdims=True))
        w_end = jnp.exp(b_last - b_col + i_col - m_new)
        decay = jnp.exp(b_last + m_old - m_new)
        kw = (k * w_end).astype(BF16)
        c_ref[h] = decay * c_old + lax.dot_general(kw, v_ext, TN_DIMS, preferred_element_type=F32)
        m_ref[h] = jnp.broadcast_to(m_new, m_ref.shape[1:])
        d_mat = jnp.where(mask, b_col - b_row + i_row, NEG)
        m_in = b_col + m_old
        m_t = jnp.maximum(m_in, jnp.max(d_mat, axis=1, keepdims=True))
        s = lax.dot_general(qb, k.astype(BF16), NT_DIMS, preferred_element_type=F32) * jnp.exp(d_mat - m_t)
        a_in = jnp.exp(m_in - m_t)
        numden = (jnp.dot(s.astype(BF16), v_ext, preferred_element_type=F32)
                  + a_in * jnp.dot(qb, c_old.astype(BF16), preferred_element_type=F32))
        den = numden[:, DV_M:DV_M + 1]
        o_ref[0, :, h * DV_M:(h + 1) * DV_M] = numden[:, :DV_M] / jnp.maximum(jnp.abs(den), jnp.exp(-m_t))


def mlstm_scan(p, g_dir, gt_dir, cos_t, sin_t, dims):
    ncx = dims.ctx // CHUNK
    nlx = dims.seq // CHUNK
    x_blocks = dims.m_x // CHUNK

    def row_block(b, d, t):
        in_ctx = t < ncx
        cc = jnp.where(d == 0, t, ncx - 1 - t)
        cx = jnp.where(d == 0, t - ncx, nlx - 1 - (t - ncx))
        return jnp.where(in_ctx, x_blocks + b * ncx + cc, b * nlx + cx)

    def rope_block(b, d, t):
        in_ctx = t < ncx
        cx = jnp.where(d == 0, t - ncx, nlx - 1 - (t - ncx))
        return jnp.where(in_ctx, nlx, cx)

    return pl.pallas_call(
        _mlstm_kernel,
        out_shape=jax.ShapeDtypeStruct((2, dims.m_all, H_M * DV_M), F32),
        grid=(dims.batch, 2, ncx + nlx),
        in_specs=[
            pl.BlockSpec((CHUNK, H_M * DK_M), lambda b, d, t: (row_block(b, d, t), PM_Q // (H_M * DK_M))),
            pl.BlockSpec((CHUNK, H_M * DK_M), lambda b, d, t: (row_block(b, d, t), PM_K // (H_M * DK_M))),
            pl.BlockSpec((CHUNK, H_M * DV_M), lambda b, d, t: (row_block(b, d, t), PM_V // (H_M * DV_M))),
            pl.BlockSpec((1, CHUNK, LANES), lambda b, d, t: (d, row_block(b, d, t), 0)),
            pl.BlockSpec((1, 8, CHUNK), lambda b, d, t: (d, 0, row_block(b, d, t))),
            pl.BlockSpec((CHUNK, DK_M), lambda b, d, t: (rope_block(b, d, t), 0)),
            pl.BlockSpec((CHUNK, DK_M), lambda b, d, t: (rope_block(b, d, t), 0)),
        ],
        out_specs=pl.BlockSpec((1, CHUNK, H_M * DV_M), lambda b, d, t: (d, row_block(b, d, t), 0)),
        scratch_shapes=[pltpu.VMEM((H_M, DK_M, DV_EXT), F32), pltpu.VMEM((H_M, 8, LANES), F32)],
        compiler_params=_cparams(("arbitrary", "arbitrary", "arbitrary")), name="mlstm_scan",
    )(p, p, p, g_dir, gt_dir, cos_t, sin_t)


def _mlstm_out_kernel(h_ref, o_ref, g_ref, out_ref):
    for h in range(H_M):
        sl = slice(h * DV_M, (h + 1) * DV_M)
        y = _rms(h_ref[0, :, sl] + h_ref[1, :, sl], g_ref[:, sl])
        out_ref[:, sl] = (y * jax.nn.sigmoid(o_ref[:, sl])).astype(out_ref.dtype)


def mlstm_out(hdir, p, g, dims):
    m = dims.m_all
    tm = 256
    n = H_M * DV_M
    return pl.pallas_call(
        _mlstm_out_kernel,
        out_shape=jax.ShapeDtypeStruct((m, n), BF16),
        grid=(m // tm,),
        in_specs=[pl.BlockSpec((2, tm, n), lambda i: (0, i, 0)),
                  pl.BlockSpec((tm, n), lambda i: (i, PM_O // n)),
                  pl.BlockSpec((1, n), lambda i: (0, 0))],
        out_specs=pl.BlockSpec((tm, n), lambda i: (i, 0)),
        compiler_params=_cparams(("arbitrary",)), name="mlstm_out",
    )(hdir, p, g.reshape(1, n))


def rope_tables(dims):
    t = jnp.arange(dims.seq)
    row = (t // GRID_W).astype(F32)
    col = (t % GRID_W).astype(F32)
    quarter = DK_M // 4
    inv = ROPE_BASE ** (-jnp.arange(quarter, dtype=F32) / quarter)
    ang = jnp.concatenate([row[:, None] * inv, col[:, None] * inv], axis=-1)
    cos = jnp.repeat(jnp.cos(ang), 2, axis=-1)
    sin = jnp.repeat(jnp.sin(ang), 2, axis=-1) * jnp.tile(jnp.array([-1.0, 1.0], F32), DK_M // 2)
    cos = jnp.concatenate([cos, jnp.ones((CHUNK, DK_M), F32)], axis=0)
    sin = jnp.concatenate([sin, jnp.zeros((CHUNK, DK_M), F32)], axis=0)
    return cos, sin


def _gather_kernel(src_ref, nrows_ref, h_hbm, o_ref, buf_ref, sem, *, tm):
    i = pl.program_id(0)
    n = nrows_ref[i]

    @pl.when(n < tm)
    def _():
        buf_ref[...] = jnp.zeros(buf_ref.shape, buf_ref.dtype)

    def start(r, carry):
        pltpu.make_async_copy(h_hbm.at[pl.ds(src_ref[i * tm + r], 1)], buf_ref.at[pl.ds(r, 1)], sem).start()
        return carry

    lax.fori_loop(0, n, start, 0)

    def wait(r, carry):
        pltpu.make_async_copy(h_hbm.at[pl.ds(0, 1)], buf_ref.at[pl.ds(r, 1)], sem).wait()
        return carry

    lax.fori_loop(0, n, wait, 0)
    o_ref[...] = buf_ref[...].astype(o_ref.dtype)


def gather_rows(h, src, tile_rows, *, tm):
    m, d = h.shape
    r_total = src.shape[0]
    return pl.pallas_call(
        functools.partial(_gather_kernel, tm=tm),
        out_shape=jax.ShapeDtypeStruct((r_total, d), BF16),
        grid_spec=pltpu.PrefetchScalarGridSpec(
            num_scalar_prefetch=2,
            grid=(r_total // tm,),
            in_specs=[pl.BlockSpec(memory_space=pl.ANY)],
            out_specs=pl.BlockSpec((tm, d), lambda i, *s: (i, 0)),
            scratch_shapes=[pltpu.VMEM((tm, d), F32), pltpu.SemaphoreType.DMA],
        ),
        compiler_params=_cparams(("arbitrary",)), name="moe_gather",
    )(src, tile_rows, h)


def _combine_kernel(p1_ref, p2_ref, y_hbm, x_ref, gate_ref, rt_ref, o_ref, y1_ref, y2_ref, sem,
                    *, tm, seq, n_groups):
    i = pl.program_id(0)

    def start(r, carry):
        t = i * tm + r
        pltpu.make_async_copy(y_hbm.at[pl.ds(p1_ref[t], 1)], y1_ref.at[pl.ds(r, 1)], sem.at[0]).start()
        pltpu.make_async_copy(y_hbm.at[pl.ds(p2_ref[t], 1)], y2_ref.at[pl.ds(r, 1)], sem.at[1]).start()
        return carry

    lax.fori_loop(0, tm, start, 0)

    def wait(r, carry):
        pltpu.make_async_copy(y_hbm.at[pl.ds(0, 1)], y1_ref.at[pl.ds(r, 1)], sem.at[0]).wait()
        pltpu.make_async_copy(y_hbm.at[pl.ds(0, 1)], y2_ref.at[pl.ds(r, 1)], sem.at[1]).wait()
        return carry

    lax.fori_loop(0, tm, wait, 0)
    row0 = i * tm
    grp = jnp.int32(n_groups - 1)
    for b in reversed(range(n_groups - 1)):
        grp = jnp.where(row0 < (b + 1) * seq, b, grp)
    gate = gate_ref[pl.ds(grp, 1), :]
    w1 = rt_ref[:, 2:3]
    w2 = rt_ref[:, 3:4]
    o_ref[...] = x_ref[...] + gate * (w1 * y1_ref[...] + w2 * y2_ref[...])


def moe_combine(xs, y, pos1, pos2, route_t, mod, layer, gate_chunk, dims, *, m):
    d = dims.d_model
    tm = _pick_tile(math.gcd(dims.seq, dims.ctx), (256, 128))
    kern = functools.partial(_combine_kernel, tm=tm, seq=dims.seq, n_groups=dims.batch + 1)
    return pl.pallas_call(
        kern,
        out_shape=jax.ShapeDtypeStruct((m, d), F32),
        grid_spec=pltpu.PrefetchScalarGridSpec(
            num_scalar_prefetch=2,
            grid=(m // tm,),
            in_specs=[pl.BlockSpec(memory_space=pl.ANY),
                      pl.BlockSpec((tm, d), lambda i, a, b: (i, 0)),
                      pl.BlockSpec((8, d), lambda i, a, b: (MOD_ROWS // 8 * layer, gate_chunk)),
                      pl.BlockSpec((tm, 8), lambda i, a, b: (i, 0))],
            out_specs=pl.BlockSpec((tm, d), lambda i, a, b: (i, 0)),
            scratch_shapes=[pltpu.VMEM((tm, d), F32), pltpu.VMEM((tm, d), F32),
                            pltpu.SemaphoreType.DMA((2,))],
        ),
        compiler_params=_cparams(("arbitrary",)), name="moe_combine",
    )(pos1, pos2, y, xs, mod, route_t)


def moe_dispatch_plan(route, m, tm):
    e = jnp.concatenate([route[0, :m], route[1, :m]]).astype(I32)
    tok = jnp.tile(jnp.arange(m, dtype=I32), 2)
    onehot = (e[:, None] == jnp.arange(N_EXPERTS, dtype=I32)[None, :]).astype(I32)
    rank = jnp.sum((jnp.cumsum(onehot, axis=0) - onehot) * onehot, axis=1)
    counts = jnp.sum(onehot, axis=0)
    padded = ((counts + tm - 1) // tm) * tm
    ends = jnp.cumsum(padded)
    offs = ends - padded
    pos = offs[e] + rank
    r_total = TOP_K * m + N_EXPERTS * tm
    src = jnp.zeros((r_total,), I32).at[pos].set(tok)
    tile_start = jnp.arange(r_total // tm, dtype=I32) * tm
    last_e = jnp.max(jnp.where(counts > 0, jnp.arange(N_EXPERTS, dtype=I32), 0))
    tile_e = jnp.minimum(jnp.sum((tile_start[:, None] >= ends[None, :]).astype(I32), axis=1), last_e)
    n_valid = (ends[-1] // tm).astype(I32).reshape(1)
    tile_rows = jnp.clip((offs + counts)[tile_e] - tile_start, 0, tm).astype(I32)
    return src, pos[:m], pos[m:], tile_e, n_valid, tile_rows


def _dense_tile(m, candidates):
    return _pick_tile(m, candidates)


def kernel(x, c, ctx, c_ctx, ada_w, ada_b, norm1_g, norm2_g, w_in, gate_b, na_q_g, na_k_g, na_rpb, m_norm_g, w_out, ffn_w_gate, ffn_w_up, ffn_w_down, moe_router, moe_w_gate, moe_w_up, moe_w_down):
    batch, seq, d = x.shape
    dims = Dims(batch=batch, seq=seq, ctx=ctx.shape[1], d_model=d, d_ff=ffn_w_gate.shape[2], depth=ada_w.shape[0])
    depth = dims.depth
    n_groups = batch + 1
    rows = seq // GRID_W
    assert rows >= 4 * NA_QROWS and rows % NA_QROWS == 0 and batch + 1 <= 8
    assert seq % 256 == 0 and dims.ctx % CHUNK == 0 and dims.m_x % dims.ctx == 0

    xs = jnp.concatenate([x.reshape(dims.m_x, d), ctx.reshape(batch * dims.ctx, d)], axis=0)

    cond = jnp.concatenate([jax.nn.silu(c), jax.nn.silu(c_ctx)[None], jnp.zeros((MOD_ROWS - n_groups, d), F32)], axis=0)
    tn_ada = _pick_tile(6 * d, (1024, 512, 256))
    tn_d = min(512, d)
    mod = gmm((cond.astype(BF16),), (ada_w,), jnp.arange(depth, dtype=I32), jnp.full((1,), depth, I32),
              tm=MOD_ROWS, tn=tn_ada, n_out=6 * d, out_dtype=F32, epilogue="bias",
              extras=(ada_b.reshape(depth, 1, 6 * d),),
              extra_specs=(((1, 1, tn_ada), lambda j, i, g, *s: (g[i], 0, j)),),
              a_fixed=True, name="ada_mod")

    cos_t, sin_t = rope_tables(dims)
    na_bias = na_bias_table(na_rpb, rows)
    w_gate_cols = jnp.pad(w_in[:, :, N_MAIN:], ((0, 0), (0, 0), (0, LANES - 4 * H_M)))
    gate_b_pad = jnp.pad(gate_b, ((0, 0), (0, LANES - 4 * H_M)))

    def dense_groups(m, tm, g):
        nt = m // tm
        return jnp.full((nt,), g, I32), jnp.full((1,), nt, I32)

    for layer in range(depth):
        last = layer == depth - 1
        m_out = dims.m_x if last else dims.m_all

        qk, vbf, pm, gates = in_proj(xs, norm1_g[layer], mod, w_in, w_gate_cols, gate_b_pad, layer,
                                     na_q_g[layer], na_k_g[layer], dims)
        g_dir = jnp.stack([gates, jnp.roll(gates, -2 * H_M, axis=1)])
        gt_dir = jnp.stack([gates[:, :2 * H_M].T, gates[:, 2 * H_M:4 * H_M].T])

        na_out = na_attention(qk, vbf, na_bias, layer, dims)
        hdir = mlstm_scan(pm, g_dir, gt_dir, cos_t, sin_t, dims)
        m_mix = mlstm_out(hdir, pm, m_norm_g[layer], dims)
        if not last:
            na_out = jnp.concatenate([na_out, ctx_attention(qk, vbf, dims)], axis=0)

        tm_o = _dense_tile(m_out, (1088, 1024, 640, 512, 256))
        grp, nv = dense_groups(m_out, tm_o, layer)
        res_specs = lambda tm, tn, chunk: (
            ((tm, tn), lambda j, i, *s: (i, j)),
            ((8, tn), lambda j, i, *s: (MOD_ROWS // 8 * layer, chunk * (d // tn) + j)))
        tn_o = min(1024, d)
        xs = gmm((na_out, m_mix), (w_out,), grp, nv, tm=tm_o, tn=tn_o, n_out=d, out_dtype=F32, epilogue="residual",
                 extras=(xs, mod), extra_specs=res_specs(tm_o, tn_o, 2), seq=seq, n_groups=n_groups,
                 name="w_out_proj")

        idx = layer // 2
        if layer % 2 == 0:
            h2 = norm_mod(xs, norm2_g[layer], mod, layer, 3, 4, dims, m=m_out)
            tm_u = _dense_tile(m_out, (1088, 1024, 640, 512, 256))
            grp, nv = dense_groups(m_out, tm_u, idx)
            act = gmm_swiglu(h2, ffn_w_gate, ffn_w_up, grp, nv, tm=tm_u, tn=512, name="ffn_up")
            tm_d = _dense_tile(m_out, (544, 512, 320, 256))
            grp, nv = dense_groups(m_out, tm_d, idx)
            xs = gmm((act,), (ffn_w_down,), grp, nv, tm=tm_d, tn=tn_d, n_out=d, out_dtype=F32, epilogue="residual",
                     extras=(xs, mod), extra_specs=res_specs(tm_d, tn_d, 5), seq=seq, n_groups=n_groups,
                     name="ffn_down")
        else:
            h2, route = norm_mod(xs, norm2_g[layer], mod, layer, 3, 4, dims, m=m_out,
                                 router_t=moe_router[idx].T)
            tm_e = 512 if m_out >= 4096 else 128
            src, pos1, pos2, tile_e, n_valid, tile_rows = moe_dispatch_plan(route, m_out, tm_e)
            hs = gather_rows(h2, src, tile_rows, tm=tm_e)
            grp = tile_e + idx * N_EXPERTS
            nw = moe_w_gate.shape[0] * N_EXPERTS
            act = gmm_swiglu(hs, moe_w_gate.reshape(nw, d, dims.d_ff), moe_w_up.reshape(nw, d, dims.d_ff),
                             grp, n_valid, tm=tm_e, tn=512, name="moe_up")
            y = gmm((act,), (moe_w_down.reshape(nw, dims.d_ff, d),), grp, n_valid, tm=tm_e, tn=tn_d, n_out=d,
                    out_dtype=F32, name="moe_down")
            xs = moe_combine(xs, y, pos1, pos2, route.T, mod, layer, 5, dims, m=m_out)

    return xs[:dims.m_x].reshape(batch, seq, d)
```

```python
import functools
import math
from typing import NamedTuple

import jax
import jax.numpy as jnp
import numpy as np
from jax import lax
from jax.experimental import pallas as pl
from jax.experimental.pallas import tpu as pltpu

F32 = jnp.float32
BF16 = jnp.bfloat16
I32 = jnp.int32

GRID_W = 64
H_NA = 8
DH_NA = 128
WIN_R = 8
WIN_C = 16
H_M = 4
DK_M = 128
DV_M = 256
CHUNK = 128
ROPE_BASE = 10000.0
N_EXPERTS = 8
TOP_K = 2
EPS = 1e-6
D_MIX = 2048
NEG = -1e30

OFF_MQ = 3 * H_NA * DH_NA
N_MAIN = 6144
PM_Q, PM_K, PM_V, PM_O = 0, H_M * DK_M, 2 * H_M * DK_M, 2 * H_M * DK_M + H_M * DV_M
DV_EXT = DV_M + 128

LANES = 128
VMEM_LIMIT = 56 * 1024 * 1024
MOD_ROWS = 16


class Dims(NamedTuple):
    batch: int
    seq: int
    ctx: int
    d_model: int
    d_ff: int
    depth: int

    @property
    def m_x(self):
        return self.batch * self.seq

    @property
    def m_all(self):
        return self.batch * (self.seq + self.ctx)


def _cparams(sem):
    return pltpu.CompilerParams(dimension_semantics=sem, vmem_limit_bytes=VMEM_LIMIT)


def _group_select(row, vals, seq):
    out = vals[-1]
    for b in reversed(range(len(vals) - 1)):
        out = jnp.where(row < (b + 1) * seq, vals[b], out)
    return out


def _pick_tile(m, candidates):
    for t in candidates:
        if m % t == 0:
            return t
    raise ValueError(f"no row tile for {m}")


N_EXTRA = {"residual": 2, "bias": 1, "swiglu": 0, None: 0}


def _gmm_kernel(grp_ref, rows_ref, rid_ref, rgrp_ref, nruns_ref, *refs,
                n_a, n_w, epilogue, tm, tn, nj, seq, n_groups, rows_outer, half_tiles):
    a_refs, refs = refs[:n_a], refs[n_a:]
    w_refs, refs = refs[:n_w], refs[n_w:]
    x_refs, refs = refs[:N_EXTRA[epilogue]], refs[N_EXTRA[epilogue]:]
    o_ref, refs = refs[0], refs[1:]
    wbf_refs, refs = refs[:n_w], refs[n_w:]
    if rows_outer:
        i = pl.program_id(0)
        for w_ref, wbf_ref in zip(w_refs, wbf_refs):
            wbf_ref[...] = w_ref[0].astype(BF16)
    else:
        wst_refs, sem = refs[:n_w], refs[n_w]
        j = pl.program_id(0)
        i = pl.program_id(1)
        rid = rid_ref[i]
        nruns = nruns_ref[0]

        def tile_copy(widx, g, jj, slot):
            src = w_refs[widx].at[g, :, pl.ds(pl.multiple_of(jj * tn, tn), tn)]
            return pltpu.make_async_copy(src, wst_refs[widx].at[slot], sem.at[widx, slot])

        @pl.when((i == 0) | (rid != rid_ref[jnp.maximum(i - 1, 0)]))
        def _():
            q = j * nruns + rid
            slot = q % 2

            @pl.when(q == 0)
            def _():
                for widx in range(n_w):
                    tile_copy(widx, rgrp_ref[0], 0, 0).start()

            for widx in range(n_w):
                tile_copy(widx, 0, 0, slot).wait()
            wrap = rid + 1 == nruns
            next_run = jnp.where(wrap, 0, rid + 1)
            next_j = j + wrap.astype(I32)

            @pl.when(next_j < nj)
            def _():
                for widx in range(n_w):
                    tile_copy(widx, rgrp_ref[next_run], next_j, 1 - slot).start()

            for widx in range(n_w):
                wbf_refs[widx][...] = wst_refs[widx][slot].astype(BF16)

    n_rows = rows_ref[i]

    def compute(nr):
        def matmul(wbf_ref):
            out, k0 = None, 0
            for a_ref in a_refs:
                kp = a_ref.shape[1]
                part = jnp.dot(a_ref[0:nr, :], wbf_ref[k0:k0 + kp, :], preferred_element_type=F32)
                out = part if out is None else out + part
                k0 += kp
            return out

        acc = matmul(wbf_refs[0])
        if epilogue == "residual":
            res_ref, gate_ref = x_refs
            row = i * tm + lax.broadcasted_iota(I32, (nr, 1), 0)
            gate = _group_select(row, [gate_ref[k:k + 1, :] for k in range(n_groups)], seq)
            acc = res_ref[0:nr, :] + gate * acc
        elif epilogue == "bias":
            acc = acc + x_refs[0][0]
        elif epilogue == "swiglu":
            acc = acc * jax.nn.sigmoid(acc) * matmul(wbf_refs[1])
        o_ref[0:nr, :] = acc.astype(o_ref.dtype)
        if nr < tm:
            o_ref[nr:, :] = jnp.zeros((tm - nr, o_ref.shape[1]), o_ref.dtype)

    half = tm // 2 if half_tiles else 0
    pl.when(n_rows > half)(lambda: compute(tm))
    if half_tiles:
        pl.when((n_rows > 0) & (n_rows <= half))(lambda: compute(half))

    @pl.when(n_rows == 0)
    def _():
        o_ref[...] = jnp.zeros(o_ref.shape, o_ref.dtype)


def _runs(grp):
    nt = grp.shape[0]
    first = jnp.concatenate([jnp.ones((1,), I32), (grp[1:] != grp[:-1]).astype(I32)])
    run_id = jnp.cumsum(first) - 1
    sel = (run_id[:, None] == jnp.arange(nt, dtype=I32)[None, :]).astype(I32) * first[:, None]
    run_grp = jnp.sum(sel * grp[:, None], axis=0)
    return run_id.astype(I32), run_grp.astype(I32), (run_id[-1:] + 1).astype(I32)


def gmm(a_parts, ws, grp, tile_rows, *, tm, tn, n_out, out_dtype, epilogue=None, extras=(), extra_specs=(),
        a_fixed=False, seq=0, n_groups=0, rows_outer=False, half_tiles=False, name="gmm"):
    k = sum(a.shape[1] for a in a_parts)
    nt = grp.shape[0]
    n_w = len(ws)
    nj = n_out // tn

    def spec(shape, fn):
        return pl.BlockSpec(shape, (lambda i, j, *s: fn(j, i, *s)) if rows_outer else fn)

    a_map = (lambda j, i, *s: (0, 0)) if a_fixed else (lambda j, i, *s: (i, 0))
    if rows_outer:
        w_specs = [spec((1, k, tn), lambda j, i, g, *s: (g[i], 0, j))] * n_w
        stream_scratch = []
    else:
        w_specs = [pl.BlockSpec(memory_space=pl.ANY)] * n_w
        stream_scratch = [pltpu.VMEM((2, k, tn), F32)] * n_w + [pltpu.SemaphoreType.DMA((n_w, 2))]
    in_specs = ([spec((tm, a.shape[1]), a_map) for a in a_parts] + w_specs
                + [spec(shape, fn) for shape, fn in extra_specs])
    kern = functools.partial(_gmm_kernel, n_a=len(a_parts), n_w=n_w, epilogue=epilogue, tm=tm, tn=tn, nj=nj,
                             seq=seq, n_groups=n_groups, rows_outer=rows_outer, half_tiles=half_tiles)
    run_id, run_grp, nruns = _runs(grp)
    return pl.pallas_call(
        kern,
        out_shape=jax.ShapeDtypeStruct((nt * tm, n_out), out_dtype),
        grid_spec=pltpu.PrefetchScalarGridSpec(
            num_scalar_prefetch=5,
            grid=(nt, nj) if rows_outer else (nj, nt),
            in_specs=in_specs,
            out_specs=spec((tm, tn), lambda j, i, *s: (i, j)),
            scratch_shapes=[pltpu.VMEM((k, tn), BF16)] * n_w + stream_scratch,
        ),
        compiler_params=_cparams(("arbitrary", "arbitrary")),
        name=name,
    )(grp, tile_rows, run_id, run_grp, nruns, *a_parts, *ws, *extras)


def gmm_swiglu(a, wg, wu, grp, tile_rows, *, tm, tn, half_tiles=False, name="gmm_swiglu"):
    return gmm((a,), (wg, wu), grp, tile_rows, tm=tm, tn=tn, n_out=wg.shape[2], out_dtype=BF16,
               epilogue="swiglu", half_tiles=half_tiles, name=name)


def _rms(x, g):
    return x * lax.rsqrt(jnp.mean(x * x, axis=-1, keepdims=True) + EPS) * g


def _split_bf16(x):
    hi = x.astype(BF16)
    lo = (x - hi.astype(F32)).astype(BF16)
    return hi, lo


def _norm_mod_kernel(x_ref, g_ref, shift_ref, scale_ref, *refs, tm, seq, n_groups, with_router):
    if with_router:
        r_ref, o_ref, route_ref = refs
    else:
        (o_ref,) = refs
    i = pl.program_id(0)
    row0 = i * tm
    grp = jnp.int32(n_groups - 1)
    for b in reversed(range(n_groups - 1)):
        grp = jnp.where(row0 < (b + 1) * seq, b, grp)
    y = _rms(x_ref[...], g_ref[...])
    h = y * (1.0 + scale_ref[pl.ds(grp, 1), :]) + shift_ref[pl.ds(grp, 1), :]
    o_ref[...] = h.astype(o_ref.dtype)
    if with_router:
        nt_dims = (((1,), (1,)), ((), ()))
        h_hi, h_lo = _split_bf16(h)
        r_hi, r_lo = _split_bf16(r_ref[...])
        logits = (lax.dot_general(r_hi, h_hi, nt_dims, preferred_element_type=F32)
                  + lax.dot_general(r_hi, h_lo, nt_dims, preferred_element_type=F32)
                  + lax.dot_general(r_lo, h_hi, nt_dims, preferred_element_type=F32))
        ids = lax.broadcasted_iota(I32, logits.shape, 0)
        m1 = jnp.max(logits, axis=0, keepdims=True)
        i1 = jnp.min(jnp.where(logits == m1, ids, N_EXPERTS), axis=0, keepdims=True)
        rest = jnp.where(ids == i1, -jnp.inf, logits)
        m2 = jnp.max(rest, axis=0, keepdims=True)
        i2 = jnp.min(jnp.where(rest == m2, ids, N_EXPERTS), axis=0, keepdims=True)
        e = jnp.exp(m2 - m1)
        w1 = 1.0 / (1.0 + e)
        w2 = e / (1.0 + e)
        out_row = lax.broadcasted_iota(I32, (8, tm), 0)
        route_ref[...] = jnp.where(out_row == 0, i1.astype(F32), jnp.where(
            out_row == 1, i2.astype(F32), jnp.where(out_row == 2, w1, jnp.where(out_row == 3, w2, 0.0))))


def norm_mod(xs, g, mod, layer, shift_chunk, scale_chunk, dims, *, m, router_t=None):
    d = dims.d_model
    tm = _pick_tile(math.gcd(dims.seq, dims.ctx), (256, 128))
    with_router = router_t is not None
    kern = functools.partial(_norm_mod_kernel, tm=tm, seq=dims.seq, n_groups=dims.batch + 1,
                             with_router=with_router)
    in_specs = [
        pl.BlockSpec((tm, d), lambda i: (i, 0)),
        pl.BlockSpec((1, d), lambda i: (0, 0)),
        pl.BlockSpec((8, d), lambda i: (MOD_ROWS // 8 * layer, shift_chunk)),
        pl.BlockSpec((8, d), lambda i: (MOD_ROWS // 8 * layer, scale_chunk)),
    ]
    args = [xs, g.reshape(1, d), mod, mod]
    out_shape = [jax.ShapeDtypeStruct((m, d), F32 if with_router else BF16)]
    out_specs = [pl.BlockSpec((tm, d), lambda i: (i, 0))]
    if with_router:
        in_specs.append(pl.BlockSpec((N_EXPERTS, d), lambda i: (0, 0)))
        args.append(router_t)
        out_shape.append(jax.ShapeDtypeStruct((8, m), F32))
        out_specs.append(pl.BlockSpec((8, tm), lambda i: (0, i)))
    res = pl.pallas_call(
        kern, out_shape=out_shape, grid=(m // tm,), in_specs=in_specs, out_specs=out_specs,
        compiler_params=_cparams(("arbitrary",)), name="norm_mod_router" if with_router else "norm_mod",
    )(*args)
    return res if with_router else res[0]


IN_TN = 512
J_QK = 2 * H_NA * DH_NA // IN_TN
J_V = H_NA * DH_NA // IN_TN
N_MLSTM = N_MAIN - OFF_MQ


def _in_proj_kernel(a_ref, w_ref, wgate_ref, gb_ref, qg_ref, kg_ref,
                    qk_ref, v_ref, pm_ref, gates_ref, wbf_ref):
    j = pl.program_id(1)

    @pl.when(j == 0)
    def _():
        gates_ref[...] = (jnp.dot(a_ref[...], wgate_ref[0].astype(BF16), preferred_element_type=F32)
                          + gb_ref[...])

    wbf_ref[...] = w_ref[0].astype(BF16)
    acc = jnp.dot(a_ref[...], wbf_ref[...], preferred_element_type=F32)

    @pl.when(j < J_QK)
    def _():
        g = jnp.where(j < J_QK // 2, qg_ref[...] * (DH_NA ** -0.5), kg_ref[...])
        for h in range(IN_TN // DH_NA):
            sl = slice(h * DH_NA, (h + 1) * DH_NA)
            qk_ref[:, sl] = _rms(acc[:, sl], g).astype(BF16)

    @pl.when((j >= J_QK) & (j < J_QK + J_V))
    def _():
        v_ref[...] = acc.astype(BF16)

    @pl.when(j >= J_QK + J_V)
    def _():
        pm_ref[...] = acc


def in_proj(hbf, w_in, w_gate_cols, gate_b_pad, layer, qg, kg):
    m, d = hbf.shape
    tm = _pick_tile(m, (2176, 1088, 640, 512, 256))
    nj = N_MAIN // IN_TN
    return pl.pallas_call(
        _in_proj_kernel,
        out_shape=[jax.ShapeDtypeStruct((m, 2 * H_NA * DH_NA), BF16),
                   jax.ShapeDtypeStruct((m, H_NA * DH_NA), BF16),
                   jax.ShapeDtypeStruct((m, N_MLSTM), F32),
                   jax.ShapeDtypeStruct((m, LANES), F32)],
        grid=(m // tm, nj),
        in_specs=[pl.BlockSpec((tm, d), lambda i, j: (i, 0)),
                  pl.BlockSpec((1, d, IN_TN), lambda i, j: (layer, 0, j)),
                  pl.BlockSpec((1, d, LANES), lambda i, j: (layer, 0, 0)),
                  pl.BlockSpec((None, 1, LANES), lambda i, j: (layer, 0, 0)),
                  pl.BlockSpec((1, DH_NA), lambda i, j: (0, 0)),
                  pl.BlockSpec((1, DH_NA), lambda i, j: (0, 0))],
        out_specs=[pl.BlockSpec((tm, IN_TN), lambda i, j: (i, jnp.minimum(j, J_QK - 1))),
                   pl.BlockSpec((tm, IN_TN), lambda i, j: (i, jnp.clip(j - J_QK, 0, J_V - 1))),
                   pl.BlockSpec((tm, IN_TN), lambda i, j: (i, jnp.maximum(j - (J_QK + J_V), 0))),
                   pl.BlockSpec((tm, LANES), lambda i, j: (i, 0))],
        scratch_shapes=[pltpu.VMEM((d, IN_TN), BF16)],
        compiler_params=_cparams(("arbitrary", "arbitrary")), name="in_proj",
    )(hbf, w_in, w_gate_cols, gate_b_pad.reshape(-1, 1, LANES), qg.reshape(1, DH_NA), kg.reshape(1, DH_NA))


NT_DIMS = (((1,), (1,)), ((), ()))
TN_DIMS = (((0,), (0,)), ((), ()))
NA_QROWS = 4
NA_HEADS = 4
NA_KROWS = NA_QROWS + WIN_R
N_BIAS_PATTERNS = 3


def _na_block_start(qb, rows):
    return jnp.clip(qb * NA_QROWS - WIN_R // 2, 0, rows - NA_KROWS)


def _na_kernel(q_ref, k_ref, v_ref, kc_ref, vc_ref, bias_ref, o_ref, *, rows):
    qb = pl.program_id(2)
    nblk = rows // NA_QROWS
    pid = jnp.where(qb == 0, 0, jnp.where(qb == nblk - 1, 2, 1))
    start = pl.multiple_of(_na_block_start(qb, rows) * GRID_W, GRID_W)
    for hh in range(NA_HEADS):
        sl = slice(hh * DH_NA, (hh + 1) * DH_NA)
        q = q_ref[:, sl]
        kb = k_ref[pl.ds(start, NA_KROWS * GRID_W), sl]
        vb = v_ref[pl.ds(start, NA_KROWS * GRID_W), sl]
        s = lax.dot_general(q, kb, NT_DIMS, preferred_element_type=F32) + bias_ref[hh, pid]
        sc = lax.dot_general(q, kc_ref[:, sl], NT_DIMS, preferred_element_type=F32)
        m = jnp.maximum(jnp.max(s, axis=-1, keepdims=True), jnp.max(sc, axis=-1, keepdims=True))
        pb = jnp.exp(s - m)
        pc = jnp.exp(sc - m)
        den = jnp.sum(pb, axis=-1, keepdims=True) + jnp.sum(pc, axis=-1, keepdims=True)
        o = (jnp.dot(pb.astype(BF16), vb, preferred_element_type=F32)
             + jnp.dot(pc.astype(BF16), vc_ref[:, sl], preferred_element_type=F32))
        o_ref[:, sl] = (o / den).astype(o_ref.dtype)


def na_bias_table(rpb, rows):
    nblk = rows // NA_QROWS
    qb = np.array([0, 2, nblk - 1])
    ks = np.clip(qb * NA_QROWS - WIN_R // 2, 0, rows - NA_KROWS)
    r = qb[:, None] * NA_QROWS + np.arange(NA_QROWS)[None, :]
    rs = np.clip(r - WIN_R // 2, 0, rows - WIN_R)
    kr = ks[:, None] + np.arange(NA_KROWS)[None, :]
    in_band = (kr[:, None, :] >= rs[:, :, None]) & (kr[:, None, :] < rs[:, :, None] + WIN_R)
    dr = kr[:, None, :] - r[:, :, None] + (WIN_R - 1)
    c = np.arange(GRID_W)
    dc = np.clip(c[None, :] - c[:, None], -(WIN_C - 1), WIN_C - 1) + (WIN_C - 1)
    cs = np.clip(c - WIN_C // 2, 0, GRID_W - WIN_C)
    col_ok = (c[None, :] >= cs[:, None]) & (c[None, :] < cs[:, None] + WIN_C)
    sel_r = jnp.asarray((dr[..., None] == np.arange(2 * WIN_R - 1)) & in_band[..., None], F32)
    sel_c = jnp.asarray(dc[..., None] == np.arange(2 * WIN_C - 1), F32)
    hi = lax.Precision.HIGHEST
    t = jnp.einsum("lhab,pxja->lhpxjb", rpb.astype(F32), sel_r, precision=hi)
    bias = jnp.einsum("lhpxjb,qkb->lhpxqjk", t, sel_c, precision=hi)
    ok = in_band[:, :, None, :, None] & col_ok[None, None, :, None, :]
    bias = jnp.where(jnp.asarray(ok)[None, None], bias, NEG)
    return bias.reshape(rpb.shape[0], H_NA, N_BIAS_PATTERNS, NA_QROWS * GRID_W, NA_KROWS * GRID_W)


def na_attention(qk, vbf, bias, layer, dims):
    rows = dims.seq // GRID_W
    qblocks = rows // NA_QROWS
    tq = NA_QROWS * GRID_W
    ctx_blk0 = dims.m_x // dims.ctx
    hgroups = H_NA // NA_HEADS
    hw = NA_HEADS * DH_NA
    kern = functools.partial(_na_kernel, rows=rows)
    return pl.pallas_call(
        kern,
        out_shape=jax.ShapeDtypeStruct((dims.m_x, H_NA * DH_NA), BF16),
        grid=(dims.batch, hgroups, qblocks),
        in_specs=[
            pl.BlockSpec((tq, hw), lambda b, h, i: (b * qblocks + i, h)),
            pl.BlockSpec((dims.seq, hw), lambda b, h, i: (b, hgroups + h)),
            pl.BlockSpec((dims.seq, hw), lambda b, h, i: (b, h)),
            pl.BlockSpec((dims.ctx, hw), lambda b, h, i: (ctx_blk0 + b, hgroups + h)),
            pl.BlockSpec((dims.ctx, hw), lambda b, h, i: (ctx_blk0 + b, h)),
            pl.BlockSpec((None, NA_HEADS, N_BIAS_PATTERNS, tq, NA_KROWS * GRID_W),
                         lambda b, h, i: (layer, h, 0, 0, 0)),
        ],
        out_specs=pl.BlockSpec((tq, hw), lambda b, h, i: (b * qblocks + i, h)),
        compiler_params=_cparams(("arbitrary", "arbitrary", "arbitrary")), name="na_attention",
    )(qk, qk, vbf, qk, vbf, bias)


def _ctx_attn_kernel(q_ref, k_ref, v_ref, o_ref):
    s = lax.dot_general(q_ref[...], k_ref[...], NT_DIMS, preferred_element_type=F32)
    m = jnp.max(s, axis=-1, keepdims=True)
    p = jnp.exp(s - m)
    den = jnp.sum(p, axis=-1, keepdims=True)
    o = jnp.dot(p.astype(BF16), v_ref[...], preferred_element_type=F32)
    o_ref[...] = (o / den).astype(o_ref.dtype)


def ctx_attention(qk, vbf, dims):
    ctx_blk0 = dims.m_x // dims.ctx
    return pl.pallas_call(
        _ctx_attn_kernel,
        out_shape=jax.ShapeDtypeStruct((dims.batch * dims.ctx, H_NA * DH_NA), BF16),
        grid=(dims.batch, H_NA),
        in_specs=[
            pl.BlockSpec((dims.ctx, DH_NA), lambda b, h: (ctx_blk0 + b, h)),
            pl.BlockSpec((dims.ctx, DH_NA), lambda b, h: (ctx_blk0 + b, H_NA + h)),
            pl.BlockSpec((dims.ctx, DH_NA), lambda b, h: (ctx_blk0 + b, h)),
        ],
        out_specs=pl.BlockSpec((dims.ctx, DH_NA), lambda b, h: (b, h)),
        compiler_params=_cparams(("arbitrary", "arbitrary")), name="ctx_attention",
    )(qk, qk, vbf)


def _log_sigmoid(x):
    return jnp.minimum(x, 0.0) - jnp.log(1.0 + jnp.exp(-jnp.abs(x)))


def _rope(t, cos, sin_signed, even):
    swapped = jnp.where(even, pltpu.roll(t, DK_M - 1, 1), pltpu.roll(t, 1, 1))
    return t * cos + swapped * sin_signed


def _mlstm_kernel(q_ref, k_ref, v_ref, g_ref, gt_ref, cos_ref, sin_ref, o_ref, c_ref, m_ref):
    direction = pl.program_id(1)
    t_step = pl.program_id(2)

    @pl.when(t_step == 0)
    def _():
        c_ref[...] = jnp.zeros(c_ref.shape, F32)
        m_ref[...] = jnp.zeros(m_ref.shape, F32)

    ti = lax.broadcasted_iota(I32, (CHUNK, CHUNK), 0)
    si = lax.broadcasted_iota(I32, (CHUNK, CHUNK), 1)
    mask = (si - ti) * (1 - 2 * direction) <= 0
    tri = mask.astype(BF16)
    g = g_ref[0]
    gt = gt_ref[0]
    lf_hi, lf_lo = _split_bf16(_log_sigmoid(g))
    cum_col = (jnp.dot(tri, lf_hi, preferred_element_type=F32)
               + jnp.dot(tri, lf_lo, preferred_element_type=F32))
    lft = _log_sigmoid(gt)
    lft_hi, lft_lo = _split_bf16(lft)
    cum_row = (lax.dot_general(lft_hi, tri, NT_DIMS, preferred_element_type=F32)
               + lax.dot_general(lft_lo, tri, NT_DIMS, preferred_element_type=F32))
    cos = cos_ref[...]
    sin = sin_ref[...]
    even = (lax.broadcasted_iota(I32, (CHUNK, DK_M), 1) % 2) == 0
    ones_col = (lax.broadcasted_iota(I32, (CHUNK, DV_EXT - DV_M), 1) == 0).astype(BF16)
    for h in range(H_M):
        q = _rope(q_ref[:, h * DK_M:(h + 1) * DK_M], cos, sin, even)
        k = _rope(k_ref[:, h * DK_M:(h + 1) * DK_M], cos, sin, even) * (DK_M ** -0.5)
        v_ext = jnp.concatenate([v_ref[:, h * DV_M:(h + 1) * DV_M].astype(BF16), ones_col], axis=1)
        qb = q.astype(BF16)
        b_col = cum_col[:, H_M + h:H_M + h + 1]
        i_col = g[:, h:h + 1]
        b_row = cum_row[H_M + h:H_M + h + 1, :]
        i_row = gt[h:h + 1, :]
        b_last = jnp.sum(lft[H_M + h:H_M + h + 1, :], axis=1, keepdims=True)
        m_old = m_ref[h][0:1, 0:1]
        c_old = c_ref[h]
        d_end_row = b_last - b_row + i_row
        m_new = jnp.maximum(b_last + m_old, jnp.max(d_end_row, axis=1, keepdims=True))
        w_end = jnp.exp(b_last - b_col + i_col - m_new)
        decay = jnp.exp(b_last + m_old - m_new)
        kw = (k * w_end).astype(BF16)
        c_ref[h] = decay * c_old + lax.dot_general(kw, v_ext, TN_DIMS, preferred_element_type=F32)
        m_ref[h] = jnp.broadcast_to(m_new, m_ref.shape[1:])
        d_mat = jnp.where(mask, b_col - b_row + i_row, NEG)
        m_in = b_col + m_old
        m_t = jnp.maximum(m_in, jnp.max(d_mat, axis=1, keepdims=True))
        s = lax.dot_general(qb, k.astype(BF16), NT_DIMS, preferred_element_type=F32) * jnp.exp(d_mat - m_t)
        a_in = jnp.exp(m_in - m_t)
        numden = (jnp.dot(s.astype(BF16), v_ext, preferred_element_type=F32)
                  + a_in * jnp.dot(qb, c_old.astype(BF16), preferred_element_type=F32))
        den = numden[:, DV_M:DV_M + 1]
        o_ref[0, :, h * DV_M:(h + 1) * DV_M] = numden[:, :DV_M] / jnp.maximum(jnp.abs(den), jnp.exp(-m_t))


def mlstm_scan(p, g_dir, gt_dir, cos_t, sin_t, dims):
    ncx = dims.ctx // CHUNK
    nlx = dims.seq // CHUNK
    x_blocks = dims.m_x // CHUNK

    def row_block(b, d, t):
        in_ctx = t < ncx
        cc = jnp.where(d == 0, t, ncx - 1 - t)
        cx = jnp.where(d == 0, t - ncx, nlx - 1 - (t - ncx))
        return jnp.where(in_ctx, x_blocks + b * ncx + cc, b * nlx + cx)

    def rope_block(b, d, t):
        in_ctx = t < ncx
        cx = jnp.where(d == 0, t - ncx, nlx - 1 - (t - ncx))
        return jnp.where(in_ctx, nlx, cx)

    return pl.pallas_call(
        _mlstm_kernel,
        out_shape=jax.ShapeDtypeStruct((2, dims.m_all, H_M * DV_M), F32),
        grid=(dims.batch, 2, ncx + nlx),
        in_specs=[
            pl.BlockSpec((CHUNK, H_M * DK_M), lambda b, d, t: (row_block(b, d, t), PM_Q // (H_M * DK_M))),
            pl.BlockSpec((CHUNK, H_M * DK_M), lambda b, d, t: (row_block(b, d, t), PM_K // (H_M * DK_M))),
            pl.BlockSpec((CHUNK, H_M * DV_M), lambda b, d, t: (row_block(b, d, t), PM_V // (H_M * DV_M))),
            pl.BlockSpec((1, CHUNK, LANES), lambda b, d, t: (d, row_block(b, d, t), 0)),
            pl.BlockSpec((1, 8, CHUNK), lambda b, d, t: (d, 0, row_block(b, d, t))),
            pl.BlockSpec((CHUNK, DK_M), lambda b, d, t: (rope_block(b, d, t), 0)),
            pl.BlockSpec((CHUNK, DK_M), lambda b, d, t: (rope_block(b, d, t), 0)),
        ],
        out_specs=pl.BlockSpec((1, CHUNK, H_M * DV_M), lambda b, d, t: (d, row_block(b, d, t), 0)),
        scratch_shapes=[pltpu.VMEM((H_M, DK_M, DV_EXT), F32), pltpu.VMEM((H_M, 8, LANES), F32)],
        compiler_params=_cparams(("arbitrary", "arbitrary", "arbitrary")), name="mlstm_scan",
    )(p, p, p, g_dir, gt_dir, cos_t, sin_t)


def _mlstm_out_kernel(h_ref, o_ref, g_ref, out_ref):
    for h in range(H_M):
        sl = slice(h * DV_M, (h + 1) * DV_M)
        y = _rms(h_ref[0, :, sl] + h_ref[1, :, sl], g_ref[:, sl])
        out_ref[:, sl] = (y * jax.nn.sigmoid(o_ref[:, sl])).astype(out_ref.dtype)


def mlstm_out(hdir, p, g, dims):
    m = dims.m_all
    tm = 256
    n = H_M * DV_M
    return pl.pallas_call(
        _mlstm_out_kernel,
        out_shape=jax.ShapeDtypeStruct((m, n), BF16),
        grid=(m // tm,),
        in_specs=[pl.BlockSpec((2, tm, n), lambda i: (0, i, 0)),
                  pl.BlockSpec((tm, n), lambda i: (i, PM_O // n)),
                  pl.BlockSpec((1, n), lambda i: (0, 0))],
        out_specs=pl.BlockSpec((tm, n), lambda i: (i, 0)),
        compiler_params=_cparams(("arbitrary",)), name="mlstm_out",
    )(hdir, p, g.reshape(1, n))


def rope_tables(dims):
    t = jnp.arange(dims.seq)
    row = (t // GRID_W).astype(F32)
    col = (t % GRID_W).astype(F32)
    quarter = DK_M // 4
    inv = ROPE_BASE ** (-jnp.arange(quarter, dtype=F32) / quarter)
    ang = jnp.concatenate([row[:, None] * inv, col[:, None] * inv], axis=-1)
    cos = jnp.repeat(jnp.cos(ang), 2, axis=-1)
    sin = jnp.repeat(jnp.sin(ang), 2, axis=-1) * jnp.tile(jnp.array([-1.0, 1.0], F32), DK_M // 2)
    cos = jnp.concatenate([cos, jnp.ones((CHUNK, DK_M), F32)], axis=0)
    sin = jnp.concatenate([sin, jnp.zeros((CHUNK, DK_M), F32)], axis=0)
    return cos, sin


def _gather_kernel(src_ref, nrows_ref, h_hbm, o_ref, buf_ref, sem, *, tm):
    i = pl.program_id(0)
    n = nrows_ref[i]

    @pl.when(n < tm)
    def _():
        buf_ref[...] = jnp.zeros(buf_ref.shape, buf_ref.dtype)

    def start(r, carry):
        pltpu.make_async_copy(h_hbm.at[pl.ds(src_ref[i * tm + r], 1)], buf_ref.at[pl.ds(r, 1)], sem).start()
        return carry

    lax.fori_loop(0, n, start, 0)

    def wait(r, carry):
        pltpu.make_async_copy(h_hbm.at[pl.ds(0, 1)], buf_ref.at[pl.ds(r, 1)], sem).wait()
        return carry

    lax.fori_loop(0, n, wait, 0)
    o_ref[...] = buf_ref[...].astype(o_ref.dtype)


def gather_rows(h, src, tile_rows, *, tm):
    m, d = h.shape
    r_total = src.shape[0]
    return pl.pallas_call(
        functools.partial(_gather_kernel, tm=tm),
        out_shape=jax.ShapeDtypeStruct((r_total, d), BF16),
        grid_spec=pltpu.PrefetchScalarGridSpec(
            num_scalar_prefetch=2,
            grid=(r_total // tm,),
            in_specs=[pl.BlockSpec(memory_space=pl.ANY)],
            out_specs=pl.BlockSpec((tm, d), lambda i, *s: (i, 0)),
            scratch_shapes=[pltpu.VMEM((tm, d), F32), pltpu.SemaphoreType.DMA],
        ),
        compiler_params=_cparams(("arbitrary",)), name="moe_gather",
    )(src, tile_rows, h)


def _combine_kernel(p1_ref, p2_ref, y_hbm, x_ref, gate_ref, rt_ref, o_ref, y1_ref, y2_ref, sem,
                    *, tm, seq, n_groups):
    i = pl.program_id(0)

    def start(r, carry):
        t = i * tm + r
        pltpu.make_async_copy(y_hbm.at[pl.ds(p1_ref[t], 1)], y1_ref.at[pl.ds(r, 1)], sem.at[0]).start()
        pltpu.make_async_copy(y_hbm.at[pl.ds(p2_ref[t], 1)], y2_ref.at[pl.ds(r, 1)], sem.at[1]).start()
        return carry

    lax.fori_loop(0, tm, start, 0)

    def wait(r, carry):
        pltpu.make_async_copy(y_hbm.at[pl.ds(0, 1)], y1_ref.at[pl.ds(r, 1)], sem.at[0]).wait()
        pltpu.make_async_copy(y_hbm.at[pl.ds(0, 1)], y2_ref.at[pl.ds(r, 1)], sem.at[1]).wait()
        return carry

    lax.fori_loop(0, tm, wait, 0)
    row0 = i * tm
    grp = jnp.int32(n_groups - 1)
    for b in reversed(range(n_groups - 1)):
        grp = jnp.where(row0 < (b + 1) * seq, b, grp)
    gate = gate_ref[pl.ds(grp, 1), :]
    w1 = rt_ref[:, 2:3]
    w2 = rt_ref[:, 3:4]
    o_ref[...] = x_ref[...] + gate * (w1 * y1_ref[...] + w2 * y2_ref[...])


def moe_combine(xs, y, pos1, pos2, route_t, mod, layer, gate_chunk, dims, *, m):
    d = dims.d_model
    tm = _pick_tile(math.gcd(dims.seq, dims.ctx), (256, 128))
    kern = functools.partial(_combine_kernel, tm=tm, seq=dims.seq, n_groups=dims.batch + 1)
    return pl.pallas_call(
        kern,
        out_shape=jax.ShapeDtypeStruct((m, d), F32),
        grid_spec=pltpu.PrefetchScalarGridSpec(
            num_scalar_prefetch=2,
            grid=(m // tm,),
            in_specs=[pl.BlockSpec(memory_space=pl.ANY),
                      pl.BlockSpec((tm, d), lambda i, a, b: (i, 0)),
                      pl.BlockSpec((8, d), lambda i, a, b: (MOD_ROWS // 8 * layer, gate_chunk)),
                      pl.BlockSpec((tm, 8), lambda i, a, b: (i, 0))],
            out_specs=pl.BlockSpec((tm, d), lambda i, a, b: (i, 0)),
            scratch_shapes=[pltpu.VMEM((tm, d), F32), pltpu.VMEM((tm, d), F32),
                            pltpu.SemaphoreType.DMA((2,))],
        ),
        compiler_params=_cparams(("arbitrary",)), name="moe_combine",
    )(pos1, pos2, y, xs, mod, route_t)


def moe_dispatch_plan(route, m, tm):
    e = jnp.concatenate([route[0, :m], route[1, :m]]).astype(I32)
    tok = jnp.tile(jnp.arange(m, dtype=I32), 2)
    onehot = (e[:, None] == jnp.arange(N_EXPERTS, dtype=I32)[None, :]).astype(I32)
    rank = jnp.sum((jnp.cumsum(onehot, axis=0) - onehot) * onehot, axis=1)
    counts = jnp.sum(onehot, axis=0)
    padded = ((counts + tm - 1) // tm) * tm
    ends = jnp.cumsum(padded)
    offs = ends - padded
    pos = offs[e] + rank
    r_total = TOP_K * m + N_EXPERTS * tm
    src = jnp.zeros((r_total,), I32).at[pos].set(tok)
    tile_start = jnp.arange(r_total // tm, dtype=I32) * tm
    last_e = jnp.max(jnp.where(counts > 0, jnp.arange(N_EXPERTS, dtype=I32), 0))
    tile_e = jnp.minimum(jnp.sum((tile_start[:, None] >= ends[None, :]).astype(I32), axis=1), last_e)
    tile_rows = jnp.clip((offs + counts)[tile_e] - tile_start, 0, tm).astype(I32)
    return src, pos[:m], pos[m:], tile_e, tile_rows


def _dense_tile(m, candidates):
    return _pick_tile(m, candidates)


def kernel(x, c, ctx, c_ctx, ada_w, ada_b, norm1_g, norm2_g, w_in, gate_b, na_q_g, na_k_g, na_rpb, m_norm_g, w_out, ffn_w_gate, ffn_w_up, ffn_w_down, moe_router, moe_w_gate, moe_w_up, moe_w_down):
    batch, seq, d = x.shape
    dims = Dims(batch=batch, seq=seq, ctx=ctx.shape[1], d_model=d, d_ff=ffn_w_gate.shape[2], depth=ada_w.shape[0])
    depth = dims.depth
    n_groups = batch + 1
    rows = seq // GRID_W
    assert rows >= 4 * NA_QROWS and rows % NA_QROWS == 0 and batch + 1 <= 8
    assert seq % 256 == 0 and dims.ctx % CHUNK == 0 and dims.m_x % dims.ctx == 0

    xs = jnp.concatenate([x.reshape(dims.m_x, d), ctx.reshape(batch * dims.ctx, d)], axis=0)

    cond = jnp.concatenate([jax.nn.silu(c), jax.nn.silu(c_ctx)[None], jnp.zeros((MOD_ROWS - n_groups, d), F32)], axis=0)
    tn_ada = _pick_tile(6 * d, (1024, 512, 256))
    tn_d = min(512, d)
    mod = gmm((cond.astype(BF16),), (ada_w,), jnp.arange(depth, dtype=I32), jnp.full((depth,), MOD_ROWS, I32),
              tm=MOD_ROWS, tn=tn_ada, n_out=6 * d, out_dtype=F32, epilogue="bias",
              extras=(ada_b.reshape(depth, 1, 6 * d),),
              extra_specs=(((1, 1, tn_ada), lambda j, i, g, *s: (g[i], 0, j)),),
              a_fixed=True, rows_outer=True, name="ada_mod")

    cos_t, sin_t = rope_tables(dims)
    na_bias = na_bias_table(na_rpb, rows)
    w_gate_cols = jnp.pad(w_in[:, :, N_MAIN:], ((0, 0), (0, 0), (0, LANES - 4 * H_M)))
    gate_b_pad = jnp.pad(gate_b, ((0, 0), (0, LANES - 4 * H_M)))

    def dense_groups(m, tm, g):
        nt = m // tm
        return jnp.full((nt,), g, I32), jnp.full((nt,), tm, I32)

    for layer in range(depth):
        last = layer == depth - 1
        m_out = dims.m_x if last else dims.m_all

        hbf = norm_mod(xs, norm1_g[layer], mod, layer, 0, 1, dims, m=dims.m_all)
        qk, vbf, pm, gates = in_proj(hbf, w_in, w_gate_cols, gate_b_pad, layer, na_q_g[layer], na_k_g[layer])
        g_dir = jnp.stack([gates, jnp.roll(gates, -2 * H_M, axis=1)])
        gt_dir = jnp.stack([gates[:, :2 * H_M].T, gates[:, 2 * H_M:4 * H_M].T])

        na_out = na_attention(qk, vbf, na_bias, layer, dims)
        hdir = mlstm_scan(pm, g_dir, gt_dir, cos_t, sin_t, dims)
        m_mix = mlstm_out(hdir, pm, m_norm_g[layer], dims)
        if not last:
            na_out = jnp.concatenate([na_out, ctx_attention(qk, vbf, dims)], axis=0)

        tm_o = _dense_tile(m_out, (1088, 1024, 640, 512, 256))
        grp, nv = dense_groups(m_out, tm_o, layer)
        res_specs = lambda tm, tn, chunk: (
            ((tm, tn), lambda j, i, *s: (i, j)),
            ((8, tn), lambda j, i, *s: (MOD_ROWS // 8 * layer, chunk * (d // tn) + j)))
        tn_o = min(1024, d)
        xs = gmm((na_out, m_mix), (w_out,), grp, nv, tm=tm_o, tn=tn_o, n_out=d, out_dtype=F32, epilogue="residual",
                 extras=(xs, mod), extra_specs=res_specs(tm_o, tn_o, 2), seq=seq, n_groups=n_groups,
                 name="w_out_proj")

        idx = layer // 2
        if layer % 2 == 0:
            h2 = norm_mod(xs, norm2_g[layer], mod, layer, 3, 4, dims, m=m_out)
            tm_u = _dense_tile(m_out, (1088, 1024, 640, 512, 256))
            grp, nv = dense_groups(m_out, tm_u, idx)
            act = gmm_swiglu(h2, ffn_w_gate, ffn_w_up, grp, nv, tm=tm_u, tn=512, name="ffn_up")
            tm_d = _dense_tile(m_out, (544, 512, 320, 256))
            grp, nv = dense_groups(m_out, tm_d, idx)
            xs = gmm((act,), (ffn_w_down,), grp, nv, tm=tm_d, tn=tn_d, n_out=d, out_dtype=F32, epilogue="residual",
                     extras=(xs, mod), extra_specs=res_specs(tm_d, tn_d, 5), seq=seq, n_groups=n_groups,
                     name="ffn_down")
        else:
            h2, route = norm_mod(xs, norm2_g[layer], mod, layer, 3, 4, dims, m=m_out,
                                 router_t=moe_router[idx].T)
            tm_e = 512 if m_out >= 4096 else 128
            src, pos1, pos2, tile_e, tile_rows = moe_dispatch_plan(route, m_out, tm_e)
            hs = gather_rows(h2, src, tile_rows, tm=tm_e)
            grp = tile_e + idx * N_EXPERTS
            nw = moe_w_gate.shape[0] * N_EXPERTS
            act = gmm_swiglu(hs, moe_w_gate.reshape(nw, d, dims.d_ff), moe_w_up.reshape(nw, d, dims.d_ff),
                             grp, tile_rows, tm=tm_e, tn=512, half_tiles=True, name="moe_up")
            y = gmm((act,), (moe_w_down.reshape(nw, dims.d_ff, d),), grp, tile_rows, tm=tm_e, tn=tn_d, n_out=d,
                    out_dtype=F32, half_tiles=True, name="moe_down")
            xs = moe_combine(xs, y, pos1, pos2, route.T, mod, layer, 5, dims, m=m_out)

    return xs[:dims.m_x].reshape(batch, seq, d)
```

```python
import functools
import math
from typing import NamedTuple

import jax
import jax.numpy as jnp
import numpy as np
from jax import lax
from jax.experimental import pallas as pl
from jax.experimental.pallas import tpu as pltpu

F32 = jnp.float32
BF16 = jnp.bfloat16
I32 = jnp.int32

GRID_W = 64
H_NA = 8
DH_NA = 128
WIN_R = 8
WIN_C = 16
H_M = 4
DK_M = 128
DV_M = 256
CHUNK = 128
ROPE_BASE = 10000.0
N_EXPERTS = 8
TOP_K = 2
EPS = 1e-6
D_MIX = 2048
NEG = -1e30

OFF_MQ = 3 * H_NA * DH_NA
N_MAIN = 6144
PM_Q, PM_K, PM_V, PM_O = 0, H_M * DK_M, 2 * H_M * DK_M, 2 * H_M * DK_M + H_M * DV_M
DV_EXT = DV_M + 128

LANES = 128
VMEM_LIMIT = 56 * 1024 * 1024
MOD_ROWS = 16


class Dims(NamedTuple):
    batch: int
    seq: int
    ctx: int
    d_model: int
    d_ff: int
    depth: int

    @property
    def m_x(self):
        return self.batch * self.seq

    @property
    def m_all(self):
        return self.batch * (self.seq + self.ctx)


def _cparams(sem):
    return pltpu.CompilerParams(dimension_semantics=sem, vmem_limit_bytes=VMEM_LIMIT)


def _group_select(row, vals, seq):
    out = vals[-1]
    for b in reversed(range(len(vals) - 1)):
        out = jnp.where(row < (b + 1) * seq, vals[b], out)
    return out


def _pick_tile(m, candidates):
    for t in candidates:
        if m % t == 0:
            return t
    raise ValueError(f"no row tile for {m}")


N_EXTRA = {"residual": 2, "bias": 1, "swiglu": 0, None: 0}


def _gmm_kernel(grp_ref, rows_ref, rid_ref, rgrp_ref, nruns_ref, *refs,
                n_a, n_w, epilogue, tm, tn, nj, seq, n_groups, rows_outer, half_tiles):
    a_refs, refs = refs[:n_a], refs[n_a:]
    w_refs, refs = refs[:n_w], refs[n_w:]
    x_refs, refs = refs[:N_EXTRA[epilogue]], refs[N_EXTRA[epilogue]:]
    o_ref, refs = refs[0], refs[1:]
    wbf_refs, refs = refs[:n_w], refs[n_w:]
    if rows_outer:
        i = pl.program_id(0)
        for w_ref, wbf_ref in zip(w_refs, wbf_refs):
            wbf_ref[...] = w_ref[0].astype(BF16)
    else:
        wst_refs, sem = refs[:n_w], refs[n_w]
        j = pl.program_id(0)
        i = pl.program_id(1)
        rid = rid_ref[i]
        nruns = nruns_ref[0]

        def tile_copy(widx, g, jj, slot):
            src = w_refs[widx].at[g, :, pl.ds(pl.multiple_of(jj * tn, tn), tn)]
            return pltpu.make_async_copy(src, wst_refs[widx].at[slot], sem.at[widx, slot])

        @pl.when((i == 0) | (rid != rid_ref[jnp.maximum(i - 1, 0)]))
        def _():
            q = j * nruns + rid
            slot = q % 2

            @pl.when(q == 0)
            def _():
                for widx in range(n_w):
                    tile_copy(widx, rgrp_ref[0], 0, 0).start()

            for widx in range(n_w):
                tile_copy(widx, 0, 0, slot).wait()
            wrap = rid + 1 == nruns
            next_run = jnp.where(wrap, 0, rid + 1)
            next_j = j + wrap.astype(I32)

            @pl.when(next_j < nj)
            def _():
                for widx in range(n_w):
                    tile_copy(widx, rgrp_ref[next_run], next_j, 1 - slot).start()

            for widx in range(n_w):
                wbf_refs[widx][...] = wst_refs[widx][slot].astype(BF16)

    n_rows = rows_ref[i]

    def compute(nr):
        def matmul(wbf_ref):
            out, k0 = None, 0
            for a_ref in a_refs:
                kp = a_ref.shape[1]
                part = jnp.dot(a_ref[0:nr, :], wbf_ref[k0:k0 + kp, :], preferred_element_type=F32)
                out = part if out is None else out + part
                k0 += kp
            return out

        acc = matmul(wbf_refs[0])
        if epilogue == "residual":
            res_ref, gate_ref = x_refs
            row = i * tm + lax.broadcasted_iota(I32, (nr, 1), 0)
            gate = _group_select(row, [gate_ref[k:k + 1, :] for k in range(n_groups)], seq)
            acc = res_ref[0:nr, :] + gate * acc
        elif epilogue == "bias":
            acc = acc + x_refs[0][0]
        elif epilogue == "swiglu":
            acc = acc * jax.nn.sigmoid(acc) * matmul(wbf_refs[1])
        o_ref[0:nr, :] = acc.astype(o_ref.dtype)
        if nr < tm:
            o_ref[nr:, :] = jnp.zeros((tm - nr, o_ref.shape[1]), o_ref.dtype)

    half = tm // 2 if half_tiles else 0
    pl.when(n_rows > half)(lambda: compute(tm))
    if half_tiles:
        pl.when((n_rows > 0) & (n_rows <= half))(lambda: compute(half))

    @pl.when(n_rows == 0)
    def _():
        o_ref[...] = jnp.zeros(o_ref.shape, o_ref.dtype)


def _runs(grp):
    nt = grp.shape[0]
    first = jnp.concatenate([jnp.ones((1,), I32), (grp[1:] != grp[:-1]).astype(I32)])
    run_id = jnp.cumsum(first) - 1
    sel = (run_id[:, None] == jnp.arange(nt, dtype=I32)[None, :]).astype(I32) * first[:, None]
    run_grp = jnp.sum(sel * grp[:, None], axis=0)
    return run_id.astype(I32), run_grp.astype(I32), (run_id[-1:] + 1).astype(I32)


def gmm(a_parts, ws, grp, tile_rows, *, tm, tn, n_out, out_dtype, epilogue=None, extras=(), extra_specs=(),
        a_fixed=False, seq=0, n_groups=0, rows_outer=False, half_tiles=False, name="gmm"):
    k = sum(a.shape[1] for a in a_parts)
    nt = grp.shape[0]
    n_w = len(ws)
    nj = n_out // tn

    def spec(shape, fn):
        return pl.BlockSpec(shape, (lambda i, j, *s: fn(j, i, *s)) if rows_outer else fn)

    a_map = (lambda j, i, *s: (0, 0)) if a_fixed else (lambda j, i, *s: (i, 0))
    if rows_outer:
        w_specs = [spec((1, k, tn), lambda j, i, g, *s: (g[i], 0, j))] * n_w
        stream_scratch = []
    else:
        w_specs = [pl.BlockSpec(memory_space=pl.ANY)] * n_w
        stream_scratch = [pltpu.VMEM((2, k, tn), F32)] * n_w + [pltpu.SemaphoreType.DMA((n_w, 2))]
    in_specs = ([spec((tm, a.shape[1]), a_map) for a in a_parts] + w_specs
                + [spec(shape, fn) for shape, fn in extra_specs])
    kern = functools.partial(_gmm_kernel, n_a=len(a_parts), n_w=n_w, epilogue=epilogue, tm=tm, tn=tn, nj=nj,
                             seq=seq, n_groups=n_groups, rows_outer=rows_outer, half_tiles=half_tiles)
    run_id, run_grp, nruns = _runs(grp)
    return pl.pallas_call(
        kern,
        out_shape=jax.ShapeDtypeStruct((nt * tm, n_out), out_dtype),
        grid_spec=pltpu.PrefetchScalarGridSpec(
            num_scalar_prefetch=5,
            grid=(nt, nj) if rows_outer else (nj, nt),
            in_specs=in_specs,
            out_specs=spec((tm, tn), lambda j, i, *s: (i, j)),
            scratch_shapes=[pltpu.VMEM((k, tn), BF16)] * n_w + stream_scratch,
        ),
        compiler_params=_cparams(("arbitrary", "arbitrary")),
        name=name,
    )(grp, tile_rows, run_id, run_grp, nruns, *a_parts, *ws, *extras)


def gmm_swiglu(a, wg, wu, grp, tile_rows, *, tm, tn, half_tiles=False, name="gmm_swiglu"):
    return gmm((a,), (wg, wu), grp, tile_rows, tm=tm, tn=tn, n_out=wg.shape[2], out_dtype=BF16,
               epilogue="swiglu", half_tiles=half_tiles, name=name)


def _rms(x, g):
    return x * lax.rsqrt(jnp.mean(x * x, axis=-1, keepdims=True) + EPS) * g


def _split_bf16(x):
    hi = x.astype(BF16)
    lo = (x - hi.astype(F32)).astype(BF16)
    return hi, lo


def _norm_mod_kernel(x_ref, g_ref, shift_ref, scale_ref, *refs, tm, seq, n_groups, with_router):
    if with_router:
        r_ref, o_ref, route_ref = refs
    else:
        (o_ref,) = refs
    i = pl.program_id(0)
    row0 = i * tm
    grp = jnp.int32(n_groups - 1)
    for b in reversed(range(n_groups - 1)):
        grp = jnp.where(row0 < (b + 1) * seq, b, grp)
    y = _rms(x_ref[...], g_ref[...])
    h = y * (1.0 + scale_ref[pl.ds(grp, 1), :]) + shift_ref[pl.ds(grp, 1), :]
    o_ref[...] = h.astype(o_ref.dtype)
    if with_router:
        nt_dims = (((1,), (1,)), ((), ()))
        h_hi, h_lo = _split_bf16(h)
        r_hi, r_lo = _split_bf16(r_ref[...])
        logits = (lax.dot_general(r_hi, h_hi, nt_dims, preferred_element_type=F32)
                  + lax.dot_general(r_hi, h_lo, nt_dims, preferred_element_type=F32)
                  + lax.dot_general(r_lo, h_hi, nt_dims, preferred_element_type=F32))
        ids = lax.broadcasted_iota(I32, logits.shape, 0)
        m1 = jnp.max(logits, axis=0, keepdims=True)
        i1 = jnp.min(jnp.where(logits == m1, ids, N_EXPERTS), axis=0, keepdims=True)
        rest = jnp.where(ids == i1, -jnp.inf, logits)
        m2 = jnp.max(rest, axis=0, keepdims=True)
        i2 = jnp.min(jnp.where(rest == m2, ids, N_EXPERTS), axis=0, keepdims=True)
        e = jnp.exp(m2 - m1)
        w1 = 1.0 / (1.0 + e)
        w2 = e / (1.0 + e)
        out_row = lax.broadcasted_iota(I32, (8, tm), 0)
        route_ref[...] = jnp.where(out_row == 0, i1.astype(F32), jnp.where(
            out_row == 1, i2.astype(F32), jnp.where(out_row == 2, w1, jnp.where(out_row == 3, w2, 0.0))))


def norm_mod(xs, g, mod, layer, shift_chunk, scale_chunk, dims, *, m, router_t=None):
    d = dims.d_model
    tm = _pick_tile(math.gcd(dims.seq, dims.ctx), (256, 128))
    with_router = router_t is not None
    kern = functools.partial(_norm_mod_kernel, tm=tm, seq=dims.seq, n_groups=dims.batch + 1,
                             with_router=with_router)
    in_specs = [
        pl.BlockSpec((tm, d), lambda i: (i, 0)),
        pl.BlockSpec((1, d), lambda i: (0, 0)),
        pl.BlockSpec((8, d), lambda i: (MOD_ROWS // 8 * layer, shift_chunk)),
        pl.BlockSpec((8, d), lambda i: (MOD_ROWS // 8 * layer, scale_chunk)),
    ]
    args = [xs, g.reshape(1, d), mod, mod]
    out_shape = [jax.ShapeDtypeStruct((m, d), F32 if with_router else BF16)]
    out_specs = [pl.BlockSpec((tm, d), lambda i: (i, 0))]
    if with_router:
        in_specs.append(pl.BlockSpec((N_EXPERTS, d), lambda i: (0, 0)))
        args.append(router_t)
        out_shape.append(jax.ShapeDtypeStruct((8, m), F32))
        out_specs.append(pl.BlockSpec((8, tm), lambda i: (0, i)))
    res = pl.pallas_call(
        kern, out_shape=out_shape, grid=(m // tm,), in_specs=in_specs, out_specs=out_specs,
        compiler_params=_cparams(("arbitrary",)), name="norm_mod_router" if with_router else "norm_mod",
    )(*args)
    return res if with_router else res[0]


IN_TN = 512
J_QK = 2 * H_NA * DH_NA // IN_TN
J_V = H_NA * DH_NA // IN_TN
N_MLSTM = N_MAIN - OFF_MQ


def _in_proj_kernel(a_ref, w_ref, wgate_ref, gb_ref, qg_ref, kg_ref,
                    qk_ref, v_ref, pm_ref, gates_ref, wbf_ref):
    j = pl.program_id(1)

    @pl.when(j == 0)
    def _():
        gates_ref[...] = (jnp.dot(a_ref[...], wgate_ref[0].astype(BF16), preferred_element_type=F32)
                          + gb_ref[...])

    wbf_ref[...] = w_ref[0].astype(BF16)
    acc = jnp.dot(a_ref[...], wbf_ref[...], preferred_element_type=F32)

    @pl.when(j < J_QK)
    def _():
        g = jnp.where(j < J_QK // 2, qg_ref[...] * (DH_NA ** -0.5), kg_ref[...])
        for h in range(IN_TN // DH_NA):
            sl = slice(h * DH_NA, (h + 1) * DH_NA)
            qk_ref[:, sl] = _rms(acc[:, sl], g).astype(BF16)

    @pl.when((j >= J_QK) & (j < J_QK + J_V))
    def _():
        v_ref[...] = acc.astype(BF16)

    @pl.when(j >= J_QK + J_V)
    def _():
        pm_ref[...] = acc


def in_proj(hbf, w_in, w_gate_cols, gate_b_pad, layer, qg, kg):
    m, d = hbf.shape
    tm = _pick_tile(m, (2176, 1088, 640, 512, 256))
    nj = N_MAIN // IN_TN
    return pl.pallas_call(
        _in_proj_kernel,
        out_shape=[jax.ShapeDtypeStruct((m, 2 * H_NA * DH_NA), BF16),
                   jax.ShapeDtypeStruct((m, H_NA * DH_NA), BF16),
                   jax.ShapeDtypeStruct((m, N_MLSTM), F32),
                   jax.ShapeDtypeStruct((m, LANES), F32)],
        grid=(m // tm, nj),
        in_specs=[pl.BlockSpec((tm, d), lambda i, j: (i, 0)),
                  pl.BlockSpec((1, d, IN_TN), lambda i, j: (layer, 0, j)),
                  pl.BlockSpec((1, d, LANES), lambda i, j: (layer, 0, 0)),
                  pl.BlockSpec((None, 1, LANES), lambda i, j: (layer, 0, 0)),
                  pl.BlockSpec((1, DH_NA), lambda i, j: (0, 0)),
                  pl.BlockSpec((1, DH_NA), lambda i, j: (0, 0))],
        out_specs=[pl.BlockSpec((tm, IN_TN), lambda i, j: (i, jnp.minimum(j, J_QK - 1))),
                   pl.BlockSpec((tm, IN_TN), lambda i, j: (i, jnp.clip(j - J_QK, 0, J_V - 1))),
                   pl.BlockSpec((tm, IN_TN), lambda i, j: (i, jnp.maximum(j - (J_QK + J_V), 0))),
                   pl.BlockSpec((tm, LANES), lambda i, j: (i, 0))],
        scratch_shapes=[pltpu.VMEM((d, IN_TN), BF16)],
        compiler_params=_cparams(("arbitrary", "arbitrary")), name="in_proj",
    )(hbf, w_in, w_gate_cols, gate_b_pad.reshape(-1, 1, LANES), qg.reshape(1, DH_NA), kg.reshape(1, DH_NA))


NT_DIMS = (((1,), (1,)), ((), ()))
TN_DIMS = (((0,), (0,)), ((), ()))
NA_QROWS = 4
NA_HEADS = 4
NA_KROWS = NA_QROWS + WIN_R
N_BIAS_PATTERNS = 3


def _na_block_start(qb, rows):
    return jnp.clip(qb * NA_QROWS - WIN_R // 2, 0, rows - NA_KROWS)


def _na_kernel(q_ref, k_ref, v_ref, kc_ref, vc_ref, bias_ref, o_ref, *, rows):
    qb = pl.program_id(2)
    nblk = rows // NA_QROWS
    pid = jnp.where(qb == 0, 0, jnp.where(qb == nblk - 1, 2, 1))
    start = pl.multiple_of(_na_block_start(qb, rows) * GRID_W, GRID_W)
    hr = range(NA_HEADS)
    sl = [slice(h * DH_NA, (h + 1) * DH_NA) for h in hr]
    win = pl.ds(start, NA_KROWS * GRID_W)
    s = [lax.dot_general(q_ref[:, sl[h]], k_ref[win, sl[h]], NT_DIMS, preferred_element_type=F32)
         + bias_ref[h, pid] for h in hr]
    sc = [lax.dot_general(q_ref[:, sl[h]], kc_ref[:, sl[h]], NT_DIMS, preferred_element_type=F32) for h in hr]
    m = [jnp.maximum(jnp.max(s[h], axis=-1, keepdims=True), jnp.max(sc[h], axis=-1, keepdims=True)) for h in hr]
    pb = [jnp.exp(s[h] - m[h]) for h in hr]
    pc = [jnp.exp(sc[h] - m[h]) for h in hr]
    den = [jnp.sum(pb[h], axis=-1, keepdims=True) + jnp.sum(pc[h], axis=-1, keepdims=True) for h in hr]
    o = [jnp.dot(pb[h].astype(BF16), v_ref[win, sl[h]], preferred_element_type=F32)
         + jnp.dot(pc[h].astype(BF16), vc_ref[:, sl[h]], preferred_element_type=F32) for h in hr]
    for h in hr:
        o_ref[:, sl[h]] = (o[h] / den[h]).astype(o_ref.dtype)


def na_bias_table(rpb, rows):
    nblk = rows // NA_QROWS
    qb = np.array([0, 2, nblk - 1])
    ks = np.clip(qb * NA_QROWS - WIN_R // 2, 0, rows - NA_KROWS)
    r = qb[:, None] * NA_QROWS + np.arange(NA_QROWS)[None, :]
    rs = np.clip(r - WIN_R // 2, 0, rows - WIN_R)
    kr = ks[:, None] + np.arange(NA_KROWS)[None, :]
    in_band = (kr[:, None, :] >= rs[:, :, None]) & (kr[:, None, :] < rs[:, :, None] + WIN_R)
    dr = kr[:, None, :] - r[:, :, None] + (WIN_R - 1)
    c = np.arange(GRID_W)
    dc = np.clip(c[None, :] - c[:, None], -(WIN_C - 1), WIN_C - 1) + (WIN_C - 1)
    cs = np.clip(c - WIN_C // 2, 0, GRID_W - WIN_C)
    col_ok = (c[None, :] >= cs[:, None]) & (c[None, :] < cs[:, None] + WIN_C)
    sel_r = jnp.asarray((dr[..., None] == np.arange(2 * WIN_R - 1)) & in_band[..., None], F32)
    sel_c = jnp.asarray(dc[..., None] == np.arange(2 * WIN_C - 1), F32)
    hi = lax.Precision.HIGHEST
    t = jnp.einsum("lhab,pxja->lhpxjb", rpb.astype(F32), sel_r, precision=hi)
    bias = jnp.einsum("lhpxjb,qkb->lhpxqjk", t, sel_c, precision=hi)
    ok = in_band[:, :, None, :, None] & col_ok[None, None, :, None, :]
    bias = jnp.where(jnp.asarray(ok)[None, None], bias, NEG)
    return bias.reshape(rpb.shape[0], H_NA, N_BIAS_PATTERNS, NA_QROWS * GRID_W, NA_KROWS * GRID_W)


def na_attention(qk, vbf, bias, layer, dims):
    rows = dims.seq // GRID_W
    qblocks = rows // NA_QROWS
    tq = NA_QROWS * GRID_W
    ctx_blk0 = dims.m_x // dims.ctx
    hgroups = H_NA // NA_HEADS
    hw = NA_HEADS * DH_NA
    kern = functools.partial(_na_kernel, rows=rows)
    return pl.pallas_call(
        kern,
        out_shape=jax.ShapeDtypeStruct((dims.m_x, H_NA * DH_NA), BF16),
        grid=(dims.batch, hgroups, qblocks),
        in_specs=[
            pl.BlockSpec((tq, hw), lambda b, h, i: (b * qblocks + i, h)),
            pl.BlockSpec((dims.seq, hw), lambda b, h, i: (b, hgroups + h)),
            pl.BlockSpec((dims.seq, hw), lambda b, h, i: (b, h)),
            pl.BlockSpec((dims.ctx, hw), lambda b, h, i: (ctx_blk0 + b, hgroups + h)),
            pl.BlockSpec((dims.ctx, hw), lambda b, h, i: (ctx_blk0 + b, h)),
            pl.BlockSpec((None, NA_HEADS, N_BIAS_PATTERNS, tq, NA_KROWS * GRID_W),
                         lambda b, h, i: (layer, h, 0, 0, 0)),
        ],
        out_specs=pl.BlockSpec((tq, hw), lambda b, h, i: (b * qblocks + i, h)),
        compiler_params=_cparams(("arbitrary", "arbitrary", "arbitrary")), name="na_attention",
    )(qk, qk, vbf, qk, vbf, bias)


def _ctx_attn_kernel(q_ref, k_ref, v_ref, o_ref):
    s = lax.dot_general(q_ref[...], k_ref[...], NT_DIMS, preferred_element_type=F32)
    m = jnp.max(s, axis=-1, keepdims=True)
    p = jnp.exp(s - m)
    den = jnp.sum(p, axis=-1, keepdims=True)
    o = jnp.dot(p.astype(BF16), v_ref[...], preferred_element_type=F32)
    o_ref[...] = (o / den).astype(o_ref.dtype)


def ctx_attention(qk, vbf, dims):
    ctx_blk0 = dims.m_x // dims.ctx
    return pl.pallas_call(
        _ctx_attn_kernel,
        out_shape=jax.ShapeDtypeStruct((dims.batch * dims.ctx, H_NA * DH_NA), BF16),
        grid=(dims.batch, H_NA),
        in_specs=[
            pl.BlockSpec((dims.ctx, DH_NA), lambda b, h: (ctx_blk0 + b, h)),
            pl.BlockSpec((dims.ctx, DH_NA), lambda b, h: (ctx_blk0 + b, H_NA + h)),
            pl.BlockSpec((dims.ctx, DH_NA), lambda b, h: (ctx_blk0 + b, h)),
        ],
        out_specs=pl.BlockSpec((dims.ctx, DH_NA), lambda b, h: (b, h)),
        compiler_params=_cparams(("arbitrary", "arbitrary")), name="ctx_attention",
    )(qk, qk, vbf)


def _log_sigmoid(x):
    return jnp.minimum(x, 0.0) - jnp.log(1.0 + jnp.exp(-jnp.abs(x)))


def _rope(t, cos, sin_signed, even):
    swapped = jnp.where(even, pltpu.roll(t, DK_M - 1, 1), pltpu.roll(t, 1, 1))
    return t * cos + swapped * sin_signed


def _mlstm_kernel(*refs):
    n_in = 7
    state = refs[2 * n_in + 2:]
    c_refs, m_refs = state[:2 * H_M], state[2 * H_M:]

    @pl.when(pl.program_id(1) == 0)
    def _():
        for ref in state:
            ref[...] = jnp.zeros(ref.shape, F32)

    _mlstm_chunks(refs[:n_in], refs[n_in:2 * n_in], refs[2 * n_in:2 * n_in + 2], c_refs, m_refs)


def _mlstm_chunks(fwd_refs, bwd_refs, o_refs, c_refs, m_refs):
    ti = lax.broadcasted_iota(I32, (CHUNK, CHUNK), 0)
    si = lax.broadcasted_iota(I32, (CHUNK, CHUNK), 1)
    even = (lax.broadcasted_iota(I32, (CHUNK, DK_M), 1) % 2) == 0
    ones_col = (lax.broadcasted_iota(I32, (CHUNK, DV_EXT - DV_M), 1) == 0).astype(BF16)
    per_dir = []
    for direction, (q_ref, k_ref, v_ref, g_ref, gt_ref, cos_ref, sin_ref) in enumerate((fwd_refs, bwd_refs)):
        mask = (si <= ti) if direction == 0 else (si >= ti)
        tri = mask.astype(BF16)
        g = g_ref[0]
        gt = gt_ref[0]
        lf_hi, lf_lo = _split_bf16(_log_sigmoid(g))
        cum_col = (jnp.dot(tri, lf_hi, preferred_element_type=F32)
                   + jnp.dot(tri, lf_lo, preferred_element_type=F32))
        lft = _log_sigmoid(gt)
        lft_hi, lft_lo = _split_bf16(lft)
        cum_row = (lax.dot_general(lft_hi, tri, NT_DIMS, preferred_element_type=F32)
                   + lax.dot_general(lft_lo, tri, NT_DIMS, preferred_element_type=F32))
        per_dir.append((q_ref, k_ref, v_ref, g, gt, lft, cum_col, cum_row, cos_ref[...], sin_ref[...], mask))
    items = [(d, h) for d in range(2) for h in range(H_M)]
    hr = range(len(items))
    qb, k, v_ext, b_col, i_col, b_row, i_row, b_last, masks = [], [], [], [], [], [], [], [], []
    for d, h in items:
        q_ref, k_ref, v_ref, g, gt, lft, cum_col, cum_row, cos, sin, mask = per_dir[d]
        qb.append(_rope(q_ref[:, h * DK_M:(h + 1) * DK_M], cos, sin, even).astype(BF16))
        k.append(_rope(k_ref[:, h * DK_M:(h + 1) * DK_M], cos, sin, even) * (DK_M ** -0.5))
        v_ext.append(jnp.concatenate([v_ref[:, h * DV_M:(h + 1) * DV_M].astype(BF16), ones_col], axis=1))
        b_col.append(cum_col[:, H_M + h:H_M + h + 1])
        i_col.append(g[:, h:h + 1])
        b_row.append(cum_row[H_M + h:H_M + h + 1, :])
        i_row.append(gt[h:h + 1, :])
        b_last.append(jnp.sum(lft[H_M + h:H_M + h + 1, :], axis=1, keepdims=True))
        masks.append(mask)
    m_old = [m_refs[h][0:1, 0:1] for h in hr]
    c_old = [c_refs[h][...] for h in hr]
    m_new = [jnp.maximum(b_last[h] + m_old[h], jnp.max(b_last[h] - b_row[h] + i_row[h], axis=1, keepdims=True))
             for h in hr]
    w_end = [jnp.exp(b_last[h] - b_col[h] + i_col[h] - m_new[h]) for h in hr]
    decay = [jnp.exp(b_last[h] + m_old[h] - m_new[h]) for h in hr]
    kw = [(k[h] * w_end[h]).astype(BF16) for h in hr]
    kv = [lax.dot_general(kw[h], v_ext[h], TN_DIMS, preferred_element_type=F32) for h in hr]
    d_mat = [jnp.where(masks[h], b_col[h] - b_row[h] + i_row[h], NEG) for h in hr]
    m_in = [b_col[h] + m_old[h] for h in hr]
    m_t = [jnp.maximum(m_in[h], jnp.max(d_mat[h], axis=1, keepdims=True)) for h in hr]
    qk = [lax.dot_general(qb[h], k[h].astype(BF16), NT_DIMS, preferred_element_type=F32) for h in hr]
    s = [(qk[h] * jnp.exp(d_mat[h] - m_t[h])).astype(BF16) for h in hr]
    a_in = [jnp.exp(m_in[h] - m_t[h]) for h in hr]
    q_c = [jnp.dot(qb[h], c_old[h].astype(BF16), preferred_element_type=F32) for h in hr]
    numden = [jnp.dot(s[h], v_ext[h], preferred_element_type=F32) + a_in[h] * q_c[h] for h in hr]
    for n, (d, h) in enumerate(items):
        c_refs[n][...] = decay[n] * c_old[n] + kv[n]
        m_refs[n][...] = jnp.broadcast_to(m_new[n], m_refs[n].shape)
        den = numden[n][:, DV_M:DV_M + 1]
        o_refs[d][:, h * DV_M:(h + 1) * DV_M] = (numden[n][:, :DV_M]
                                                 / jnp.maximum(jnp.abs(den), jnp.exp(-m_t[n])))


def mlstm_scan(p, g_dir, gt_dir, cos_t, sin_t, dims):
    ncx = dims.ctx // CHUNK
    nlx = dims.seq // CHUNK
    x_blocks = dims.m_x // CHUNK

    def row_block(b, d, t):
        in_ctx = t < ncx
        cc = jnp.where(d == 0, t, ncx - 1 - t)
        cx = jnp.where(d == 0, t - ncx, nlx - 1 - (t - ncx))
        return jnp.where(in_ctx, x_blocks + b * ncx + cc, b * nlx + cx)

    def rope_block(b, d, t):
        in_ctx = t < ncx
        cx = jnp.where(d == 0, t - ncx, nlx - 1 - (t - ncx))
        return jnp.where(in_ctx, nlx, cx)

    def dir_specs(d):
        return [
            pl.BlockSpec((CHUNK, H_M * DK_M), lambda b, t: (row_block(b, d, t), PM_Q // (H_M * DK_M))),
            pl.BlockSpec((CHUNK, H_M * DK_M), lambda b, t: (row_block(b, d, t), PM_K // (H_M * DK_M))),
            pl.BlockSpec((CHUNK, H_M * DV_M), lambda b, t: (row_block(b, d, t), PM_V // (H_M * DV_M))),
            pl.BlockSpec((1, CHUNK, LANES), lambda b, t: (d, row_block(b, d, t), 0)),
            pl.BlockSpec((1, 8, CHUNK), lambda b, t: (d, 0, row_block(b, d, t))),
            pl.BlockSpec((CHUNK, DK_M), lambda b, t: (rope_block(b, d, t), 0)),
            pl.BlockSpec((CHUNK, DK_M), lambda b, t: (rope_block(b, d, t), 0)),
        ]

    h_shape = jax.ShapeDtypeStruct((dims.m_all, H_M * DV_M), F32)
    dir_args = (p, p, p, g_dir, gt_dir, cos_t, sin_t)
    return pl.pallas_call(
        _mlstm_kernel,
        out_shape=[h_shape, h_shape],
        grid=(dims.batch, ncx + nlx),
        in_specs=dir_specs(0) + dir_specs(1),
        out_specs=[pl.BlockSpec((CHUNK, H_M * DV_M), lambda b, t: (row_block(b, 0, t), 0)),
                   pl.BlockSpec((CHUNK, H_M * DV_M), lambda b, t: (row_block(b, 1, t), 0))],
        scratch_shapes=([pltpu.VMEM((DK_M, DV_EXT), F32)] * (2 * H_M)
                        + [pltpu.VMEM((8, LANES), F32)] * (2 * H_M)),
        compiler_params=_cparams(("arbitrary", "arbitrary")), name="mlstm_scan",
    )(*dir_args, *dir_args)


def _mlstm_out_kernel(hf_ref, hb_ref, o_ref, g_ref, out_ref):
    for h in range(H_M):
        sl = slice(h * DV_M, (h + 1) * DV_M)
        y = _rms(hf_ref[:, sl] + hb_ref[:, sl], g_ref[:, sl])
        out_ref[:, sl] = (y * jax.nn.sigmoid(o_ref[:, sl])).astype(out_ref.dtype)


def mlstm_out(hf, hb, p, g, dims):
    m = dims.m_all
    tm = 256
    n = H_M * DV_M
    return pl.pallas_call(
        _mlstm_out_kernel,
        out_shape=jax.ShapeDtypeStruct((m, n), BF16),
        grid=(m // tm,),
        in_specs=[pl.BlockSpec((tm, n), lambda i: (i, 0)),
                  pl.BlockSpec((tm, n), lambda i: (i, 0)),
                  pl.BlockSpec((tm, n), lambda i: (i, PM_O // n)),
                  pl.BlockSpec((1, n), lambda i: (0, 0))],
        out_specs=pl.BlockSpec((tm, n), lambda i: (i, 0)),
        compiler_params=_cparams(("arbitrary",)), name="mlstm_out",
    )(hf, hb, p, g.reshape(1, n))


def rope_tables(dims):
    t = jnp.arange(dims.seq)
    row = (t // GRID_W).astype(F32)
    col = (t % GRID_W).astype(F32)
    quarter = DK_M // 4
    inv = ROPE_BASE ** (-jnp.arange(quarter, dtype=F32) / quarter)
    ang = jnp.concatenate([row[:, None] * inv, col[:, None] * inv], axis=-1)
    cos = jnp.repeat(jnp.cos(ang), 2, axis=-1)
    sin = jnp.repeat(jnp.sin(ang), 2, axis=-1) * jnp.tile(jnp.array([-1.0, 1.0], F32), DK_M // 2)
    cos = jnp.concatenate([cos, jnp.ones((CHUNK, DK_M), F32)], axis=0)
    sin = jnp.concatenate([sin, jnp.zeros((CHUNK, DK_M), F32)], axis=0)
    return cos, sin


def _gather_kernel(src_ref, nrows_ref, h_hbm, o_ref, buf_ref, sem, *, tm):
    i = pl.program_id(0)
    n = nrows_ref[i]

    @pl.when(n < tm)
    def _():
        buf_ref[...] = jnp.zeros(buf_ref.shape, buf_ref.dtype)

    def start(r, carry):
        pltpu.make_async_copy(h_hbm.at[pl.ds(src_ref[i * tm + r], 1)], buf_ref.at[pl.ds(r, 1)], sem).start()
        return carry

    lax.fori_loop(0, n, start, 0)

    def wait(r, carry):
        pltpu.make_async_copy(h_hbm.at[pl.ds(0, 1)], buf_ref.at[pl.ds(r, 1)], sem).wait()
        return carry

    lax.fori_loop(0, n, wait, 0)
    o_ref[...] = buf_ref[...].astype(o_ref.dtype)


def gather_rows(h, src, tile_rows, *, tm):
    m, d = h.shape
    r_total = src.shape[0]
    return pl.pallas_call(
        functools.partial(_gather_kernel, tm=tm),
        out_shape=jax.ShapeDtypeStruct((r_total, d), BF16),
        grid_spec=pltpu.PrefetchScalarGridSpec(
            num_scalar_prefetch=2,
            grid=(r_total // tm,),
            in_specs=[pl.BlockSpec(memory_space=pl.ANY)],
            out_specs=pl.BlockSpec((tm, d), lambda i, *s: (i, 0)),
            scratch_shapes=[pltpu.VMEM((tm, d), F32), pltpu.SemaphoreType.DMA],
        ),
        compiler_params=_cparams(("arbitrary",)), name="moe_gather",
    )(src, tile_rows, h)


def _combine_kernel(p1_ref, p2_ref, y_hbm, x_ref, gate_ref, rt_ref, o_ref, y1_ref, y2_ref, sem,
                    *, tm, seq, n_groups):
    i = pl.program_id(0)

    def start(r, carry):
        t = i * tm + r
        pltpu.make_async_copy(y_hbm.at[pl.ds(p1_ref[t], 1)], y1_ref.at[pl.ds(r, 1)], sem.at[0]).start()
        pltpu.make_async_copy(y_hbm.at[pl.ds(p2_ref[t], 1)], y2_ref.at[pl.ds(r, 1)], sem.at[1]).start()
        return carry

    lax.fori_loop(0, tm, start, 0)

    def wait(r, carry):
        pltpu.make_async_copy(y_hbm.at[pl.ds(0, 1)], y1_ref.at[pl.ds(r, 1)], sem.at[0]).wait()
        pltpu.make_async_copy(y_hbm.at[pl.ds(0, 1)], y2_ref.at[pl.ds(r, 1)], sem.at[1]).wait()
        return carry

    lax.fori_loop(0, tm, wait, 0)
    row0 = i * tm
    grp = jnp.int32(n_groups - 1)
    for b in reversed(range(n_groups - 1)):
        grp = jnp.where(row0 < (b + 1) * seq, b, grp)
    gate = gate_ref[pl.ds(grp, 1), :]
    w1 = rt_ref[:, 2:3]
    w2 = rt_ref[:, 3:4]
    o_ref[...] = x_ref[...] + gate * (w1 * y1_ref[...] + w2 * y2_ref[...])


def moe_combine(xs, y, pos1, pos2, route_t, mod, layer, gate_chunk, dims, *, m):
    d = dims.d_model
    tm = _pick_tile(math.gcd(dims.seq, dims.ctx), (256, 128))
    kern = functools.partial(_combine_kernel, tm=tm, seq=dims.seq, n_groups=dims.batch + 1)
    return pl.pallas_call(
        kern,
        out_shape=jax.ShapeDtypeStruct((m, d), F32),
        grid_spec=pltpu.PrefetchScalarGridSpec(
            num_scalar_prefetch=2,
            grid=(m // tm,),
            in_specs=[pl.BlockSpec(memory_space=pl.ANY),
                      pl.BlockSpec((tm, d), lambda i, a, b: (i, 0)),
                      pl.BlockSpec((8, d), lambda i, a, b: (MOD_ROWS // 8 * layer, gate_chunk)),
                      pl.BlockSpec((tm, 8), lambda i, a, b: (i, 0))],
            out_specs=pl.BlockSpec((tm, d), lambda i, a, b: (i, 0)),
            scratch_shapes=[pltpu.VMEM((tm, d), F32), pltpu.VMEM((tm, d), F32),
                            pltpu.SemaphoreType.DMA((2,))],
        ),
        compiler_params=_cparams(("arbitrary",)), name="moe_combine",
    )(pos1, pos2, y, xs, mod, route_t)


def moe_dispatch_plan(route, m, tm):
    e = jnp.concatenate([route[0, :m], route[1, :m]]).astype(I32)
    tok = jnp.tile(jnp.arange(m, dtype=I32), 2)
    onehot = (e[:, None] == jnp.arange(N_EXPERTS, dtype=I32)[None, :]).astype(I32)
    rank = jnp.sum((jnp.cumsum(onehot, axis=0) - onehot) * onehot, axis=1)
    counts = jnp.sum(onehot, axis=0)
    padded = ((counts + tm - 1) // tm) * tm
    ends = jnp.cumsum(padded)
    offs = ends - padded
    pos = offs[e] + rank
    r_total = TOP_K * m + N_EXPERTS * tm
    src = jnp.zeros((r_total,), I32).at[pos].set(tok)
    tile_start = jnp.arange(r_total // tm, dtype=I32) * tm
    last_e = jnp.max(jnp.where(counts > 0, jnp.arange(N_EXPERTS, dtype=I32), 0))
    tile_e = jnp.minimum(jnp.sum((tile_start[:, None] >= ends[None, :]).astype(I32), axis=1), last_e)
    tile_rows = jnp.clip((offs + counts)[tile_e] - tile_start, 0, tm).astype(I32)
    return src, pos[:m], pos[m:], tile_e, tile_rows


def _dense_tile(m, candidates):
    return _pick_tile(m, candidates)


def kernel(x, c, ctx, c_ctx, ada_w, ada_b, norm1_g, norm2_g, w_in, gate_b, na_q_g, na_k_g, na_rpb, m_norm_g, w_out, ffn_w_gate, ffn_w_up, ffn_w_down, moe_router, moe_w_gate, moe_w_up, moe_w_down):
    batch, seq, d = x.shape
    dims = Dims(batch=batch, seq=seq, ctx=ctx.shape[1], d_model=d, d_ff=ffn_w_gate.shape[2], depth=ada_w.shape[0])
    depth = dims.depth
    n_groups = batch + 1
    rows = seq // GRID_W
    assert rows >= 4 * NA_QROWS and rows % NA_QROWS == 0 and batch + 1 <= 8
    assert seq % 256 == 0 and dims.ctx % CHUNK == 0 and dims.m_x % dims.ctx == 0

    xs = jnp.concatenate([x.reshape(dims.m_x, d), ctx.reshape(batch * dims.ctx, d)], axis=0)

    cond = jnp.concatenate([jax.nn.silu(c), jax.nn.silu(c_ctx)[None], jnp.zeros((MOD_ROWS - n_groups, d), F32)], axis=0)
    tn_ada = _pick_tile(6 * d, (1024, 512, 256))
    tn_d = min(512, d)
    mod = gmm((cond.astype(BF16),), (ada_w,), jnp.arange(depth, dtype=I32), jnp.full((depth,), MOD_ROWS, I32),
              tm=MOD_ROWS, tn=tn_ada, n_out=6 * d, out_dtype=F32, epilogue="bias",
              extras=(ada_b.reshape(depth, 1, 6 * d),),
              extra_specs=(((1, 1, tn_ada), lambda j, i, g, *s: (g[i], 0, j)),),
              a_fixed=True, rows_outer=True, name="ada_mod")

    cos_t, sin_t = rope_tables(dims)
    na_bias = na_bias_table(na_rpb, rows)
    w_gate_cols = jnp.pad(w_in[:, :, N_MAIN:], ((0, 0), (0, 0), (0, LANES - 4 * H_M)))
    gate_b_pad = jnp.pad(gate_b, ((0, 0), (0, LANES - 4 * H_M)))

    def dense_groups(m, tm, g):
        nt = m // tm
        return jnp.full((nt,), g, I32), jnp.full((nt,), tm, I32)

    for layer in range(depth):
        last = layer == depth - 1
        m_out = dims.m_x if last else dims.m_all

        hbf = norm_mod(xs, norm1_g[layer], mod, layer, 0, 1, dims, m=dims.m_all)
        qk, vbf, pm, gates = in_proj(hbf, w_in, w_gate_cols, gate_b_pad, layer, na_q_g[layer], na_k_g[layer])
        g_dir = jnp.stack([gates, jnp.roll(gates, -2 * H_M, axis=1)])
        gt_dir = jnp.stack([gates[:, :2 * H_M].T, gates[:, 2 * H_M:4 * H_M].T])

        na_out = na_attention(qk, vbf, na_bias, layer, dims)
        hf, hb = mlstm_scan(pm, g_dir, gt_dir, cos_t, sin_t, dims)
        m_mix = mlstm_out(hf, hb, pm, m_norm_g[layer], dims)
        if not last:
            na_out = jnp.concatenate([na_out, ctx_attention(qk, vbf, dims)], axis=0)

        tm_o = _dense_tile(m_out, (1088, 1024, 640, 512, 256))
        grp, nv = dense_groups(m_out, tm_o, layer)
        res_specs = lambda tm, tn, chunk: (
            ((tm, tn), lambda j, i, *s: (i, j)),
            ((8, tn), lambda j, i, *s: (MOD_ROWS // 8 * layer, chunk * (d // tn) + j)))
        tn_o = min(1024, d)
        xs = gmm((na_out, m_mix), (w_out,), grp, nv, tm=tm_o, tn=tn_o, n_out=d, out_dtype=F32, epilogue="residual",
                 extras=(xs, mod), extra_specs=res_specs(tm_o, tn_o, 2), seq=seq, n_groups=n_groups,
                 name="w_out_proj")

        idx = layer // 2
        if layer % 2 == 0:
            h2 = norm_mod(xs, norm2_g[layer], mod, layer, 3, 4, dims, m=m_out)
            tm_u = _dense_tile(m_out, (1088, 1024, 640, 512, 256))
            grp, nv = dense_groups(m_out, tm_u, idx)
            act = gmm_swiglu(h2, ffn_w_gate, ffn_w_up, grp, nv, tm=tm_u, tn=512, name="ffn_up")
            tm_d = _dense_tile(m_out, (544, 512, 320, 256))
            grp, nv = dense_groups(m_out, tm_d, idx)
            xs = gmm((act,), (ffn_w_down,), grp, nv, tm=tm_d, tn=tn_d, n_out=d, out_dtype=F32, epilogue="residual",
                     extras=(xs, mod), extra_specs=res_specs(tm_d, tn_d, 5), seq=seq, n_groups=n_groups,
                     name="ffn_down")
        else:
            h2, route = norm_mod(xs, norm2_g[layer], mod, layer, 3, 4, dims, m=m_out,
                                 router_t=moe_router[idx].T)
            tm_e = 512 if m_out >= 4096 else 128
            src, pos1, pos2, tile_e, tile_rows = moe_dispatch_plan(route, m_out, tm_e)
            hs = gather_rows(h2, src, tile_rows, tm=tm_e)
            grp = tile_e + idx * N_EXPERTS
            nw = moe_w_gate.shape[0] * N_EXPERTS
            act = gmm_swiglu(hs, moe_w_gate.reshape(nw, d, dims.d_ff), moe_w_up.reshape(nw, d, dims.d_ff),
                             grp, tile_rows, tm=tm_e, tn=512, half_tiles=True, name="moe_up")
            y = gmm((act,), (moe_w_down.reshape(nw, dims.d_ff, d),), grp, tile_rows, tm=tm_e, tn=tn_d, n_out=d,
                    out_dtype=F32, half_tiles=True, name="moe_down")
            xs = moe_combine(xs, y, pos1, pos2, route.T, mod, layer, 5, dims, m=m_out)

    return xs[:dims.m_x].reshape(batch, seq, d)
```

```python
import functools
import math
from typing import NamedTuple

import jax
import jax.numpy as jnp
import numpy as np
from jax import lax
from jax.experimental import pallas as pl
from jax.experimental.pallas import tpu as pltpu

F32 = jnp.float32
BF16 = jnp.bfloat16
I32 = jnp.int32

GRID_W = 64
H_NA = 8
DH_NA = 128
WIN_R = 8
WIN_C = 16
H_M = 4
DK_M = 128
DV_M = 256
CHUNK = 128
ROPE_BASE = 10000.0
N_EXPERTS = 8
TOP_K = 2
EPS = 1e-6
D_MIX = 2048
NEG = -1e30

OFF_MQ = 3 * H_NA * DH_NA
N_MAIN = 6144
PM_Q, PM_K, PM_V, PM_O = 0, H_M * DK_M, 2 * H_M * DK_M, 2 * H_M * DK_M + H_M * DV_M
DV_EXT = DV_M + 128

LANES = 128
VMEM_LIMIT = 56 * 1024 * 1024
MOD_ROWS = 16


class Dims(NamedTuple):
    batch: int
    seq: int
    ctx: int
    d_model: int
    d_ff: int
    depth: int

    @property
    def m_x(self):
        return self.batch * self.seq

    @property
    def m_all(self):
        return self.batch * (self.seq + self.ctx)


def _cparams(sem):
    return pltpu.CompilerParams(dimension_semantics=sem, vmem_limit_bytes=VMEM_LIMIT)


def _group_select(row, vals, seq):
    out = vals[-1]
    for b in reversed(range(len(vals) - 1)):
        out = jnp.where(row < (b + 1) * seq, vals[b], out)
    return out


def _pick_tile(m, candidates):
    for t in candidates:
        if m % t == 0:
            return t
    raise ValueError(f"no row tile for {m}")


N_EXTRA = {"residual": 2, "bias": 1, "swiglu": 0, None: 0}


def _gmm_kernel(grp_ref, rows_ref, rid_ref, rgrp_ref, nruns_ref, *refs,
                n_a, n_w, epilogue, tm, tn, nj, seq, n_groups, rows_outer, half_tiles):
    a_refs, refs = refs[:n_a], refs[n_a:]
    w_refs, refs = refs[:n_w], refs[n_w:]
    x_refs, refs = refs[:N_EXTRA[epilogue]], refs[N_EXTRA[epilogue]:]
    o_ref, refs = refs[0], refs[1:]
    wbf_refs, refs = refs[:n_w], refs[n_w:]
    if rows_outer:
        i = pl.program_id(0)
        for w_ref, wbf_ref in zip(w_refs, wbf_refs):
            wbf_ref[...] = w_ref[0].astype(BF16)
    else:
        wst_refs, sem = refs[:n_w], refs[n_w]
        j = pl.program_id(0)
        i = pl.program_id(1)
        rid = rid_ref[i]
        nruns = nruns_ref[0]

        def tile_copy(widx, g, jj, slot):
            src = w_refs[widx].at[g, :, pl.ds(pl.multiple_of(jj * tn, tn), tn)]
            return pltpu.make_async_copy(src, wst_refs[widx].at[slot], sem.at[widx, slot])

        @pl.when((i == 0) | (rid != rid_ref[jnp.maximum(i - 1, 0)]))
        def _():
            q = j * nruns + rid
            slot = q % 2

            @pl.when(q == 0)
            def _():
                for widx in range(n_w):
                    tile_copy(widx, rgrp_ref[0], 0, 0).start()

            for widx in range(n_w):
                tile_copy(widx, 0, 0, slot).wait()
            wrap = rid + 1 == nruns
            next_run = jnp.where(wrap, 0, rid + 1)
            next_j = j + wrap.astype(I32)

            @pl.when(next_j < nj)
            def _():
                for widx in range(n_w):
                    tile_copy(widx, rgrp_ref[next_run], next_j, 1 - slot).start()

            for widx in range(n_w):
                wbf_refs[widx][...] = wst_refs[widx][slot].astype(BF16)

    n_rows = rows_ref[i]

    def compute(nr):
        def matmul(wbf_ref):
            out, k0 = None, 0
            for a_ref in a_refs:
                kp = a_ref.shape[1]
                part = jnp.dot(a_ref[0:nr, :], wbf_ref[k0:k0 + kp, :], preferred_element_type=F32)
                out = part if out is None else out + part
                k0 += kp
            return out

        acc = matmul(wbf_refs[0])
        if epilogue == "residual":
            res_ref, gate_ref = x_refs
            row = i * tm + lax.broadcasted_iota(I32, (nr, 1), 0)
            gate = _group_select(row, [gate_ref[k:k + 1, :] for k in range(n_groups)], seq)
            acc = res_ref[0:nr, :] + gate * acc
        elif epilogue == "bias":
            acc = acc + x_refs[0][0]
        elif epilogue == "swiglu":
            acc = acc * jax.nn.sigmoid(acc) * matmul(wbf_refs[1])
        o_ref[0:nr, :] = acc.astype(o_ref.dtype)
        if nr < tm:
            o_ref[nr:, :] = jnp.zeros((tm - nr, o_ref.shape[1]), o_ref.dtype)

    half = tm // 2 if half_tiles else 0
    pl.when(n_rows > half)(lambda: compute(tm))
    if half_tiles:
        pl.when((n_rows > 0) & (n_rows <= half))(lambda: compute(half))

    @pl.when(n_rows == 0)
    def _():
        o_ref[...] = jnp.zeros(o_ref.shape, o_ref.dtype)


def _runs(grp):
    nt = grp.shape[0]
    first = jnp.concatenate([jnp.ones((1,), I32), (grp[1:] != grp[:-1]).astype(I32)])
    run_id = jnp.cumsum(first) - 1
    sel = (run_id[:, None] == jnp.arange(nt, dtype=I32)[None, :]).astype(I32) * first[:, None]
    run_grp = jnp.sum(sel * grp[:, None], axis=0)
    return run_id.astype(I32), run_grp.astype(I32), (run_id[-1:] + 1).astype(I32)


def gmm(a_parts, ws, grp, tile_rows, *, tm, tn, n_out, out_dtype, epilogue=None, extras=(), extra_specs=(),
        a_fixed=False, seq=0, n_groups=0, rows_outer=False, half_tiles=False, name="gmm"):
    k = sum(a.shape[1] for a in a_parts)
    nt = grp.shape[0]
    n_w = len(ws)
    nj = n_out // tn

    def spec(shape, fn):
        return pl.BlockSpec(shape, (lambda i, j, *s: fn(j, i, *s)) if rows_outer else fn)

    a_map = (lambda j, i, *s: (0, 0)) if a_fixed else (lambda j, i, *s: (i, 0))
    if rows_outer:
        w_specs = [spec((1, k, tn), lambda j, i, g, *s: (g[i], 0, j))] * n_w
        stream_scratch = []
    else:
        w_specs = [pl.BlockSpec(memory_space=pl.ANY)] * n_w
        stream_scratch = [pltpu.VMEM((2, k, tn), F32)] * n_w + [pltpu.SemaphoreType.DMA((n_w, 2))]
    in_specs = ([spec((tm, a.shape[1]), a_map) for a in a_parts] + w_specs
                + [spec(shape, fn) for shape, fn in extra_specs])
    kern = functools.partial(_gmm_kernel, n_a=len(a_parts), n_w=n_w, epilogue=epilogue, tm=tm, tn=tn, nj=nj,
                             seq=seq, n_groups=n_groups, rows_outer=rows_outer, half_tiles=half_tiles)
    run_id, run_grp, nruns = _runs(grp)
    return pl.pallas_call(
        kern,
        out_shape=jax.ShapeDtypeStruct((nt * tm, n_out), out_dtype),
        grid_spec=pltpu.PrefetchScalarGridSpec(
            num_scalar_prefetch=5,
            grid=(nt, nj) if rows_outer else (nj, nt),
            in_specs=in_specs,
            out_specs=spec((tm, tn), lambda j, i, *s: (i, j)),
            scratch_shapes=[pltpu.VMEM((k, tn), BF16)] * n_w + stream_scratch,
        ),
        compiler_params=_cparams(("arbitrary", "arbitrary")),
        name=name,
    )(grp, tile_rows, run_id, run_grp, nruns, *a_parts, *ws, *extras)


def gmm_swiglu(a, wg, wu, grp, tile_rows, *, tm, tn, half_tiles=False, name="gmm_swiglu"):
    return gmm((a,), (wg, wu), grp, tile_rows, tm=tm, tn=tn, n_out=wg.shape[2], out_dtype=BF16,
               epilogue="swiglu", half_tiles=half_tiles, name=name)


def _rms(x, g):
    return x * lax.rsqrt(jnp.mean(x * x, axis=-1, keepdims=True) + EPS) * g


def _split_bf16(x):
    hi = x.astype(BF16)
    lo = (x - hi.astype(F32)).astype(BF16)
    return hi, lo


def _norm_mod_kernel(x_ref, g_ref, shift_ref, scale_ref, *refs, tm, seq, n_groups, with_router):
    if with_router:
        r_ref, o_ref, route_ref = refs
    else:
        (o_ref,) = refs
    i = pl.program_id(0)
    row0 = i * tm
    grp = jnp.int32(n_groups - 1)
    for b in reversed(range(n_groups - 1)):
        grp = jnp.where(row0 < (b + 1) * seq, b, grp)
    y = _rms(x_ref[...], g_ref[...])
    h = y * (1.0 + scale_ref[pl.ds(grp, 1), :]) + shift_ref[pl.ds(grp, 1), :]
    o_ref[...] = h.astype(o_ref.dtype)
    if with_router:
        nt_dims = (((1,), (1,)), ((), ()))
        h_hi, h_lo = _split_bf16(h)
        r_hi, r_lo = _split_bf16(r_ref[...])
        logits = (lax.dot_general(r_hi, h_hi, nt_dims, preferred_element_type=F32)
                  + lax.dot_general(r_hi, h_lo, nt_dims, preferred_element_type=F32)
                  + lax.dot_general(r_lo, h_hi, nt_dims, preferred_element_type=F32))
        ids = lax.broadcasted_iota(I32, logits.shape, 0)
        m1 = jnp.max(logits, axis=0, keepdims=True)
        i1 = jnp.min(jnp.where(logits == m1, ids, N_EXPERTS), axis=0, keepdims=True)
        rest = jnp.where(ids == i1, -jnp.inf, logits)
        m2 = jnp.max(rest, axis=0, keepdims=True)
        i2 = jnp.min(jnp.where(rest == m2, ids, N_EXPERTS), axis=0, keepdims=True)
        e = jnp.exp(m2 - m1)
        w1 = 1.0 / (1.0 + e)
        w2 = e / (1.0 + e)
        out_row = lax.broadcasted_iota(I32, (8, tm), 0)
        route_ref[...] = jnp.where(out_row == 0, i1.astype(F32), jnp.where(
            out_row == 1, i2.astype(F32), jnp.where(out_row == 2, w1, jnp.where(out_row == 3, w2, 0.0))))


def norm_mod(xs, g, mod, layer, shift_chunk, scale_chunk, dims, *, m, router_t=None):
    d = dims.d_model
    tm = _pick_tile(math.gcd(dims.seq, dims.ctx), (256, 128))
    with_router = router_t is not None
    kern = functools.partial(_norm_mod_kernel, tm=tm, seq=dims.seq, n_groups=dims.batch + 1,
                             with_router=with_router)
    in_specs = [
        pl.BlockSpec((tm, d), lambda i: (i, 0)),
        pl.BlockSpec((1, d), lambda i: (0, 0)),
        pl.BlockSpec((8, d), lambda i: (MOD_ROWS // 8 * layer, shift_chunk)),
        pl.BlockSpec((8, d), lambda i: (MOD_ROWS // 8 * layer, scale_chunk)),
    ]
    args = [xs, g.reshape(1, d), mod, mod]
    out_shape = [jax.ShapeDtypeStruct((m, d), F32 if with_router else BF16)]
    out_specs = [pl.BlockSpec((tm, d), lambda i: (i, 0))]
    if with_router:
        in_specs.append(pl.BlockSpec((N_EXPERTS, d), lambda i: (0, 0)))
        args.append(router_t)
        out_shape.append(jax.ShapeDtypeStruct((8, m), F32))
        out_specs.append(pl.BlockSpec((8, tm), lambda i: (0, i)))
    res = pl.pallas_call(
        kern, out_shape=out_shape, grid=(m // tm,), in_specs=in_specs, out_specs=out_specs,
        compiler_params=_cparams(("arbitrary",)), name="norm_mod_router" if with_router else "norm_mod",
    )(*args)
    return res if with_router else res[0]


IN_TN = 512
J_QK = 2 * H_NA * DH_NA // IN_TN
J_V = H_NA * DH_NA // IN_TN
N_MLSTM = N_MAIN - OFF_MQ


def _in_proj_kernel(a_ref, w_ref, wgate_ref, gb_ref, qg_ref, kg_ref,
                    qk_ref, v_ref, pm_ref, gates_ref, wbf_ref):
    j = pl.program_id(1)

    @pl.when(j == 0)
    def _():
        gates_ref[...] = (jnp.dot(a_ref[...], wgate_ref[0].astype(BF16), preferred_element_type=F32)
                          + gb_ref[...])

    wbf_ref[...] = w_ref[0].astype(BF16)
    acc = jnp.dot(a_ref[...], wbf_ref[...], preferred_element_type=F32)

    @pl.when(j < J_QK)
    def _():
        g = jnp.where(j < J_QK // 2, qg_ref[...] * (DH_NA ** -0.5), kg_ref[...])
        for h in range(IN_TN // DH_NA):
            sl = slice(h * DH_NA, (h + 1) * DH_NA)
            qk_ref[:, sl] = _rms(acc[:, sl], g).astype(BF16)

    @pl.when((j >= J_QK) & (j < J_QK + J_V))
    def _():
        v_ref[...] = acc.astype(BF16)

    @pl.when(j >= J_QK + J_V)
    def _():
        pm_ref[...] = acc


def in_proj(hbf, w_in, w_gate_cols, gate_b_pad, layer, qg, kg):
    m, d = hbf.shape
    tm = _pick_tile(m, (2176, 1088, 640, 512, 256))
    nj = N_MAIN // IN_TN
    return pl.pallas_call(
        _in_proj_kernel,
        out_shape=[jax.ShapeDtypeStruct((m, 2 * H_NA * DH_NA), BF16),
                   jax.ShapeDtypeStruct((m, H_NA * DH_NA), BF16),
                   jax.ShapeDtypeStruct((m, N_MLSTM), F32),
                   jax.ShapeDtypeStruct((m, LANES), F32)],
        grid=(m // tm, nj),
        in_specs=[pl.BlockSpec((tm, d), lambda i, j: (i, 0)),
                  pl.BlockSpec((1, d, IN_TN), lambda i, j: (layer, 0, j)),
                  pl.BlockSpec((1, d, LANES), lambda i, j: (layer, 0, 0)),
                  pl.BlockSpec((None, 1, LANES), lambda i, j: (layer, 0, 0)),
                  pl.BlockSpec((1, DH_NA), lambda i, j: (0, 0)),
                  pl.BlockSpec((1, DH_NA), lambda i, j: (0, 0))],
        out_specs=[pl.BlockSpec((tm, IN_TN), lambda i, j: (i, jnp.minimum(j, J_QK - 1))),
                   pl.BlockSpec((tm, IN_TN), lambda i, j: (i, jnp.clip(j - J_QK, 0, J_V - 1))),
                   pl.BlockSpec((tm, IN_TN), lambda i, j: (i, jnp.maximum(j - (J_QK + J_V), 0))),
                   pl.BlockSpec((tm, LANES), lambda i, j: (i, 0))],
        scratch_shapes=[pltpu.VMEM((d, IN_TN), BF16)],
        compiler_params=_cparams(("arbitrary", "arbitrary")), name="in_proj",
    )(hbf, w_in, w_gate_cols, gate_b_pad.reshape(-1, 1, LANES), qg.reshape(1, DH_NA), kg.reshape(1, DH_NA))


NT_DIMS = (((1,), (1,)), ((), ()))
TN_DIMS = (((0,), (0,)), ((), ()))
NA_QROWS = 4
NA_HEADS = 4
NA_KROWS = NA_QROWS + WIN_R
N_BIAS_PATTERNS = 3


def _na_block_start(qb, rows):
    return jnp.clip(qb * NA_QROWS - WIN_R // 2, 0, rows - NA_KROWS)


def _na_kernel(q_ref, k_ref, v_ref, kc_ref, vc_ref, bias_ref, o_ref, *, rows):
    qb = pl.program_id(2)
    nblk = rows // NA_QROWS
    pid = jnp.where(qb == 0, 0, jnp.where(qb == nblk - 1, 2, 1))
    start = pl.multiple_of(_na_block_start(qb, rows) * GRID_W, GRID_W)
    hr = range(NA_HEADS)
    sl = [slice(h * DH_NA, (h + 1) * DH_NA) for h in hr]
    win = pl.ds(start, NA_KROWS * GRID_W)
    s = [lax.dot_general(q_ref[:, sl[h]], k_ref[win, sl[h]], NT_DIMS, preferred_element_type=F32)
         + bias_ref[h, pid] for h in hr]
    sc = [lax.dot_general(q_ref[:, sl[h]], kc_ref[:, sl[h]], NT_DIMS, preferred_element_type=F32) for h in hr]
    m = [jnp.maximum(jnp.max(s[h], axis=-1, keepdims=True), jnp.max(sc[h], axis=-1, keepdims=True)) for h in hr]
    pb = [jnp.exp(s[h] - m[h]) for h in hr]
    pc = [jnp.exp(sc[h] - m[h]) for h in hr]
    den = [jnp.sum(pb[h], axis=-1, keepdims=True) + jnp.sum(pc[h], axis=-1, keepdims=True) for h in hr]
    o = [jnp.dot(pb[h].astype(BF16), v_ref[win, sl[h]], preferred_element_type=F32)
         + jnp.dot(pc[h].astype(BF16), vc_ref[:, sl[h]], preferred_element_type=F32) for h in hr]
    for h in hr:
        o_ref[:, sl[h]] = (o[h] / den[h]).astype(o_ref.dtype)


N_ROW_OFFSETS = 2 * WIN_R - 1


def _na_bias_kernel(tz_ref, o_ref, *, block_of):
    for p in range(N_BIAS_PATTERNS):
        for x in range(NA_QROWS):
            for j in range(NA_KROWS):
                o_ref[p, x * GRID_W:(x + 1) * GRID_W, j * GRID_W:(j + 1) * GRID_W] = tz_ref[block_of[p][x][j]]


def na_bias_table(rpb, rows):
    nblk = rows // NA_QROWS
    qb = np.array([0, 2, nblk - 1])
    ks = np.clip(qb * NA_QROWS - WIN_R // 2, 0, rows - NA_KROWS)
    r = qb[:, None] * NA_QROWS + np.arange(NA_QROWS)[None, :]
    rs = np.clip(r - WIN_R // 2, 0, rows - WIN_R)
    kr = ks[:, None] + np.arange(NA_KROWS)[None, :]
    in_band = (kr[:, None, :] >= rs[:, :, None]) & (kr[:, None, :] < rs[:, :, None] + WIN_R)
    dr = kr[:, None, :] - r[:, :, None] + (WIN_R - 1)
    block_of = np.where(in_band, dr, N_ROW_OFFSETS).tolist()
    c = np.arange(GRID_W)
    dc = np.clip(c[None, :] - c[:, None], -(WIN_C - 1), WIN_C - 1) + (WIN_C - 1)
    cs = np.clip(c - WIN_C // 2, 0, GRID_W - WIN_C)
    col_ok = (c[None, :] >= cs[:, None]) & (c[None, :] < cs[:, None] + WIN_C)
    sel_c = jnp.asarray(dc[..., None] == np.arange(2 * WIN_C - 1), F32)
    tz = jnp.einsum("lhab,qkb->lhaqk", rpb.astype(F32), sel_c, precision=lax.Precision.HIGHEST)
    tz = jnp.where(jnp.asarray(col_ok), tz, NEG)
    n_l = rpb.shape[0]
    tz = jnp.concatenate([tz, jnp.full((n_l, H_NA, 1, GRID_W, GRID_W), NEG, F32)], axis=2)
    tq, tk = NA_QROWS * GRID_W, NA_KROWS * GRID_W
    return pl.pallas_call(
        functools.partial(_na_bias_kernel, block_of=block_of),
        out_shape=jax.ShapeDtypeStruct((n_l, H_NA, N_BIAS_PATTERNS, tq, tk), F32),
        grid=(n_l, H_NA),
        in_specs=[pl.BlockSpec((None, None, N_ROW_OFFSETS + 1, GRID_W, GRID_W), lambda l, h: (l, h, 0, 0, 0))],
        out_specs=pl.BlockSpec((None, None, N_BIAS_PATTERNS, tq, tk), lambda l, h: (l, h, 0, 0, 0)),
        compiler_params=_cparams(("arbitrary", "arbitrary")), name="na_bias",
    )(tz)


def na_attention(qk, vbf, bias, layer, dims):
    rows = dims.seq // GRID_W
    qblocks = rows // NA_QROWS
    tq = NA_QROWS * GRID_W
    ctx_blk0 = dims.m_x // dims.ctx
    hgroups = H_NA // NA_HEADS
    hw = NA_HEADS * DH_NA
    kern = functools.partial(_na_kernel, rows=rows)
    return pl.pallas_call(
        kern,
        out_shape=jax.ShapeDtypeStruct((dims.m_x, H_NA * DH_NA), BF16),
        grid=(dims.batch, hgroups, qblocks),
        in_specs=[
            pl.BlockSpec((tq, hw), lambda b, h, i: (b * qblocks + i, h)),
            pl.BlockSpec((dims.seq, hw), lambda b, h, i: (b, hgroups + h)),
            pl.BlockSpec((dims.seq, hw), lambda b, h, i: (b, h)),
            pl.BlockSpec((dims.ctx, hw), lambda b, h, i: (ctx_blk0 + b, hgroups + h)),
            pl.BlockSpec((dims.ctx, hw), lambda b, h, i: (ctx_blk0 + b, h)),
            pl.BlockSpec((None, NA_HEADS, N_BIAS_PATTERNS, tq, NA_KROWS * GRID_W),
                         lambda b, h, i: (layer, h, 0, 0, 0)),
        ],
        out_specs=pl.BlockSpec((tq, hw), lambda b, h, i: (b * qblocks + i, h)),
        compiler_params=_cparams(("arbitrary", "arbitrary", "arbitrary")), name="na_attention",
    )(qk, qk, vbf, qk, vbf, bias)


def _ctx_attn_kernel(q_ref, k_ref, v_ref, o_ref):
    s = lax.dot_general(q_ref[...], k_ref[...], NT_DIMS, preferred_element_type=F32)
    m = jnp.max(s, axis=-1, keepdims=True)
    p = jnp.exp(s - m)
    den = jnp.sum(p, axis=-1, keepdims=True)
    o = jnp.dot(p.astype(BF16), v_ref[...], preferred_element_type=F32)
    o_ref[...] = (o / den).astype(o_ref.dtype)


def ctx_attention(qk, vbf, dims):
    ctx_blk0 = dims.m_x // dims.ctx
    return pl.pallas_call(
        _ctx_attn_kernel,
        out_shape=jax.ShapeDtypeStruct((dims.batch * dims.ctx, H_NA * DH_NA), BF16),
        grid=(dims.batch, H_NA),
        in_specs=[
            pl.BlockSpec((dims.ctx, DH_NA), lambda b, h: (ctx_blk0 + b, h)),
            pl.BlockSpec((dims.ctx, DH_NA), lambda b, h: (ctx_blk0 + b, H_NA + h)),
            pl.BlockSpec((dims.ctx, DH_NA), lambda b, h: (ctx_blk0 + b, h)),
        ],
        out_specs=pl.BlockSpec((dims.ctx, DH_NA), lambda b, h: (b, h)),
        compiler_params=_cparams(("arbitrary", "arbitrary")), name="ctx_attention",
    )(qk, qk, vbf)


def _log_sigmoid(x):
    return jnp.minimum(x, 0.0) - jnp.log(1.0 + jnp.exp(-jnp.abs(x)))


def _rope(t, cos, sin_signed, even):
    swapped = jnp.where(even, pltpu.roll(t, DK_M - 1, 1), pltpu.roll(t, 1, 1))
    return t * cos + swapped * sin_signed


def _mlstm_kernel(*refs):
    n_in = 7
    state = refs[2 * n_in + 2:]
    c_refs, m_refs = state[:2 * H_M], state[2 * H_M:]

    @pl.when(pl.program_id(1) == 0)
    def _():
        for ref in state:
            ref[...] = jnp.zeros(ref.shape, F32)

    _mlstm_chunks(refs[:n_in], refs[n_in:2 * n_in], refs[2 * n_in:2 * n_in + 2], c_refs, m_refs)


def _mlstm_chunks(fwd_refs, bwd_refs, o_refs, c_refs, m_refs):
    ti = lax.broadcasted_iota(I32, (CHUNK, CHUNK), 0)
    si = lax.broadcasted_iota(I32, (CHUNK, CHUNK), 1)
    even = (lax.broadcasted_iota(I32, (CHUNK, DK_M), 1) % 2) == 0
    ones_col = (lax.broadcasted_iota(I32, (CHUNK, DV_EXT - DV_M), 1) == 0).astype(BF16)
    per_dir = []
    for direction, (q_ref, k_ref, v_ref, g_ref, gt_ref, cos_ref, sin_ref) in enumerate((fwd_refs, bwd_refs)):
        mask = (si <= ti) if direction == 0 else (si >= ti)
        tri = mask.astype(BF16)
        g = g_ref[0]
        gt = gt_ref[0]
        lf_hi, lf_lo = _split_bf16(_log_sigmoid(g))
        cum_col = (jnp.dot(tri, lf_hi, preferred_element_type=F32)
                   + jnp.dot(tri, lf_lo, preferred_element_type=F32))
        lft = _log_sigmoid(gt)
        lft_hi, lft_lo = _split_bf16(lft)
        cum_row = (lax.dot_general(lft_hi, tri, NT_DIMS, preferred_element_type=F32)
                   + lax.dot_general(lft_lo, tri, NT_DIMS, preferred_element_type=F32))
        per_dir.append((q_ref, k_ref, v_ref, g, gt, lft, cum_col, cum_row, cos_ref[...], sin_ref[...], mask))
    items = [(d, h) for d in range(2) for h in range(H_M)]
    hr = range(len(items))
    qb, k, v_ext, b_col, i_col, b_row, i_row, b_last, masks = [], [], [], [], [], [], [], [], []
    for d, h in items:
        q_ref, k_ref, v_ref, g, gt, lft, cum_col, cum_row, cos, sin, mask = per_dir[d]
        qb.append(_rope(q_ref[:, h * DK_M:(h + 1) * DK_M], cos, sin, even).astype(BF16))
        k.append(_rope(k_ref[:, h * DK_M:(h + 1) * DK_M], cos, sin, even) * (DK_M ** -0.5))
        v_ext.append(jnp.concatenate([v_ref[:, h * DV_M:(h + 1) * DV_M].astype(BF16), ones_col], axis=1))
        b_col.append(cum_col[:, H_M + h:H_M + h + 1])
        i_col.append(g[:, h:h + 1])
        b_row.append(cum_row[H_M + h:H_M + h + 1, :])
        i_row.append(gt[h:h + 1, :])
        b_last.append(jnp.sum(lft[H_M + h:H_M + h + 1, :], axis=1, keepdims=True))
        masks.append(mask)
    m_old = [m_refs[h][0:1, 0:1] for h in hr]
    c_old = [c_refs[h][...] for h in hr]
    m_new = [jnp.maximum(b_last[h] + m_old[h], jnp.max(b_last[h] - b_row[h] + i_row[h], axis=1, keepdims=True))
             for h in hr]
    w_end = [jnp.exp(b_last[h] - b_col[h] + i_col[h] - m_new[h]) for h in hr]
    decay = [jnp.exp(b_last[h] + m_old[h] - m_new[h]) for h in hr]
    kw = [(k[h] * w_end[h]).astype(BF16) for h in hr]
    kv = [lax.dot_general(kw[h], v_ext[h], TN_DIMS, preferred_element_type=F32) for h in hr]
    d_mat = [jnp.where(masks[h], b_col[h] - b_row[h] + i_row[h], NEG) for h in hr]
    m_in = [b_col[h] + m_old[h] for h in hr]
    m_t = [jnp.maximum(m_in[h], jnp.max(d_mat[h], axis=1, keepdims=True)) for h in hr]
    qk = [lax.dot_general(qb[h], k[h].astype(BF16), NT_DIMS, preferred_element_type=F32) for h in hr]
    s = [(qk[h] * jnp.exp(d_mat[h] - m_t[h])).astype(BF16) for h in hr]
    a_in = [jnp.exp(m_in[h] - m_t[h]) for h in hr]
    q_c = [jnp.dot(qb[h], c_old[h].astype(BF16), preferred_element_type=F32) for h in hr]
    numden = [jnp.dot(s[h], v_ext[h], preferred_element_type=F32) + a_in[h] * q_c[h] for h in hr]
    for n, (d, h) in enumerate(items):
        c_refs[n][...] = decay[n] * c_old[n] + kv[n]
        m_refs[n][...] = jnp.broadcast_to(m_new[n], m_refs[n].shape)
        den = numden[n][:, DV_M:DV_M + 1]
        o_refs[d][:, h * DV_M:(h + 1) * DV_M] = (numden[n][:, :DV_M]
                                                 / jnp.maximum(jnp.abs(den), jnp.exp(-m_t[n])))


def mlstm_scan(p, g_dir, gt_dir, cos_t, sin_t, dims):
    ncx = dims.ctx // CHUNK
    nlx = dims.seq // CHUNK
    x_blocks = dims.m_x // CHUNK

    def row_block(b, d, t):
        in_ctx = t < ncx
        cc = jnp.where(d == 0, t, ncx - 1 - t)
        cx = jnp.where(d == 0, t - ncx, nlx - 1 - (t - ncx))
        return jnp.where(in_ctx, x_blocks + b * ncx + cc, b * nlx + cx)

    def rope_block(b, d, t):
        in_ctx = t < ncx
        cx = jnp.where(d == 0, t - ncx, nlx - 1 - (t - ncx))
        return jnp.where(in_ctx, nlx, cx)

    def dir_specs(d):
        return [
            pl.BlockSpec((CHUNK, H_M * DK_M), lambda b, t: (row_block(b, d, t), PM_Q // (H_M * DK_M))),
            pl.BlockSpec((CHUNK, H_M * DK_M), lambda b, t: (row_block(b, d, t), PM_K // (H_M * DK_M))),
            pl.BlockSpec((CHUNK, H_M * DV_M), lambda b, t: (row_block(b, d, t), PM_V // (H_M * DV_M))),
            pl.BlockSpec((1, CHUNK, LANES), lambda b, t: (d, row_block(b, d, t), 0)),
            pl.BlockSpec((1, 8, CHUNK), lambda b, t: (d, 0, row_block(b, d, t))),
            pl.BlockSpec((CHUNK, DK_M), lambda b, t: (rope_block(b, d, t), 0)),
            pl.BlockSpec((CHUNK, DK_M), lambda b, t: (rope_block(b, d, t), 0)),
        ]

    h_shape = jax.ShapeDtypeStruct((dims.m_all, H_M * DV_M), F32)
    dir_args = (p, p, p, g_dir, gt_dir, cos_t, sin_t)
    return pl.pallas_call(
        _mlstm_kernel,
        out_shape=[h_shape, h_shape],
        grid=(dims.batch, ncx + nlx),
        in_specs=dir_specs(0) + dir_specs(1),
        out_specs=[pl.BlockSpec((CHUNK, H_M * DV_M), lambda b, t: (row_block(b, 0, t), 0)),
                   pl.BlockSpec((CHUNK, H_M * DV_M), lambda b, t: (row_block(b, 1, t), 0))],
        scratch_shapes=([pltpu.VMEM((DK_M, DV_EXT), F32)] * (2 * H_M)
                        + [pltpu.VMEM((8, LANES), F32)] * (2 * H_M)),
        compiler_params=_cparams(("arbitrary", "arbitrary")), name="mlstm_scan",
    )(*dir_args, *dir_args)


def _mlstm_out_kernel(hf_ref, hb_ref, o_ref, g_ref, out_ref):
    for h in range(H_M):
        sl = slice(h * DV_M, (h + 1) * DV_M)
        y = _rms(hf_ref[:, sl] + hb_ref[:, sl], g_ref[:, sl])
        out_ref[:, sl] = (y * jax.nn.sigmoid(o_ref[:, sl])).astype(out_ref.dtype)


def mlstm_out(hf, hb, p, g, dims):
    m = dims.m_all
    tm = 256
    n = H_M * DV_M
    return pl.pallas_call(
        _mlstm_out_kernel,
        out_shape=jax.ShapeDtypeStruct((m, n), BF16),
        grid=(m // tm,),
        in_specs=[pl.BlockSpec((tm, n), lambda i: (i, 0)),
                  pl.BlockSpec((tm, n), lambda i: (i, 0)),
                  pl.BlockSpec((tm, n), lambda i: (i, PM_O // n)),
                  pl.BlockSpec((1, n), lambda i: (0, 0))],
        out_specs=pl.BlockSpec((tm, n), lambda i: (i, 0)),
        compiler_params=_cparams(("arbitrary",)), name="mlstm_out",
    )(hf, hb, p, g.reshape(1, n))


def rope_tables(dims):
    t = jnp.arange(dims.seq)
    row = (t // GRID_W).astype(F32)
    col = (t % GRID_W).astype(F32)
    quarter = DK_M // 4
    inv = ROPE_BASE ** (-jnp.arange(quarter, dtype=F32) / quarter)
    ang = jnp.concatenate([row[:, None] * inv, col[:, None] * inv], axis=-1)
    cos = jnp.repeat(jnp.cos(ang), 2, axis=-1)
    sin = jnp.repeat(jnp.sin(ang), 2, axis=-1) * jnp.tile(jnp.array([-1.0, 1.0], F32), DK_M // 2)
    cos = jnp.concatenate([cos, jnp.ones((CHUNK, DK_M), F32)], axis=0)
    sin = jnp.concatenate([sin, jnp.zeros((CHUNK, DK_M), F32)], axis=0)
    return cos, sin


def _gather_kernel(src_ref, nrows_ref, h_hbm, o_ref, buf_ref, sem, *, tm):
    i = pl.program_id(0)
    n = nrows_ref[i]

    @pl.when(n < tm)
    def _():
        buf_ref[...] = jnp.zeros(buf_ref.shape, buf_ref.dtype)

    def start(r, carry):
        pltpu.make_async_copy(h_hbm.at[pl.ds(src_ref[i * tm + r], 1)], buf_ref.at[pl.ds(r, 1)], sem).start()
        return carry

    lax.fori_loop(0, n, start, 0)

    def wait(r, carry):
        pltpu.make_async_copy(h_hbm.at[pl.ds(0, 1)], buf_ref.at[pl.ds(r, 1)], sem).wait()
        return carry

    @pl.when(n == tm)
    def _():
        pltpu.make_async_copy(h_hbm.at[pl.ds(0, tm)], buf_ref, sem).wait()

    @pl.when(n < tm)
    def _():
        lax.fori_loop(0, n, wait, 0)

    o_ref[...] = buf_ref[...].astype(o_ref.dtype)


def gather_rows(h, src, tile_rows, *, tm):
    m, d = h.shape
    r_total = src.shape[0]
    return pl.pallas_call(
        functools.partial(_gather_kernel, tm=tm),
        out_shape=jax.ShapeDtypeStruct((r_total, d), BF16),
        grid_spec=pltpu.PrefetchScalarGridSpec(
            num_scalar_prefetch=2,
            grid=(r_total // tm,),
            in_specs=[pl.BlockSpec(memory_space=pl.ANY)],
            out_specs=pl.BlockSpec((tm, d), lambda i, *s: (i, 0)),
            scratch_shapes=[pltpu.VMEM((tm, d), F32), pltpu.SemaphoreType.DMA],
        ),
        compiler_params=_cparams(("arbitrary",)), name="moe_gather",
    )(src, tile_rows, h)


def _combine_kernel(p1_ref, p2_ref, y_hbm, x_ref, gate_ref, rt_ref, o_ref, y1_ref, y2_ref, sem,
                    *, tm, seq, n_groups):
    i = pl.program_id(0)

    def start(r, carry):
        t = i * tm + r
        pltpu.make_async_copy(y_hbm.at[pl.ds(p1_ref[t], 1)], y1_ref.at[pl.ds(r, 1)], sem.at[0]).start()
        pltpu.make_async_copy(y_hbm.at[pl.ds(p2_ref[t], 1)], y2_ref.at[pl.ds(r, 1)], sem.at[1]).start()
        return carry

    lax.fori_loop(0, tm, start, 0)

    pltpu.make_async_copy(y_hbm.at[pl.ds(0, tm)], y1_ref, sem.at[0]).wait()
    pltpu.make_async_copy(y_hbm.at[pl.ds(0, tm)], y2_ref, sem.at[1]).wait()
    row0 = i * tm
    grp = jnp.int32(n_groups - 1)
    for b in reversed(range(n_groups - 1)):
        grp = jnp.where(row0 < (b + 1) * seq, b, grp)
    gate = gate_ref[pl.ds(grp, 1), :]
    w1 = rt_ref[:, 2:3]
    w2 = rt_ref[:, 3:4]
    o_ref[...] = x_ref[...] + gate * (w1 * y1_ref[...] + w2 * y2_ref[...])


def moe_combine(xs, y, pos1, pos2, route_t, mod, layer, gate_chunk, dims, *, m):
    d = dims.d_model
    tm = _pick_tile(math.gcd(dims.seq, dims.ctx), (256, 128))
    kern = functools.partial(_combine_kernel, tm=tm, seq=dims.seq, n_groups=dims.batch + 1)
    return pl.pallas_call(
        kern,
        out_shape=jax.ShapeDtypeStruct((m, d), F32),
        grid_spec=pltpu.PrefetchScalarGridSpec(
            num_scalar_prefetch=2,
            grid=(m // tm,),
            in_specs=[pl.BlockSpec(memory_space=pl.ANY),
                      pl.BlockSpec((tm, d), lambda i, a, b: (i, 0)),
                      pl.BlockSpec((8, d), lambda i, a, b: (MOD_ROWS // 8 * layer, gate_chunk)),
                      pl.BlockSpec((tm, 8), lambda i, a, b: (i, 0))],
            out_specs=pl.BlockSpec((tm, d), lambda i, a, b: (i, 0)),
            scratch_shapes=[pltpu.VMEM((tm, d), F32), pltpu.VMEM((tm, d), F32),
                            pltpu.SemaphoreType.DMA((2,))],
        ),
        compiler_params=_cparams(("arbitrary",)), name="moe_combine",
    )(pos1, pos2, y, xs, mod, route_t)


def moe_dispatch_plan(route, m, tm):
    e = jnp.concatenate([route[0, :m], route[1, :m]]).astype(I32)
    tok = jnp.tile(jnp.arange(m, dtype=I32), 2)
    onehot = (e[:, None] == jnp.arange(N_EXPERTS, dtype=I32)[None, :]).astype(I32)
    rank = jnp.sum((jnp.cumsum(onehot, axis=0) - onehot) * onehot, axis=1)
    counts = jnp.sum(onehot, axis=0)
    padded = ((counts + tm - 1) // tm) * tm
    ends = jnp.cumsum(padded)
    offs = ends - padded
    pos = offs[e] + rank
    r_total = TOP_K * m + N_EXPERTS * tm
    src = jnp.zeros((r_total,), I32).at[pos].set(tok)
    tile_start = jnp.arange(r_total // tm, dtype=I32) * tm
    last_e = jnp.max(jnp.where(counts > 0, jnp.arange(N_EXPERTS, dtype=I32), 0))
    tile_e = jnp.minimum(jnp.sum((tile_start[:, None] >= ends[None, :]).astype(I32), axis=1), last_e)
    tile_rows = jnp.clip((offs + counts)[tile_e] - tile_start, 0, tm).astype(I32)
    return src, pos[:m], pos[m:], tile_e, tile_rows


def _dense_tile(m, candidates):
    return _pick_tile(m, candidates)


def kernel(x, c, ctx, c_ctx, ada_w, ada_b, norm1_g, norm2_g, w_in, gate_b, na_q_g, na_k_g, na_rpb, m_norm_g, w_out, ffn_w_gate, ffn_w_up, ffn_w_down, moe_router, moe_w_gate, moe_w_up, moe_w_down):
    batch, seq, d = x.shape
    dims = Dims(batch=batch, seq=seq, ctx=ctx.shape[1], d_model=d, d_ff=ffn_w_gate.shape[2], depth=ada_w.shape[0])
    depth = dims.depth
    n_groups = batch + 1
    rows = seq // GRID_W
    assert rows >= 4 * NA_QROWS and rows % NA_QROWS == 0 and batch + 1 <= 8
    assert seq % 256 == 0 and dims.ctx % CHUNK == 0 and dims.m_x % dims.ctx == 0

    xs = jnp.concatenate([x.reshape(dims.m_x, d), ctx.reshape(batch * dims.ctx, d)], axis=0)

    cond = jnp.concatenate([jax.nn.silu(c), jax.nn.silu(c_ctx)[None], jnp.zeros((MOD_ROWS - n_groups, d), F32)], axis=0)
    tn_ada = _pick_tile(6 * d, (1024, 512, 256))
    tn_d = min(512, d)
    mod = gmm((cond.astype(BF16),), (ada_w,), jnp.arange(depth, dtype=I32), jnp.full((depth,), MOD_ROWS, I32),
              tm=MOD_ROWS, tn=tn_ada, n_out=6 * d, out_dtype=F32, epilogue="bias",
              extras=(ada_b.reshape(depth, 1, 6 * d),),
              extra_specs=(((1, 1, tn_ada), lambda j, i, g, *s: (g[i], 0, j)),),
              a_fixed=True, rows_outer=True, name="ada_mod")

    cos_t, sin_t = rope_tables(dims)
    na_bias = na_bias_table(na_rpb, rows)
    w_gate_cols = jnp.pad(w_in[:, :, N_MAIN:], ((0, 0), (0, 0), (0, LANES - 4 * H_M)))
    gate_b_pad = jnp.pad(gate_b, ((0, 0), (0, LANES - 4 * H_M)))

    def dense_groups(m, tm, g):
        nt = m // tm
        return jnp.full((nt,), g, I32), jnp.full((nt,), tm, I32)

    for layer in range(depth):
        last = layer == depth - 1
        m_out = dims.m_x if last else dims.m_all

        hbf = norm_mod(xs, norm1_g[layer], mod, layer, 0, 1, dims, m=dims.m_all)
        qk, vbf, pm, gates = in_proj(hbf, w_in, w_gate_cols, gate_b_pad, layer, na_q_g[layer], na_k_g[layer])
        g_dir = jnp.stack([gates, jnp.roll(gates, -2 * H_M, axis=1)])
        gt_dir = jnp.stack([gates[:, :2 * H_M].T, gates[:, 2 * H_M:4 * H_M].T])

        na_out = na_attention(qk, vbf, na_bias, layer, dims)
        hf, hb = mlstm_scan(pm, g_dir, gt_dir, cos_t, sin_t, dims)
        m_mix = mlstm_out(hf, hb, pm, m_norm_g[layer], dims)
        if not last:
            na_out = jnp.concatenate([na_out, ctx_attention(qk, vbf, dims)], axis=0)

        tm_o = _dense_tile(m_out, (1088, 1024, 640, 512, 256))
        grp, nv = dense_groups(m_out, tm_o, layer)
        res_specs = lambda tm, tn, chunk: (
            ((tm, tn), lambda j, i, *s: (i, j)),
            ((8, tn), lambda j, i, *s: (MOD_ROWS // 8 * layer, chunk * (d // tn) + j)))
        tn_o = min(1024, d)
        xs = gmm((na_out, m_mix), (w_out,), grp, nv, tm=tm_o, tn=tn_o, n_out=d, out_dtype=F32, epilogue="residual",
                 extras=(xs, mod), extra_specs=res_specs(tm_o, tn_o, 2), seq=seq, n_groups=n_groups,
                 name="w_out_proj")

        idx = layer // 2
        if layer % 2 == 0:
            h2 = norm_mod(xs, norm2_g[layer], mod, layer, 3, 4, dims, m=m_out)
            tm_u = _dense_tile(m_out, (1088, 1024, 640, 512, 256))
            grp, nv = dense_groups(m_out, tm_u, idx)
            act = gmm_swiglu(h2, ffn_w_gate, ffn_w_up, grp, nv, tm=tm_u, tn=512, name="ffn_up")
            tm_d = _dense_tile(m_out, (544, 512, 320, 256))
            grp, nv = dense_groups(m_out, tm_d, idx)
            xs = gmm((act,), (ffn_w_down,), grp, nv, tm=tm_d, tn=tn_d, n_out=d, out_dtype=F32, epilogue="residual",
                     extras=(xs, mod), extra_specs=res_specs(tm_d, tn_d, 5), seq=seq, n_groups=n_groups,
                     name="ffn_down")
        else:
            h2, route = norm_mod(xs, norm2_g[layer], mod, layer, 3, 4, dims, m=m_out,
                                 router_t=moe_router[idx].T)
            tm_e = 512 if m_out >= 4096 else 128
            src, pos1, pos2, tile_e, tile_rows = moe_dispatch_plan(route, m_out, tm_e)
            hs = gather_rows(h2, src, tile_rows, tm=tm_e)
            grp = tile_e + idx * N_EXPERTS
            nw = moe_w_gate.shape[0] * N_EXPERTS
            act = gmm_swiglu(hs, moe_w_gate.reshape(nw, d, dims.d_ff), moe_w_up.reshape(nw, d, dims.d_ff),
                             grp, tile_rows, tm=tm_e, tn=512, half_tiles=True, name="moe_up")
            y = gmm((act,), (moe_w_down.reshape(nw, dims.d_ff, d),), grp, tile_rows, tm=tm_e, tn=tn_d, n_out=d,
                    out_dtype=F32, half_tiles=True, name="moe_down")
            xs = moe_combine(xs, y, pos1, pos2, route.T, mod, layer, 5, dims, m=m_out)

    return xs[:dims.m_x].reshape(batch, seq, d)
```

```python
import functools
import math
from typing import NamedTuple

import jax
import jax.numpy as jnp
import numpy as np
from jax import lax
from jax.experimental import pallas as pl
from jax.experimental.pallas import tpu as pltpu

F32 = jnp.float32
BF16 = jnp.bfloat16
I32 = jnp.int32

GRID_W = 64
H_NA = 8
DH_NA = 128
WIN_R = 8
WIN_C = 16
H_M = 4
DK_M = 128
DV_M = 256
CHUNK = 128
ROPE_BASE = 10000.0
N_EXPERTS = 8
TOP_K = 2
EPS = 1e-6
D_MIX = 2048
NEG = -1e30

OFF_MQ = 3 * H_NA * DH_NA
N_MAIN = 6144
PM_Q, PM_K, PM_V, PM_O = 0, H_M * DK_M, 2 * H_M * DK_M, 2 * H_M * DK_M + H_M * DV_M
DV_EXT = DV_M + 128

LANES = 128
VMEM_LIMIT = 56 * 1024 * 1024
MOD_ROWS = 16


class Dims(NamedTuple):
    batch: int
    seq: int
    ctx: int
    d_model: int
    d_ff: int
    depth: int

    @property
    def m_x(self):
        return self.batch * self.seq

    @property
    def m_all(self):
        return self.batch * (self.seq + self.ctx)


def _cparams(sem):
    return pltpu.CompilerParams(dimension_semantics=sem, vmem_limit_bytes=VMEM_LIMIT)


def _group_select(row, vals, seq):
    out = vals[-1]
    for b in reversed(range(len(vals) - 1)):
        out = jnp.where(row < (b + 1) * seq, vals[b], out)
    return out


def _pick_tile(m, candidates):
    for t in candidates:
        if m % t == 0:
            return t
    raise ValueError(f"no row tile for {m}")


N_EXTRA = {"residual": 2, "bias": 1, "swiglu": 0, None: 0}


def _gmm_kernel(grp_ref, rows_ref, rid_ref, rgrp_ref, nruns_ref, *refs,
                n_a, n_w, epilogue, tm, tn, nj, seq, n_groups, rows_outer, half_tiles):
    a_refs, refs = refs[:n_a], refs[n_a:]
    w_refs, refs = refs[:n_w], refs[n_w:]
    x_refs, refs = refs[:N_EXTRA[epilogue]], refs[N_EXTRA[epilogue]:]
    o_ref, refs = refs[0], refs[1:]
    wbf_refs, refs = refs[:n_w], refs[n_w:]
    if rows_outer:
        i = pl.program_id(0)
        for w_ref, wbf_ref in zip(w_refs, wbf_refs):
            wbf_ref[...] = w_ref[0].astype(BF16)
    else:
        wst_refs, sem = refs[:n_w], refs[n_w]
        j = pl.program_id(0)
        i = pl.program_id(1)
        rid = rid_ref[i]
        nruns = nruns_ref[0]

        def tile_copy(widx, g, jj, slot):
            src = w_refs[widx].at[g, :, pl.ds(pl.multiple_of(jj * tn, tn), tn)]
            return pltpu.make_async_copy(src, wst_refs[widx].at[slot], sem.at[widx, slot])

        @pl.when((i == 0) | (rid != rid_ref[jnp.maximum(i - 1, 0)]))
        def _():
            q = j * nruns + rid
            slot = q % 2

            @pl.when(q == 0)
            def _():
                for widx in range(n_w):
                    tile_copy(widx, rgrp_ref[0], 0, 0).start()

            for widx in range(n_w):
                tile_copy(widx, 0, 0, slot).wait()
            wrap = rid + 1 == nruns
            next_run = jnp.where(wrap, 0, rid + 1)
            next_j = j + wrap.astype(I32)

            @pl.when(next_j < nj)
            def _():
                for widx in range(n_w):
                    tile_copy(widx, rgrp_ref[next_run], next_j, 1 - slot).start()

            for widx in range(n_w):
                wbf_refs[widx][...] = wst_refs[widx][slot].astype(BF16)

    n_rows = rows_ref[i]

    def compute(nr):
        def matmul(wbf_ref):
            out, k0 = None, 0
            for a_ref in a_refs:
                kp = a_ref.shape[1]
                part = jnp.dot(a_ref[0:nr, :], wbf_ref[k0:k0 + kp, :], preferred_element_type=F32)
                out = part if out is None else out + part
                k0 += kp
            return out

        acc = matmul(wbf_refs[0])
        if epilogue == "residual":
            res_ref, gate_ref = x_refs
            row = i * tm + lax.broadcasted_iota(I32, (nr, 1), 0)
            gate = _group_select(row, [gate_ref[k:k + 1, :] for k in range(n_groups)], seq)
            acc = res_ref[0:nr, :] + gate * acc
        elif epilogue == "bias":
            acc = acc + x_refs[0][0]
        elif epilogue == "swiglu":
            acc = acc * jax.nn.sigmoid(acc) * matmul(wbf_refs[1])
        o_ref[0:nr, :] = acc.astype(o_ref.dtype)
        if nr < tm:
            o_ref[nr:, :] = jnp.zeros((tm - nr, o_ref.shape[1]), o_ref.dtype)

    steps = [tm * q // 4 for q in (1, 2, 3, 4)] if half_tiles else [tm]
    lo = 0
    for nr in steps:
        pl.when((n_rows > lo) & (n_rows <= nr))(functools.partial(compute, nr))
        lo = nr

    @pl.when(n_rows == 0)
    def _():
        o_ref[...] = jnp.zeros(o_ref.shape, o_ref.dtype)


def _runs(grp):
    nt = grp.shape[0]
    first = jnp.concatenate([jnp.ones((1,), I32), (grp[1:] != grp[:-1]).astype(I32)])
    run_id = jnp.cumsum(first) - 1
    sel = (run_id[:, None] == jnp.arange(nt, dtype=I32)[None, :]).astype(I32) * first[:, None]
    run_grp = jnp.sum(sel * grp[:, None], axis=0)
    return run_id.astype(I32), run_grp.astype(I32), (run_id[-1:] + 1).astype(I32)


def gmm(a_parts, ws, grp, tile_rows, *, tm, tn, n_out, out_dtype, epilogue=None, extras=(), extra_specs=(),
        a_fixed=False, seq=0, n_groups=0, rows_outer=False, half_tiles=False, name="gmm"):
    k = sum(a.shape[1] for a in a_parts)
    nt = grp.shape[0]
    n_w = len(ws)
    nj = n_out // tn

    def spec(shape, fn):
        return pl.BlockSpec(shape, (lambda i, j, *s: fn(j, i, *s)) if rows_outer else fn)

    a_map = (lambda j, i, *s: (0, 0)) if a_fixed else (lambda j, i, *s: (i, 0))
    if rows_outer:
        w_specs = [spec((1, k, tn), lambda j, i, g, *s: (g[i], 0, j))] * n_w
        stream_scratch = []
    else:
        w_specs = [pl.BlockSpec(memory_space=pl.ANY)] * n_w
        stream_scratch = [pltpu.VMEM((2, k, tn), F32)] * n_w + [pltpu.SemaphoreType.DMA((n_w, 2))]
    in_specs = ([spec((tm, a.shape[1]), a_map) for a in a_parts] + w_specs
                + [spec(shape, fn) for shape, fn in extra_specs])
    kern = functools.partial(_gmm_kernel, n_a=len(a_parts), n_w=n_w, epilogue=epilogue, tm=tm, tn=tn, nj=nj,
                             seq=seq, n_groups=n_groups, rows_outer=rows_outer, half_tiles=half_tiles)
    run_id, run_grp, nruns = _runs(grp)
    return pl.pallas_call(
        kern,
        out_shape=jax.ShapeDtypeStruct((nt * tm, n_out), out_dtype),
        grid_spec=pltpu.PrefetchScalarGridSpec(
            num_scalar_prefetch=5,
            grid=(nt, nj) if rows_outer else (nj, nt),
            in_specs=in_specs,
            out_specs=spec((tm, tn), lambda j, i, *s: (i, j)),
            scratch_shapes=[pltpu.VMEM((k, tn), BF16)] * n_w + stream_scratch,
        ),
        compiler_params=_cparams(("arbitrary", "arbitrary")),
        name=name,
    )(grp, tile_rows, run_id, run_grp, nruns, *a_parts, *ws, *extras)


def gmm_swiglu(a, wg, wu, grp, tile_rows, *, tm, tn, half_tiles=False, name="gmm_swiglu"):
    return gmm((a,), (wg, wu), grp, tile_rows, tm=tm, tn=tn, n_out=wg.shape[2], out_dtype=BF16,
               epilogue="swiglu", half_tiles=half_tiles, name=name)


def _rms(x, g):
    return x * lax.rsqrt(jnp.mean(x * x, axis=-1, keepdims=True) + EPS) * g


def _split_bf16(x):
    hi = x.astype(BF16)
    lo = (x - hi.astype(F32)).astype(BF16)
    return hi, lo


def _norm_mod_kernel(x_ref, g_ref, shift_ref, scale_ref, *refs, tm, seq, n_groups, with_router):
    if with_router:
        r_ref, o_ref, route_ref = refs
    else:
        (o_ref,) = refs
    i = pl.program_id(0)
    row0 = i * tm
    grp = jnp.int32(n_groups - 1)
    for b in reversed(range(n_groups - 1)):
        grp = jnp.where(row0 < (b + 1) * seq, b, grp)
    y = _rms(x_ref[...], g_ref[...])
    h = y * (1.0 + scale_ref[pl.ds(grp, 1), :]) + shift_ref[pl.ds(grp, 1), :]
    o_ref[...] = h.astype(o_ref.dtype)
    if with_router:
        nt_dims = (((1,), (1,)), ((), ()))
        h_hi, h_lo = _split_bf16(h)
        r_hi, r_lo = _split_bf16(r_ref[...])
        logits = (lax.dot_general(r_hi, h_hi, nt_dims, preferred_element_type=F32)
                  + lax.dot_general(r_hi, h_lo, nt_dims, preferred_element_type=F32)
                  + lax.dot_general(r_lo, h_hi, nt_dims, preferred_element_type=F32))
        ids = lax.broadcasted_iota(I32, logits.shape, 0)
        m1 = jnp.max(logits, axis=0, keepdims=True)
        i1 = jnp.min(jnp.where(logits == m1, ids, N_EXPERTS), axis=0, keepdims=True)
        rest = jnp.where(ids == i1, -jnp.inf, logits)
        m2 = jnp.max(rest, axis=0, keepdims=True)
        i2 = jnp.min(jnp.where(rest == m2, ids, N_EXPERTS), axis=0, keepdims=True)
        e = jnp.exp(m2 - m1)
        w1 = 1.0 / (1.0 + e)
        w2 = e / (1.0 + e)
        out_row = lax.broadcasted_iota(I32, (8, tm), 0)
        route_ref[...] = jnp.where(out_row == 0, i1.astype(F32), jnp.where(
            out_row == 1, i2.astype(F32), jnp.where(out_row == 2, w1, jnp.where(out_row == 3, w2, 0.0))))


def norm_mod(xs, g, mod, layer, shift_chunk, scale_chunk, dims, *, m, router_t=None):
    d = dims.d_model
    tm = _pick_tile(math.gcd(dims.seq, dims.ctx), (256, 128))
    with_router = router_t is not None
    kern = functools.partial(_norm_mod_kernel, tm=tm, seq=dims.seq, n_groups=dims.batch + 1,
                             with_router=with_router)
    in_specs = [
        pl.BlockSpec((tm, d), lambda i: (i, 0)),
        pl.BlockSpec((1, d), lambda i: (0, 0)),
        pl.BlockSpec((8, d), lambda i: (MOD_ROWS // 8 * layer, shift_chunk)),
        pl.BlockSpec((8, d), lambda i: (MOD_ROWS // 8 * layer, scale_chunk)),
    ]
    args = [xs, g.reshape(1, d), mod, mod]
    out_shape = [jax.ShapeDtypeStruct((m, d), F32 if with_router else BF16)]
    out_specs = [pl.BlockSpec((tm, d), lambda i: (i, 0))]
    if with_router:
        in_specs.append(pl.BlockSpec((N_EXPERTS, d), lambda i: (0, 0)))
        args.append(router_t)
        out_shape.append(jax.ShapeDtypeStruct((8, m), F32))
        out_specs.append(pl.BlockSpec((8, tm), lambda i: (0, i)))
    res = pl.pallas_call(
        kern, out_shape=out_shape, grid=(m // tm,), in_specs=in_specs, out_specs=out_specs,
        compiler_params=_cparams(("arbitrary",)), name="norm_mod_router" if with_router else "norm_mod",
    )(*args)
    return res if with_router else res[0]


IN_TN = 512
J_QK = 2 * H_NA * DH_NA // IN_TN
J_V = H_NA * DH_NA // IN_TN
N_MLSTM = N_MAIN - OFF_MQ


def _in_proj_kernel(a_ref, w_ref, wgate_ref, gb_ref, qg_ref, kg_ref,
                    qk_ref, v_ref, pm_ref, gates_ref, wbf_ref):
    j = pl.program_id(1)

    @pl.when(j == 0)
    def _():
        gates_ref[...] = (jnp.dot(a_ref[...], wgate_ref[0].astype(BF16), preferred_element_type=F32)
                          + gb_ref[...])

    wbf_ref[...] = w_ref[0].astype(BF16)
    acc = jnp.dot(a_ref[...], wbf_ref[...], preferred_element_type=F32)

    @pl.when(j < J_QK)
    def _():
        g = jnp.where(j < J_QK // 2, qg_ref[...] * (DH_NA ** -0.5), kg_ref[...])
        for h in range(IN_TN // DH_NA):
            sl = slice(h * DH_NA, (h + 1) * DH_NA)
            qk_ref[:, sl] = _rms(acc[:, sl], g).astype(BF16)

    @pl.when((j >= J_QK) & (j < J_QK + J_V))
    def _():
        v_ref[...] = acc.astype(BF16)

    @pl.when(j >= J_QK + J_V)
    def _():
        pm_ref[...] = acc


def in_proj(hbf, w_in, w_gate_cols, gate_b_pad, layer, qg, kg):
    m, d = hbf.shape
    tm = _pick_tile(m, (2176, 1088, 640, 512, 256))
    nj = N_MAIN // IN_TN
    return pl.pallas_call(
        _in_proj_kernel,
        out_shape=[jax.ShapeDtypeStruct((m, 2 * H_NA * DH_NA), BF16),
                   jax.ShapeDtypeStruct((m, H_NA * DH_NA), BF16),
                   jax.ShapeDtypeStruct((m, N_MLSTM), F32),
                   jax.ShapeDtypeStruct((m, LANES), F32)],
        grid=(m // tm, nj),
        in_specs=[pl.BlockSpec((tm, d), lambda i, j: (i, 0)),
                  pl.BlockSpec((1, d, IN_TN), lambda i, j: (layer, 0, j)),
                  pl.BlockSpec((1, d, LANES), lambda i, j: (layer, 0, 0)),
                  pl.BlockSpec((None, 1, LANES), lambda i, j: (layer, 0, 0)),
                  pl.BlockSpec((1, DH_NA), lambda i, j: (0, 0)),
                  pl.BlockSpec((1, DH_NA), lambda i, j: (0, 0))],
        out_specs=[pl.BlockSpec((tm, IN_TN), lambda i, j: (i, jnp.minimum(j, J_QK - 1))),
                   pl.BlockSpec((tm, IN_TN), lambda i, j: (i, jnp.clip(j - J_QK, 0, J_V - 1))),
                   pl.BlockSpec((tm, IN_TN), lambda i, j: (i, jnp.maximum(j - (J_QK + J_V), 0))),
                   pl.BlockSpec((tm, LANES), lambda i, j: (i, 0))],
        scratch_shapes=[pltpu.VMEM((d, IN_TN), BF16)],
        compiler_params=_cparams(("arbitrary", "arbitrary")), name="in_proj",
    )(hbf, w_in, w_gate_cols, gate_b_pad.reshape(-1, 1, LANES), qg.reshape(1, DH_NA), kg.reshape(1, DH_NA))


NT_DIMS = (((1,), (1,)), ((), ()))
TN_DIMS = (((0,), (0,)), ((), ()))
NA_QROWS = 4
NA_HEADS = 4
NA_KROWS = NA_QROWS + WIN_R
N_BIAS_PATTERNS = 3


def _na_block_start(qb, rows):
    return jnp.clip(qb * NA_QROWS - WIN_R // 2, 0, rows - NA_KROWS)


def _na_kernel(q_ref, k_ref, v_ref, kc_ref, vc_ref, bias_ref, o_ref, *, rows):
    qb = pl.program_id(2)
    nblk = rows // NA_QROWS
    pid = jnp.where(qb == 0, 0, jnp.where(qb == nblk - 1, 2, 1))
    start = pl.multiple_of(_na_block_start(qb, rows) * GRID_W, GRID_W)
    hr = range(NA_HEADS)
    sl = [slice(h * DH_NA, (h + 1) * DH_NA) for h in hr]
    win = pl.ds(start, NA_KROWS * GRID_W)
    s = [lax.dot_general(q_ref[:, sl[h]], k_ref[win, sl[h]], NT_DIMS, preferred_element_type=F32)
         + bias_ref[h, pid] for h in hr]
    sc = [lax.dot_general(q_ref[:, sl[h]], kc_ref[:, sl[h]], NT_DIMS, preferred_element_type=F32) for h in hr]
    m = [jnp.maximum(jnp.max(s[h], axis=-1, keepdims=True), jnp.max(sc[h], axis=-1, keepdims=True)) for h in hr]
    pb = [jnp.exp(s[h] - m[h]) for h in hr]
    pc = [jnp.exp(sc[h] - m[h]) for h in hr]
    den = [jnp.sum(pb[h], axis=-1, keepdims=True) + jnp.sum(pc[h], axis=-1, keepdims=True) for h in hr]
    o = [jnp.dot(pb[h].astype(BF16), v_ref[win, sl[h]], preferred_element_type=F32)
         + jnp.dot(pc[h].astype(BF16), vc_ref[:, sl[h]], preferred_element_type=F32) for h in hr]
    for h in hr:
        o_ref[:, sl[h]] = (o[h] / den[h]).astype(o_ref.dtype)


N_ROW_OFFSETS = 2 * WIN_R - 1


def _na_bias_kernel(tz_ref, o_ref, *, block_of):
    for p in range(N_BIAS_PATTERNS):
        for x in range(NA_QROWS):
            for j in range(NA_KROWS):
                o_ref[p, x * GRID_W:(x + 1) * GRID_W, j * GRID_W:(j + 1) * GRID_W] = tz_ref[block_of[p][x][j]]


def na_bias_table(rpb, rows):
    nblk = rows // NA_QROWS
    qb = np.array([0, 2, nblk - 1])
    ks = np.clip(qb * NA_QROWS - WIN_R // 2, 0, rows - NA_KROWS)
    r = qb[:, None] * NA_QROWS + np.arange(NA_QROWS)[None, :]
    rs = np.clip(r - WIN_R // 2, 0, rows - WIN_R)
    kr = ks[:, None] + np.arange(NA_KROWS)[None, :]
    in_band = (kr[:, None, :] >= rs[:, :, None]) & (kr[:, None, :] < rs[:, :, None] + WIN_R)
    dr = kr[:, None, :] - r[:, :, None] + (WIN_R - 1)
    block_of = np.where(in_band, dr, N_ROW_OFFSETS).tolist()
    c = np.arange(GRID_W)
    dc = np.clip(c[None, :] - c[:, None], -(WIN_C - 1), WIN_C - 1) + (WIN_C - 1)
    cs = np.clip(c - WIN_C // 2, 0, GRID_W - WIN_C)
    col_ok = (c[None, :] >= cs[:, None]) & (c[None, :] < cs[:, None] + WIN_C)
    sel_c = jnp.asarray(dc[..., None] == np.arange(2 * WIN_C - 1), F32)
    tz = jnp.einsum("lhab,qkb->lhaqk", rpb.astype(F32), sel_c, precision=lax.Precision.HIGHEST)
    tz = jnp.where(jnp.asarray(col_ok), tz, NEG)
    n_l = rpb.shape[0]
    tz = jnp.concatenate([tz, jnp.full((n_l, H_NA, 1, GRID_W, GRID_W), NEG, F32)], axis=2)
    tq, tk = NA_QROWS * GRID_W, NA_KROWS * GRID_W
    return pl.pallas_call(
        functools.partial(_na_bias_kernel, block_of=block_of),
        out_shape=jax.ShapeDtypeStruct((n_l, H_NA, N_BIAS_PATTERNS, tq, tk), F32),
        grid=(n_l, H_NA),
        in_specs=[pl.BlockSpec((None, None, N_ROW_OFFSETS + 1, GRID_W, GRID_W), lambda l, h: (l, h, 0, 0, 0))],
        out_specs=pl.BlockSpec((None, None, N_BIAS_PATTERNS, tq, tk), lambda l, h: (l, h, 0, 0, 0)),
        compiler_params=_cparams(("arbitrary", "arbitrary")), name="na_bias",
    )(tz)


def na_attention(qk, vbf, bias, layer, dims):
    rows = dims.seq // GRID_W
    qblocks = rows // NA_QROWS
    tq = NA_QROWS * GRID_W
    ctx_blk0 = dims.m_x // dims.ctx
    hgroups = H_NA // NA_HEADS
    hw = NA_HEADS * DH_NA
    kern = functools.partial(_na_kernel, rows=rows)
    return pl.pallas_call(
        kern,
        out_shape=jax.ShapeDtypeStruct((dims.m_x, H_NA * DH_NA), BF16),
        grid=(dims.batch, hgroups, qblocks),
        in_specs=[
            pl.BlockSpec((tq, hw), lambda b, h, i: (b * qblocks + i, h)),
            pl.BlockSpec((dims.seq, hw), lambda b, h, i: (b, hgroups + h)),
            pl.BlockSpec((dims.seq, hw), lambda b, h, i: (b, h)),
            pl.BlockSpec((dims.ctx, hw), lambda b, h, i: (ctx_blk0 + b, hgroups + h)),
            pl.BlockSpec((dims.ctx, hw), lambda b, h, i: (ctx_blk0 + b, h)),
            pl.BlockSpec((None, NA_HEADS, N_BIAS_PATTERNS, tq, NA_KROWS * GRID_W),
                         lambda b, h, i: (layer, h, 0, 0, 0)),
        ],
        out_specs=pl.BlockSpec((tq, hw), lambda b, h, i: (b * qblocks + i, h)),
        compiler_params=_cparams(("arbitrary", "arbitrary", "arbitrary")), name="na_attention",
    )(qk, qk, vbf, qk, vbf, bias)


def _ctx_attn_kernel(q_ref, k_ref, v_ref, o_ref):
    s = lax.dot_general(q_ref[...], k_ref[...], NT_DIMS, preferred_element_type=F32)
    m = jnp.max(s, axis=-1, keepdims=True)
    p = jnp.exp(s - m)
    den = jnp.sum(p, axis=-1, keepdims=True)
    o = jnp.dot(p.astype(BF16), v_ref[...], preferred_element_type=F32)
    o_ref[...] = (o / den).astype(o_ref.dtype)


def ctx_attention(qk, vbf, dims):
    ctx_blk0 = dims.m_x // dims.ctx
    return pl.pallas_call(
        _ctx_attn_kernel,
        out_shape=jax.ShapeDtypeStruct((dims.batch * dims.ctx, H_NA * DH_NA), BF16),
        grid=(dims.batch, H_NA),
        in_specs=[
            pl.BlockSpec((dims.ctx, DH_NA), lambda b, h: (ctx_blk0 + b, h)),
            pl.BlockSpec((dims.ctx, DH_NA), lambda b, h: (ctx_blk0 + b, H_NA + h)),
            pl.BlockSpec((dims.ctx, DH_NA), lambda b, h: (ctx_blk0 + b, h)),
        ],
        out_specs=pl.BlockSpec((dims.ctx, DH_NA), lambda b, h: (b, h)),
        compiler_params=_cparams(("arbitrary", "arbitrary")), name="ctx_attention",
    )(qk, qk, vbf)


def _log_sigmoid(x):
    return jnp.minimum(x, 0.0) - jnp.log(1.0 + jnp.exp(-jnp.abs(x)))


def _rope(t, cos, sin_signed, even):
    swapped = jnp.where(even, pltpu.roll(t, DK_M - 1, 1), pltpu.roll(t, 1, 1))
    return t * cos + swapped * sin_signed


def _mlstm_kernel(*refs):
    n_in = 7
    state = refs[2 * n_in + 2:]
    c_refs, m_refs = state[:2 * H_M], state[2 * H_M:]

    @pl.when(pl.program_id(1) == 0)
    def _():
        for ref in state:
            ref[...] = jnp.zeros(ref.shape, F32)

    _mlstm_chunks(refs[:n_in], refs[n_in:2 * n_in], refs[2 * n_in:2 * n_in + 2], c_refs, m_refs)


def _mlstm_chunks(fwd_refs, bwd_refs, o_refs, c_refs, m_refs):
    ti = lax.broadcasted_iota(I32, (CHUNK, CHUNK), 0)
    si = lax.broadcasted_iota(I32, (CHUNK, CHUNK), 1)
    even = (lax.broadcasted_iota(I32, (CHUNK, DK_M), 1) % 2) == 0
    ones_col = (lax.broadcasted_iota(I32, (CHUNK, DV_EXT - DV_M), 1) == 0).astype(BF16)
    per_dir = []
    for direction, (q_ref, k_ref, v_ref, g_ref, gt_ref, cos_ref, sin_ref) in enumerate((fwd_refs, bwd_refs)):
        mask = (si <= ti) if direction == 0 else (si >= ti)
        tri = mask.astype(BF16)
        g = g_ref[0]
        gt = gt_ref[0]
        lf_hi, lf_lo = _split_bf16(_log_sigmoid(g))
        cum_col = (jnp.dot(tri, lf_hi, preferred_element_type=F32)
                   + jnp.dot(tri, lf_lo, preferred_element_type=F32))
        lft = _log_sigmoid(gt)
        lft_hi, lft_lo = _split_bf16(lft)
        cum_row = (lax.dot_general(lft_hi, tri, NT_DIMS, preferred_element_type=F32)
                   + lax.dot_general(lft_lo, tri, NT_DIMS, preferred_element_type=F32))
        per_dir.append((q_ref, k_ref, v_ref, g, gt, lft, cum_col, cum_row, cos_ref[...], sin_ref[...], mask))
    items = [(d, h) for d in range(2) for h in range(H_M)]
    hr = range(len(items))
    qb, k, v_ext, b_col, i_col, b_row, i_row, b_last, masks = [], [], [], [], [], [], [], [], []
    for d, h in items:
        q_ref, k_ref, v_ref, g, gt, lft, cum_col, cum_row, cos, sin, mask = per_dir[d]
        qb.append(_rope(q_ref[:, h * DK_M:(h + 1) * DK_M], cos, sin, even).astype(BF16))
        k.append(_rope(k_ref[:, h * DK_M:(h + 1) * DK_M], cos, sin, even) * (DK_M ** -0.5))
        v_ext.append(jnp.concatenate([v_ref[:, h * DV_M:(h + 1) * DV_M].astype(BF16), ones_col], axis=1))
        b_col.append(cum_col[:, H_M + h:H_M + h + 1])
        i_col.append(g[:, h:h + 1])
        b_row.append(cum_row[H_M + h:H_M + h + 1, :])
        i_row.append(gt[h:h + 1, :])
        b_last.append(jnp.sum(lft[H_M + h:H_M + h + 1, :], axis=1, keepdims=True))
        masks.append(mask)
    m_old = [m_refs[h][0:1, 0:1] for h in hr]
    c_old = [c_refs[h][...] for h in hr]
    m_new = [jnp.maximum(b_last[h] + m_old[h], jnp.max(b_last[h] - b_row[h] + i_row[h], axis=1, keepdims=True))
             for h in hr]
    w_end = [jnp.exp(b_last[h] - b_col[h] + i_col[h] - m_new[h]) for h in hr]
    decay = [jnp.exp(b_last[h] + m_old[h] - m_new[h]) for h in hr]
    kw = [(k[h] * w_end[h]).astype(BF16) for h in hr]
    kv = [lax.dot_general(kw[h], v_ext[h], TN_DIMS, preferred_element_type=F32) for h in hr]
    d_mat = [jnp.where(masks[h], b_col[h] - b_row[h] + i_row[h], NEG) for h in hr]
    m_in = [b_col[h] + m_old[h] for h in hr]
    m_t = [jnp.maximum(m_in[h], jnp.max(d_mat[h], axis=1, keepdims=True)) for h in hr]
    qk = [lax.dot_general(qb[h], k[h].astype(BF16), NT_DIMS, preferred_element_type=F32) for h in hr]
    s = [(qk[h] * jnp.exp(d_mat[h] - m_t[h])).astype(BF16) for h in hr]
    a_in = [jnp.exp(m_in[h] - m_t[h]) for h in hr]
    q_c = [jnp.dot(qb[h], c_old[h].astype(BF16), preferred_element_type=F32) for h in hr]
    numden = [jnp.dot(s[h], v_ext[h], preferred_element_type=F32) + a_in[h] * q_c[h] for h in hr]
    for n, (d, h) in enumerate(items):
        c_refs[n][...] = decay[n] * c_old[n] + kv[n]
        m_refs[n][...] = jnp.broadcast_to(m_new[n], m_refs[n].shape)
        den = numden[n][:, DV_M:DV_M + 1]
        o_refs[d][:, h * DV_M:(h + 1) * DV_M] = (numden[n][:, :DV_M]
                                                 / jnp.maximum(jnp.abs(den), jnp.exp(-m_t[n])))


def mlstm_scan(p, g_dir, gt_dir, cos_t, sin_t, dims):
    ncx = dims.ctx // CHUNK
    nlx = dims.seq // CHUNK
    x_blocks = dims.m_x // CHUNK

    def row_block(b, d, t):
        in_ctx = t < ncx
        cc = jnp.where(d == 0, t, ncx - 1 - t)
        cx = jnp.where(d == 0, t - ncx, nlx - 1 - (t - ncx))
        return jnp.where(in_ctx, x_blocks + b * ncx + cc, b * nlx + cx)

    def rope_block(b, d, t):
        in_ctx = t < ncx
        cx = jnp.where(d == 0, t - ncx, nlx - 1 - (t - ncx))
        return jnp.where(in_ctx, nlx, cx)

    def dir_specs(d):
        return [
            pl.BlockSpec((CHUNK, H_M * DK_M), lambda b, t: (row_block(b, d, t), PM_Q // (H_M * DK_M))),
            pl.BlockSpec((CHUNK, H_M * DK_M), lambda b, t: (row_block(b, d, t), PM_K // (H_M * DK_M))),
            pl.BlockSpec((CHUNK, H_M * DV_M), lambda b, t: (row_block(b, d, t), PM_V // (H_M * DV_M))),
            pl.BlockSpec((1, CHUNK, LANES), lambda b, t: (d, row_block(b, d, t), 0)),
            pl.BlockSpec((1, 8, CHUNK), lambda b, t: (d, 0, row_block(b, d, t))),
            pl.BlockSpec((CHUNK, DK_M), lambda b, t: (rope_block(b, d, t), 0)),
            pl.BlockSpec((CHUNK, DK_M), lambda b, t: (rope_block(b, d, t), 0)),
        ]

    h_shape = jax.ShapeDtypeStruct((dims.m_all, H_M * DV_M), F32)
    dir_args = (p, p, p, g_dir, gt_dir, cos_t, sin_t)
    return pl.pallas_call(
        _mlstm_kernel,
        out_shape=[h_shape, h_shape],
        grid=(dims.batch, ncx + nlx),
        in_specs=dir_specs(0) + dir_specs(1),
        out_specs=[pl.BlockSpec((CHUNK, H_M * DV_M), lambda b, t: (row_block(b, 0, t), 0)),
                   pl.BlockSpec((CHUNK, H_M * DV_M), lambda b, t: (row_block(b, 1, t), 0))],
        scratch_shapes=([pltpu.VMEM((DK_M, DV_EXT), F32)] * (2 * H_M)
                        + [pltpu.VMEM((8, LANES), F32)] * (2 * H_M)),
        compiler_params=_cparams(("arbitrary", "arbitrary")), name="mlstm_scan",
    )(*dir_args, *dir_args)


def _mlstm_out_kernel(hf_ref, hb_ref, o_ref, g_ref, out_ref):
    for h in range(H_M):
        sl = slice(h * DV_M, (h + 1) * DV_M)
        y = _rms(hf_ref[:, sl] + hb_ref[:, sl], g_ref[:, sl])
        out_ref[:, sl] = (y * jax.nn.sigmoid(o_ref[:, sl])).astype(out_ref.dtype)


def mlstm_out(hf, hb, p, g, dims):
    m = dims.m_all
    tm = 256
    n = H_M * DV_M
    return pl.pallas_call(
        _mlstm_out_kernel,
        out_shape=jax.ShapeDtypeStruct((m, n), BF16),
        grid=(m // tm,),
        in_specs=[pl.BlockSpec((tm, n), lambda i: (i, 0)),
                  pl.BlockSpec((tm, n), lambda i: (i, 0)),
                  pl.BlockSpec((tm, n), lambda i: (i, PM_O // n)),
                  pl.BlockSpec((1, n), lambda i: (0, 0))],
        out_specs=pl.BlockSpec((tm, n), lambda i: (i, 0)),
        compiler_params=_cparams(("arbitrary",)), name="mlstm_out",
    )(hf, hb, p, g.reshape(1, n))


def rope_tables(dims):
    t = jnp.arange(dims.seq)
    row = (t // GRID_W).astype(F32)
    col = (t % GRID_W).astype(F32)
    quarter = DK_M // 4
    inv = ROPE_BASE ** (-jnp.arange(quarter, dtype=F32) / quarter)
    ang = jnp.concatenate([row[:, None] * inv, col[:, None] * inv], axis=-1)
    cos = jnp.repeat(jnp.cos(ang), 2, axis=-1)
    sin = jnp.repeat(jnp.sin(ang), 2, axis=-1) * jnp.tile(jnp.array([-1.0, 1.0], F32), DK_M // 2)
    cos = jnp.concatenate([cos, jnp.ones((CHUNK, DK_M), F32)], axis=0)
    sin = jnp.concatenate([sin, jnp.zeros((CHUNK, DK_M), F32)], axis=0)
    return cos, sin


def _gather_kernel(src_ref, nrows_ref, h_hbm, o_ref, buf_ref, sem, *, tm):
    i = pl.program_id(0)
    n = nrows_ref[i]

    @pl.when(n < tm)
    def _():
        buf_ref[...] = jnp.zeros(buf_ref.shape, buf_ref.dtype)

    def start(r, carry):
        pltpu.make_async_copy(h_hbm.at[pl.ds(src_ref[i * tm + r], 1)], buf_ref.at[pl.ds(r, 1)], sem).start()
        return carry

    def wait(r, carry):
        pltpu.make_async_copy(h_hbm.at[pl.ds(0, 1)], buf_ref.at[pl.ds(r, 1)], sem).wait()
        return carry

    @pl.when(n == tm)
    def _():
        lax.fori_loop(0, tm, start, 0, unroll=8)
        pltpu.make_async_copy(h_hbm.at[pl.ds(0, tm)], buf_ref, sem).wait()

    @pl.when(n < tm)
    def _():
        lax.fori_loop(0, n, start, 0)
        lax.fori_loop(0, n, wait, 0)

    o_ref[...] = buf_ref[...].astype(o_ref.dtype)


def gather_rows(h, src, tile_rows, *, tm):
    m, d = h.shape
    r_total = src.shape[0]
    return pl.pallas_call(
        functools.partial(_gather_kernel, tm=tm),
        out_shape=jax.ShapeDtypeStruct((r_total, d), BF16),
        grid_spec=pltpu.PrefetchScalarGridSpec(
            num_scalar_prefetch=2,
            grid=(r_total // tm,),
            in_specs=[pl.BlockSpec(memory_space=pl.ANY)],
            out_specs=pl.BlockSpec((tm, d), lambda i, *s: (i, 0)),
            scratch_shapes=[pltpu.VMEM((tm, d), F32), pltpu.SemaphoreType.DMA],
        ),
        compiler_params=_cparams(("arbitrary",)), name="moe_gather",
    )(src, tile_rows, h)


def _combine_kernel(p1_ref, p2_ref, y_hbm, x_ref, gate_ref, rt_ref, o_ref, y1_ref, y2_ref, sem,
                    *, tm, seq, n_groups):
    i = pl.program_id(0)

    def start(r, carry):
        t = i * tm + r
        pltpu.make_async_copy(y_hbm.at[pl.ds(p1_ref[t], 1)], y1_ref.at[pl.ds(r, 1)], sem.at[0]).start()
        pltpu.make_async_copy(y_hbm.at[pl.ds(p2_ref[t], 1)], y2_ref.at[pl.ds(r, 1)], sem.at[1]).start()
        return carry

    lax.fori_loop(0, tm, start, 0, unroll=8)
    pltpu.make_async_copy(y_hbm.at[pl.ds(0, tm)], y1_ref, sem.at[0]).wait()
    pltpu.make_async_copy(y_hbm.at[pl.ds(0, tm)], y2_ref, sem.at[1]).wait()
    row0 = i * tm
    grp = jnp.int32(n_groups - 1)
    for b in reversed(range(n_groups - 1)):
        grp = jnp.where(row0 < (b + 1) * seq, b, grp)
    gate = gate_ref[pl.ds(grp, 1), :]
    w1 = rt_ref[:, 2:3]
    w2 = rt_ref[:, 3:4]
    o_ref[...] = x_ref[...] + gate * (w1 * y1_ref[...] + w2 * y2_ref[...])


def moe_combine(xs, y, pos1, pos2, route_t, mod, layer, gate_chunk, dims, *, m):
    d = dims.d_model
    tm = _pick_tile(math.gcd(dims.seq, dims.ctx), (256, 128))
    kern = functools.partial(_combine_kernel, tm=tm, seq=dims.seq, n_groups=dims.batch + 1)
    return pl.pallas_call(
        kern,
        out_shape=jax.ShapeDtypeStruct((m, d), F32),
        grid_spec=pltpu.PrefetchScalarGridSpec(
            num_scalar_prefetch=2,
            grid=(m // tm,),
            in_specs=[pl.BlockSpec(memory_space=pl.ANY),
                      pl.BlockSpec((tm, d), lambda i, a, b: (i, 0)),
                      pl.BlockSpec((8, d), lambda i, a, b: (MOD_ROWS // 8 * layer, gate_chunk)),
                      pl.BlockSpec((tm, 8), lambda i, a, b: (i, 0))],
            out_specs=pl.BlockSpec((tm, d), lambda i, a, b: (i, 0)),
            scratch_shapes=[pltpu.VMEM((tm, d), F32), pltpu.VMEM((tm, d), F32),
                            pltpu.SemaphoreType.DMA((2,))],
        ),
        compiler_params=_cparams(("arbitrary",)), name="moe_combine",
    )(pos1, pos2, y, xs, mod, route_t)


def moe_dispatch_plan(route, m, tm):
    e = jnp.concatenate([route[0, :m], route[1, :m]]).astype(I32)
    tok = jnp.tile(jnp.arange(m, dtype=I32), 2)
    onehot = (e[:, None] == jnp.arange(N_EXPERTS, dtype=I32)[None, :]).astype(I32)
    rank = jnp.sum((jnp.cumsum(onehot, axis=0) - onehot) * onehot, axis=1)
    counts = jnp.sum(onehot, axis=0)
    padded = ((counts + tm - 1) // tm) * tm
    ends = jnp.cumsum(padded)
    offs = ends - padded
    pos = offs[e] + rank
    r_total = TOP_K * m + N_EXPERTS * tm
    src = jnp.zeros((r_total,), I32).at[pos].set(tok)
    tile_start = jnp.arange(r_total // tm, dtype=I32) * tm
    last_e = jnp.max(jnp.where(counts > 0, jnp.arange(N_EXPERTS, dtype=I32), 0))
    tile_e = jnp.minimum(jnp.sum((tile_start[:, None] >= ends[None, :]).astype(I32), axis=1), last_e)
    tile_rows = jnp.clip((offs + counts)[tile_e] - tile_start, 0, tm).astype(I32)
    return src, pos[:m], pos[m:], tile_e, tile_rows


def _dense_tile(m, candidates):
    return _pick_tile(m, candidates)


def kernel(x, c, ctx, c_ctx, ada_w, ada_b, norm1_g, norm2_g, w_in, gate_b, na_q_g, na_k_g, na_rpb, m_norm_g, w_out, ffn_w_gate, ffn_w_up, ffn_w_down, moe_router, moe_w_gate, moe_w_up, moe_w_down):
    batch, seq, d = x.shape
    dims = Dims(batch=batch, seq=seq, ctx=ctx.shape[1], d_model=d, d_ff=ffn_w_gate.shape[2], depth=ada_w.shape[0])
    depth = dims.depth
    n_groups = batch + 1
    rows = seq // GRID_W
    assert rows >= 4 * NA_QROWS and rows % NA_QROWS == 0 and batch + 1 <= 8
    assert seq % 256 == 0 and dims.ctx % CHUNK == 0 and dims.m_x % dims.ctx == 0

    xs = jnp.concatenate([x.reshape(dims.m_x, d), ctx.reshape(batch * dims.ctx, d)], axis=0)

    cond = jnp.concatenate([jax.nn.silu(c), jax.nn.silu(c_ctx)[None], jnp.zeros((MOD_ROWS - n_groups, d), F32)], axis=0)
    tn_ada = _pick_tile(6 * d, (1024, 512, 256))
    tn_d = min(512, d)
    mod = gmm((cond.astype(BF16),), (ada_w,), jnp.arange(depth, dtype=I32), jnp.full((depth,), MOD_ROWS, I32),
              tm=MOD_ROWS, tn=tn_ada, n_out=6 * d, out_dtype=F32, epilogue="bias",
              extras=(ada_b.reshape(depth, 1, 6 * d),),
              extra_specs=(((1, 1, tn_ada), lambda j, i, g, *s: (g[i], 0, j)),),
              a_fixed=True, rows_outer=True, name="ada_mod")

    cos_t, sin_t = rope_tables(dims)
    na_bias = na_bias_table(na_rpb, rows)
    w_gate_cols = jnp.pad(w_in[:, :, N_MAIN:], ((0, 0), (0, 0), (0, LANES - 4 * H_M)))
    gate_b_pad = jnp.pad(gate_b, ((0, 0), (0, LANES - 4 * H_M)))

    def dense_groups(m, tm, g):
        nt = m // tm
        return jnp.full((nt,), g, I32), jnp.full((nt,), tm, I32)

    for layer in range(depth):
        last = layer == depth - 1
        m_out = dims.m_x if last else dims.m_all

        hbf = norm_mod(xs, norm1_g[layer], mod, layer, 0, 1, dims, m=dims.m_all)
        qk, vbf, pm, gates = in_proj(hbf, w_in, w_gate_cols, gate_b_pad, layer, na_q_g[layer], na_k_g[layer])
        g_dir = jnp.stack([gates, jnp.roll(gates, -2 * H_M, axis=1)])
        gt_dir = jnp.stack([gates[:, :2 * H_M].T, gates[:, 2 * H_M:4 * H_M].T])

        na_out = na_attention(qk, vbf, na_bias, layer, dims)
        hf, hb = mlstm_scan(pm, g_dir, gt_dir, cos_t, sin_t, dims)
        m_mix = mlstm_out(hf, hb, pm, m_norm_g[layer], dims)
        if not last:
            na_out = jnp.concatenate([na_out, ctx_attention(qk, vbf, dims)], axis=0)

        tm_o = _dense_tile(m_out, (1088, 1024, 640, 512, 256))
        grp, nv = dense_groups(m_out, tm_o, layer)
        res_specs = lambda tm, tn, chunk: (
            ((tm, tn), lambda j, i, *s: (i, j)),
            ((8, tn), lambda j, i, *s: (MOD_ROWS // 8 * layer, chunk * (d // tn) + j)))
        tn_o = min(1024, d)
        xs = gmm((na_out, m_mix), (w_out,), grp, nv, tm=tm_o, tn=tn_o, n_out=d, out_dtype=F32, epilogue="residual",
                 extras=(xs, mod), extra_specs=res_specs(tm_o, tn_o, 2), seq=seq, n_groups=n_groups,
                 name="w_out_proj")

        idx = layer // 2
        if layer % 2 == 0:
            h2 = norm_mod(xs, norm2_g[layer], mod, layer, 3, 4, dims, m=m_out)
            tm_u = _dense_tile(m_out, (1088, 1024, 640, 512, 256))
            grp, nv = dense_groups(m_out, tm_u, idx)
            act = gmm_swiglu(h2, ffn_w_gate, ffn_w_up, grp, nv, tm=tm_u, tn=512, name="ffn_up")
            tm_d = _dense_tile(m_out, (544, 512, 320, 256))
            grp, nv = dense_groups(m_out, tm_d, idx)
            xs = gmm((act,), (ffn_w_down,), grp, nv, tm=tm_d, tn=tn_d, n_out=d, out_dtype=F32, epilogue="residual",
                     extras=(xs, mod), extra_specs=res_specs(tm_d, tn_d, 5), seq=seq, n_groups=n_groups,
                     name="ffn_down")
        else:
            h2, route = norm_mod(xs, norm2_g[layer], mod, layer, 3, 4, dims, m=m_out,
                                 router_t=moe_router[idx].T)
            tm_e = 512 if m_out >= 4096 else 128
            src, pos1, pos2, tile_e, tile_rows = moe_dispatch_plan(route, m_out, tm_e)
            hs = gather_rows(h2, src, tile_rows, tm=tm_e)
            grp = tile_e + idx * N_EXPERTS
            nw = moe_w_gate.shape[0] * N_EXPERTS
            act = gmm_swiglu(hs, moe_w_gate.reshape(nw, d, dims.d_ff), moe_w_up.reshape(nw, d, dims.d_ff),
                             grp, tile_rows, tm=tm_e, tn=512, half_tiles=True, name="moe_up")
            y = gmm((act,), (moe_w_down.reshape(nw, dims.d_ff, d),), grp, tile_rows, tm=tm_e, tn=tn_d, n_out=d,
                    out_dtype=F32, half_tiles=True, name="moe_down")
            xs = moe_combine(xs, y, pos1, pos2, route.T, mod, layer, 5, dims, m=m_out)

    return xs[:dims.m_x].reshape(batch, seq, d)
```

```python
import functools
import math
from typing import NamedTuple

import jax
import jax.numpy as jnp
import numpy as np
from jax import lax
from jax.experimental import pallas as pl
from jax.experimental.pallas import tpu as pltpu

F32 = jnp.float32
BF16 = jnp.bfloat16
I32 = jnp.int32

GRID_W = 64
H_NA = 8
DH_NA = 128
WIN_R = 8
WIN_C = 16
H_M = 4
DK_M = 128
DV_M = 256
CHUNK = 128
ROPE_BASE = 10000.0
N_EXPERTS = 8
TOP_K = 2
EPS = 1e-6
D_MIX = 2048
NEG = -1e30

OFF_MQ = 3 * H_NA * DH_NA
N_MAIN = 6144
PM_Q, PM_K, PM_V, PM_O = 0, H_M * DK_M, 2 * H_M * DK_M, 2 * H_M * DK_M + H_M * DV_M
DV_EXT = DV_M + 128

LANES = 128
VMEM_LIMIT = 56 * 1024 * 1024
MOD_ROWS = 16


class Dims(NamedTuple):
    batch: int
    seq: int
    ctx: int
    d_model: int
    d_ff: int
    depth: int

    @property
    def m_x(self):
        return self.batch * self.seq

    @property
    def m_all(self):
        return self.batch * (self.seq + self.ctx)


def _cparams(sem):
    return pltpu.CompilerParams(dimension_semantics=sem, vmem_limit_bytes=VMEM_LIMIT)


def _group_select(row, vals, seq):
    out = vals[-1]
    for b in reversed(range(len(vals) - 1)):
        out = jnp.where(row < (b + 1) * seq, vals[b], out)
    return out


def _pick_tile(m, candidates):
    for t in candidates:
        if m % t == 0:
            return t
    raise ValueError(f"no row tile for {m}")


N_EXTRA = {"residual": 2, "bias": 1, "swiglu": 0, None: 0}


def _gmm_kernel(grp_ref, rows_ref, rid_ref, rgrp_ref, nruns_ref, *refs,
                n_a, n_w, epilogue, tm, tn, nj, seq, n_groups, rows_outer, half_tiles):
    a_refs, refs = refs[:n_a], refs[n_a:]
    w_refs, refs = refs[:n_w], refs[n_w:]
    x_refs, refs = refs[:N_EXTRA[epilogue]], refs[N_EXTRA[epilogue]:]
    o_ref, refs = refs[0], refs[1:]
    wbf_refs, refs = refs[:n_w], refs[n_w:]
    if rows_outer:
        i = pl.program_id(0)
        for w_ref, wbf_ref in zip(w_refs, wbf_refs):
            wbf_ref[...] = w_ref[0].astype(BF16)
    else:
        wst_refs, sem = refs[:n_w], refs[n_w]
        j = pl.program_id(0)
        i = pl.program_id(1)
        rid = rid_ref[i]
        nruns = nruns_ref[0]

        def tile_copy(widx, g, jj, slot):
            src = w_refs[widx].at[g, :, pl.ds(pl.multiple_of(jj * tn, tn), tn)]
            return pltpu.make_async_copy(src, wst_refs[widx].at[slot], sem.at[widx, slot])

        @pl.when((i == 0) | (rid != rid_ref[jnp.maximum(i - 1, 0)]))
        def _():
            q = j * nruns + rid
            slot = q % 2

            @pl.when(q == 0)
            def _():
                for widx in range(n_w):
                    tile_copy(widx, rgrp_ref[0], 0, 0).start()

            for widx in range(n_w):
                tile_copy(widx, 0, 0, slot).wait()
            wrap = rid + 1 == nruns
            next_run = jnp.where(wrap, 0, rid + 1)
            next_j = j + wrap.astype(I32)

            @pl.when(next_j < nj)
            def _():
                for widx in range(n_w):
                    tile_copy(widx, rgrp_ref[next_run], next_j, 1 - slot).start()

            for widx in range(n_w):
                wbf_refs[widx][...] = wst_refs[widx][slot].astype(BF16)

    n_rows = rows_ref[i]

    def compute(nr):
        def matmul(wbf_ref):
            out, k0 = None, 0
            for a_ref in a_refs:
                kp = a_ref.shape[1]
                part = jnp.dot(a_ref[0:nr, :], wbf_ref[k0:k0 + kp, :], preferred_element_type=F32)
                out = part if out is None else out + part
                k0 += kp
            return out

        acc = matmul(wbf_refs[0])
        if epilogue == "residual":
            res_ref, gate_ref = x_refs
            row = i * tm + lax.broadcasted_iota(I32, (nr, 1), 0)
            gate = _group_select(row, [gate_ref[k:k + 1, :] for k in range(n_groups)], seq)
            acc = res_ref[0:nr, :] + gate * acc
        elif epilogue == "bias":
            acc = acc + x_refs[0][0]
        elif epilogue == "swiglu":
            acc = acc * jax.nn.sigmoid(acc) * matmul(wbf_refs[1])
        o_ref[0:nr, :] = acc.astype(o_ref.dtype)
        if nr < tm:
            o_ref[nr:, :] = jnp.zeros((tm - nr, o_ref.shape[1]), o_ref.dtype)

    steps = [tm * q // 4 for q in (1, 2, 3, 4)] if half_tiles else [tm]
    lo = 0
    for nr in steps:
        pl.when((n_rows > lo) & (n_rows <= nr))(functools.partial(compute, nr))
        lo = nr

    @pl.when(n_rows == 0)
    def _():
        o_ref[...] = jnp.zeros(o_ref.shape, o_ref.dtype)


def _runs(grp):
    nt = grp.shape[0]
    first = jnp.concatenate([jnp.ones((1,), I32), (grp[1:] != grp[:-1]).astype(I32)])
    run_id = jnp.cumsum(first) - 1
    sel = (run_id[:, None] == jnp.arange(nt, dtype=I32)[None, :]).astype(I32) * first[:, None]
    run_grp = jnp.sum(sel * grp[:, None], axis=0)
    return run_id.astype(I32), run_grp.astype(I32), (run_id[-1:] + 1).astype(I32)


def gmm(a_parts, ws, grp, tile_rows, *, tm, tn, n_out, out_dtype, epilogue=None, extras=(), extra_specs=(),
        a_fixed=False, seq=0, n_groups=0, rows_outer=False, half_tiles=False, name="gmm"):
    k = sum(a.shape[1] for a in a_parts)
    nt = grp.shape[0]
    n_w = len(ws)
    nj = n_out // tn

    def spec(shape, fn):
        return pl.BlockSpec(shape, (lambda i, j, *s: fn(j, i, *s)) if rows_outer else fn)

    a_map = (lambda j, i, *s: (0, 0)) if a_fixed else (lambda j, i, *s: (i, 0))
    if rows_outer:
        w_specs = [spec((1, k, tn), lambda j, i, g, *s: (g[i], 0, j))] * n_w
        stream_scratch = []
    else:
        w_specs = [pl.BlockSpec(memory_space=pl.ANY)] * n_w
        stream_scratch = [pltpu.VMEM((2, k, tn), F32)] * n_w + [pltpu.SemaphoreType.DMA((n_w, 2))]
    in_specs = ([spec((tm, a.shape[1]), a_map) for a in a_parts] + w_specs
                + [spec(shape, fn) for shape, fn in extra_specs])
    kern = functools.partial(_gmm_kernel, n_a=len(a_parts), n_w=n_w, epilogue=epilogue, tm=tm, tn=tn, nj=nj,
                             seq=seq, n_groups=n_groups, rows_outer=rows_outer, half_tiles=half_tiles)
    run_id, run_grp, nruns = _runs(grp)
    return pl.pallas_call(
        kern,
        out_shape=jax.ShapeDtypeStruct((nt * tm, n_out), out_dtype),
        grid_spec=pltpu.PrefetchScalarGridSpec(
            num_scalar_prefetch=5,
            grid=(nt, nj) if rows_outer else (nj, nt),
            in_specs=in_specs,
            out_specs=spec((tm, tn), lambda j, i, *s: (i, j)),
            scratch_shapes=[pltpu.VMEM((k, tn), BF16)] * n_w + stream_scratch,
        ),
        compiler_params=_cparams(("arbitrary", "arbitrary")),
        name=name,
    )(grp, tile_rows, run_id, run_grp, nruns, *a_parts, *ws, *extras)


def gmm_swiglu(a, wg, wu, grp, tile_rows, *, tm, tn, half_tiles=False, name="gmm_swiglu"):
    return gmm((a,), (wg, wu), grp, tile_rows, tm=tm, tn=tn, n_out=wg.shape[2], out_dtype=BF16,
               epilogue="swiglu", half_tiles=half_tiles, name=name)


def _rms(x, g):
    return x * lax.rsqrt(jnp.mean(x * x, axis=-1, keepdims=True) + EPS) * g


def _split_bf16(x):
    hi = x.astype(BF16)
    lo = (x - hi.astype(F32)).astype(BF16)
    return hi, lo


def _norm_mod_kernel(x_ref, g_ref, shift_ref, scale_ref, *refs, tm, seq, n_groups, with_router):
    if with_router:
        r_ref, o_ref, route_ref = refs
    else:
        (o_ref,) = refs
    i = pl.program_id(0)
    row0 = i * tm
    grp = jnp.int32(n_groups - 1)
    for b in reversed(range(n_groups - 1)):
        grp = jnp.where(row0 < (b + 1) * seq, b, grp)
    y = _rms(x_ref[...], g_ref[...])
    h = y * (1.0 + scale_ref[pl.ds(grp, 1), :]) + shift_ref[pl.ds(grp, 1), :]
    o_ref[...] = h.astype(o_ref.dtype)
    if with_router:
        nt_dims = (((1,), (1,)), ((), ()))
        h_hi, h_lo = _split_bf16(h)
        r_hi, r_lo = _split_bf16(r_ref[...])
        logits = (lax.dot_general(r_hi, h_hi, nt_dims, preferred_element_type=F32)
                  + lax.dot_general(r_hi, h_lo, nt_dims, preferred_element_type=F32)
                  + lax.dot_general(r_lo, h_hi, nt_dims, preferred_element_type=F32))
        ids = lax.broadcasted_iota(I32, logits.shape, 0)
        m1 = jnp.max(logits, axis=0, keepdims=True)
        i1 = jnp.min(jnp.where(logits == m1, ids, N_EXPERTS), axis=0, keepdims=True)
        rest = jnp.where(ids == i1, -jnp.inf, logits)
        m2 = jnp.max(rest, axis=0, keepdims=True)
        i2 = jnp.min(jnp.where(rest == m2, ids, N_EXPERTS), axis=0, keepdims=True)
        e = jnp.exp(m2 - m1)
        w1 = 1.0 / (1.0 + e)
        w2 = e / (1.0 + e)
        out_row = lax.broadcasted_iota(I32, (8, tm), 0)
        route_ref[...] = jnp.where(out_row == 0, i1.astype(F32), jnp.where(
            out_row == 1, i2.astype(F32), jnp.where(out_row == 2, w1, jnp.where(out_row == 3, w2, 0.0))))


def norm_mod(xs, g, mod, layer, shift_chunk, scale_chunk, dims, *, m, router_t=None):
    d = dims.d_model
    tm = _pick_tile(math.gcd(dims.seq, dims.ctx), (256, 128))
    with_router = router_t is not None
    kern = functools.partial(_norm_mod_kernel, tm=tm, seq=dims.seq, n_groups=dims.batch + 1,
                             with_router=with_router)
    in_specs = [
        pl.BlockSpec((tm, d), lambda i: (i, 0)),
        pl.BlockSpec((1, d), lambda i: (0, 0)),
        pl.BlockSpec((8, d), lambda i: (MOD_ROWS // 8 * layer, shift_chunk)),
        pl.BlockSpec((8, d), lambda i: (MOD_ROWS // 8 * layer, scale_chunk)),
    ]
    args = [xs, g.reshape(1, d), mod, mod]
    out_shape = [jax.ShapeDtypeStruct((m, d), F32 if with_router else BF16)]
    out_specs = [pl.BlockSpec((tm, d), lambda i: (i, 0))]
    if with_router:
        in_specs.append(pl.BlockSpec((N_EXPERTS, d), lambda i: (0, 0)))
        args.append(router_t)
        out_shape.append(jax.ShapeDtypeStruct((8, m), F32))
        out_specs.append(pl.BlockSpec((8, tm), lambda i: (0, i)))
    res = pl.pallas_call(
        kern, out_shape=out_shape, grid=(m // tm,), in_specs=in_specs, out_specs=out_specs,
        compiler_params=_cparams(("arbitrary",)), name="norm_mod_router" if with_router else "norm_mod",
    )(*args)
    return res if with_router else res[0]


IN_TN = 512
J_QK = 2 * H_NA * DH_NA // IN_TN
J_V = H_NA * DH_NA // IN_TN
N_MLSTM = N_MAIN - OFF_MQ


def _in_proj_kernel(a_ref, w_ref, wgate_ref, gb_ref, qg_ref, kg_ref,
                    qk_ref, v_ref, pm_ref, gates_ref, wbf_ref):
    j = pl.program_id(1)

    @pl.when(j == 0)
    def _():
        gates_ref[...] = (jnp.dot(a_ref[...], wgate_ref[0].astype(BF16), preferred_element_type=F32)
                          + gb_ref[...])

    wbf_ref[...] = w_ref[0].astype(BF16)
    acc = jnp.dot(a_ref[...], wbf_ref[...], preferred_element_type=F32)

    @pl.when(j < J_QK)
    def _():
        g = jnp.where(j < J_QK // 2, qg_ref[...] * (DH_NA ** -0.5), kg_ref[...])
        for h in range(IN_TN // DH_NA):
            sl = slice(h * DH_NA, (h + 1) * DH_NA)
            qk_ref[:, sl] = _rms(acc[:, sl], g).astype(BF16)

    @pl.when((j >= J_QK) & (j < J_QK + J_V))
    def _():
        v_ref[...] = acc.astype(BF16)

    @pl.when(j >= J_QK + J_V)
    def _():
        pm_ref[...] = acc


def in_proj(hbf, w_in, w_gate_cols, gate_b_pad, layer, qg, kg):
    m, d = hbf.shape
    tm = _pick_tile(m, (2176, 1088, 640, 512, 256))
    nj = N_MAIN // IN_TN
    return pl.pallas_call(
        _in_proj_kernel,
        out_shape=[jax.ShapeDtypeStruct((m, 2 * H_NA * DH_NA), BF16),
                   jax.ShapeDtypeStruct((m, H_NA * DH_NA), BF16),
                   jax.ShapeDtypeStruct((m, N_MLSTM), F32),
                   jax.ShapeDtypeStruct((m, LANES), F32)],
        grid=(m // tm, nj),
        in_specs=[pl.BlockSpec((tm, d), lambda i, j: (i, 0)),
                  pl.BlockSpec((1, d, IN_TN), lambda i, j: (layer, 0, j)),
                  pl.BlockSpec((1, d, LANES), lambda i, j: (layer, 0, 0)),
                  pl.BlockSpec((None, 1, LANES), lambda i, j: (layer, 0, 0)),
                  pl.BlockSpec((1, DH_NA), lambda i, j: (0, 0)),
                  pl.BlockSpec((1, DH_NA), lambda i, j: (0, 0))],
        out_specs=[pl.BlockSpec((tm, IN_TN), lambda i, j: (i, jnp.minimum(j, J_QK - 1))),
                   pl.BlockSpec((tm, IN_TN), lambda i, j: (i, jnp.clip(j - J_QK, 0, J_V - 1))),
                   pl.BlockSpec((tm, IN_TN), lambda i, j: (i, jnp.maximum(j - (J_QK + J_V), 0))),
                   pl.BlockSpec((tm, LANES), lambda i, j: (i, 0))],
        scratch_shapes=[pltpu.VMEM((d, IN_TN), BF16)],
        compiler_params=_cparams(("arbitrary", "arbitrary")), name="in_proj",
    )(hbf, w_in, w_gate_cols, gate_b_pad.reshape(-1, 1, LANES), qg.reshape(1, DH_NA), kg.reshape(1, DH_NA))


NT_DIMS = (((1,), (1,)), ((), ()))
TN_DIMS = (((0,), (0,)), ((), ()))
NA_QROWS = 4
NA_HEADS = 4
NA_KROWS = NA_QROWS + WIN_R
N_BIAS_PATTERNS = 3


def _na_block_start(qb, rows):
    return jnp.clip(qb * NA_QROWS - WIN_R // 2, 0, rows - NA_KROWS)


def _na_kernel(q_ref, k_ref, v_ref, kc_ref, vc_ref, bias_ref, o_ref, *, rows):
    qb = pl.program_id(2)
    nblk = rows // NA_QROWS
    pid = jnp.where(qb == 0, 0, jnp.where(qb == nblk - 1, 2, 1))
    start = pl.multiple_of(_na_block_start(qb, rows) * GRID_W, GRID_W)
    hr = range(NA_HEADS)
    sl = [slice(h * DH_NA, (h + 1) * DH_NA) for h in hr]
    win = pl.ds(start, NA_KROWS * GRID_W)
    s = [lax.dot_general(q_ref[:, sl[h]], k_ref[win, sl[h]], NT_DIMS, preferred_element_type=F32)
         + bias_ref[h, pid] for h in hr]
    sc = [lax.dot_general(q_ref[:, sl[h]], kc_ref[:, sl[h]], NT_DIMS, preferred_element_type=F32) for h in hr]
    m = [jnp.maximum(jnp.max(s[h], axis=-1, keepdims=True), jnp.max(sc[h], axis=-1, keepdims=True)) for h in hr]
    pb = [jnp.exp(s[h] - m[h]) for h in hr]
    pc = [jnp.exp(sc[h] - m[h]) for h in hr]
    den = [jnp.sum(pb[h], axis=-1, keepdims=True) + jnp.sum(pc[h], axis=-1, keepdims=True) for h in hr]
    o = [jnp.dot(pb[h].astype(BF16), v_ref[win, sl[h]], preferred_element_type=F32)
         + jnp.dot(pc[h].astype(BF16), vc_ref[:, sl[h]], preferred_element_type=F32) for h in hr]
    for h in hr:
        o_ref[:, sl[h]] = (o[h] / den[h]).astype(o_ref.dtype)


N_ROW_OFFSETS = 2 * WIN_R - 1


def _na_bias_kernel(tz_ref, o_ref, *, block_of):
    for p in range(N_BIAS_PATTERNS):
        for x in range(NA_QROWS):
            for j in range(NA_KROWS):
                o_ref[p, x * GRID_W:(x + 1) * GRID_W, j * GRID_W:(j + 1) * GRID_W] = tz_ref[block_of[p][x][j]]


def na_bias_table(rpb, rows):
    nblk = rows // NA_QROWS
    qb = np.array([0, 2, nblk - 1])
    ks = np.clip(qb * NA_QROWS - WIN_R // 2, 0, rows - NA_KROWS)
    r = qb[:, None] * NA_QROWS + np.arange(NA_QROWS)[None, :]
    rs = np.clip(r - WIN_R // 2, 0, rows - WIN_R)
    kr = ks[:, None] + np.arange(NA_KROWS)[None, :]
    in_band = (kr[:, None, :] >= rs[:, :, None]) & (kr[:, None, :] < rs[:, :, None] + WIN_R)
    dr = kr[:, None, :] - r[:, :, None] + (WIN_R - 1)
    block_of = np.where(in_band, dr, N_ROW_OFFSETS).tolist()
    c = np.arange(GRID_W)
    dc = np.clip(c[None, :] - c[:, None], -(WIN_C - 1), WIN_C - 1) + (WIN_C - 1)
    cs = np.clip(c - WIN_C // 2, 0, GRID_W - WIN_C)
    col_ok = (c[None, :] >= cs[:, None]) & (c[None, :] < cs[:, None] + WIN_C)
    sel_c = jnp.asarray(dc[..., None] == np.arange(2 * WIN_C - 1), F32)
    tz = jnp.einsum("lhab,qkb->lhaqk", rpb.astype(F32), sel_c, precision=lax.Precision.HIGHEST)
    tz = jnp.where(jnp.asarray(col_ok), tz, NEG)
    n_l = rpb.shape[0]
    tz = jnp.concatenate([tz, jnp.full((n_l, H_NA, 1, GRID_W, GRID_W), NEG, F32)], axis=2)
    tq, tk = NA_QROWS * GRID_W, NA_KROWS * GRID_W
    return pl.pallas_call(
        functools.partial(_na_bias_kernel, block_of=block_of),
        out_shape=jax.ShapeDtypeStruct((n_l, H_NA, N_BIAS_PATTERNS, tq, tk), F32),
        grid=(n_l, H_NA),
        in_specs=[pl.BlockSpec((None, None, N_ROW_OFFSETS + 1, GRID_W, GRID_W), lambda l, h: (l, h, 0, 0, 0))],
        out_specs=pl.BlockSpec((None, None, N_BIAS_PATTERNS, tq, tk), lambda l, h: (l, h, 0, 0, 0)),
        compiler_params=_cparams(("arbitrary", "arbitrary")), name="na_bias",
    )(tz)


def na_attention(qk, vbf, bias, layer, dims):
    rows = dims.seq // GRID_W
    qblocks = rows // NA_QROWS
    tq = NA_QROWS * GRID_W
    ctx_blk0 = dims.m_x // dims.ctx
    hgroups = H_NA // NA_HEADS
    hw = NA_HEADS * DH_NA
    kern = functools.partial(_na_kernel, rows=rows)
    return pl.pallas_call(
        kern,
        out_shape=jax.ShapeDtypeStruct((dims.m_x, H_NA * DH_NA), BF16),
        grid=(dims.batch, hgroups, qblocks),
        in_specs=[
            pl.BlockSpec((tq, hw), lambda b, h, i: (b * qblocks + i, h)),
            pl.BlockSpec((dims.seq, hw), lambda b, h, i: (b, hgroups + h)),
            pl.BlockSpec((dims.seq, hw), lambda b, h, i: (b, h)),
            pl.BlockSpec((dims.ctx, hw), lambda b, h, i: (ctx_blk0 + b, hgroups + h)),
            pl.BlockSpec((dims.ctx, hw), lambda b, h, i: (ctx_blk0 + b, h)),
            pl.BlockSpec((None, NA_HEADS, N_BIAS_PATTERNS, tq, NA_KROWS * GRID_W),
                         lambda b, h, i: (layer, h, 0, 0, 0)),
        ],
        out_specs=pl.BlockSpec((tq, hw), lambda b, h, i: (b * qblocks + i, h)),
        compiler_params=_cparams(("arbitrary", "arbitrary", "arbitrary")), name="na_attention",
    )(qk, qk, vbf, qk, vbf, bias)


def _ctx_attn_kernel(q_ref, k_ref, v_ref, o_ref):
    s = lax.dot_general(q_ref[...], k_ref[...], NT_DIMS, preferred_element_type=F32)
    m = jnp.max(s, axis=-1, keepdims=True)
    p = jnp.exp(s - m)
    den = jnp.sum(p, axis=-1, keepdims=True)
    o = jnp.dot(p.astype(BF16), v_ref[...], preferred_element_type=F32)
    o_ref[...] = (o / den).astype(o_ref.dtype)


def ctx_attention(qk, vbf, dims):
    ctx_blk0 = dims.m_x // dims.ctx
    return pl.pallas_call(
        _ctx_attn_kernel,
        out_shape=jax.ShapeDtypeStruct((dims.batch * dims.ctx, H_NA * DH_NA), BF16),
        grid=(dims.batch, H_NA),
        in_specs=[
            pl.BlockSpec((dims.ctx, DH_NA), lambda b, h: (ctx_blk0 + b, h)),
            pl.BlockSpec((dims.ctx, DH_NA), lambda b, h: (ctx_blk0 + b, H_NA + h)),
            pl.BlockSpec((dims.ctx, DH_NA), lambda b, h: (ctx_blk0 + b, h)),
        ],
        out_specs=pl.BlockSpec((dims.ctx, DH_NA), lambda b, h: (b, h)),
        compiler_params=_cparams(("arbitrary", "arbitrary")), name="ctx_attention",
    )(qk, qk, vbf)


def _log_sigmoid(x):
    return jnp.minimum(x, 0.0) - jnp.log(1.0 + jnp.exp(-jnp.abs(x)))


def _rope(t, cos, sin_signed, even):
    swapped = jnp.where(even, pltpu.roll(t, DK_M - 1, 1), pltpu.roll(t, 1, 1))
    return t * cos + swapped * sin_signed


def _mlstm_kernel(*refs):
    n_in = 7
    state = refs[2 * n_in + 2:]
    c_refs, m_refs = state[:2 * H_M], state[2 * H_M:]

    @pl.when(pl.program_id(1) == 0)
    def _():
        for ref in state:
            ref[...] = jnp.zeros(ref.shape, F32)

    _mlstm_chunks(refs[:n_in], refs[n_in:2 * n_in], refs[2 * n_in:2 * n_in + 2], c_refs, m_refs)


def _mlstm_chunks(fwd_refs, bwd_refs, o_refs, c_refs, m_refs):
    ti = lax.broadcasted_iota(I32, (CHUNK, CHUNK), 0)
    si = lax.broadcasted_iota(I32, (CHUNK, CHUNK), 1)
    even = (lax.broadcasted_iota(I32, (CHUNK, DK_M), 1) % 2) == 0
    ones_col = (lax.broadcasted_iota(I32, (CHUNK, DV_EXT - DV_M), 1) == 0).astype(BF16)
    per_dir = []
    for direction, (q_ref, k_ref, v_ref, g_ref, gt_ref, cos_ref, sin_ref) in enumerate((fwd_refs, bwd_refs)):
        mask = (si <= ti) if direction == 0 else (si >= ti)
        tri = mask.astype(BF16)
        g = g_ref[0]
        gt = gt_ref[0]
        lf_hi, lf_lo = _split_bf16(_log_sigmoid(g))
        cum_col = (jnp.dot(tri, lf_hi, preferred_element_type=F32)
                   + jnp.dot(tri, lf_lo, preferred_element_type=F32))
        lft = _log_sigmoid(gt)
        lft_hi, lft_lo = _split_bf16(lft)
        cum_row = (lax.dot_general(lft_hi, tri, NT_DIMS, preferred_element_type=F32)
                   + lax.dot_general(lft_lo, tri, NT_DIMS, preferred_element_type=F32))
        per_dir.append((q_ref, k_ref, v_ref, g, gt, lft, cum_col, cum_row, cos_ref[...], sin_ref[...], mask))
    items = [(d, h) for d in range(2) for h in range(H_M)]
    hr = range(len(items))
    qb, k, v_ext, b_col, i_col, b_row, i_row, b_last, masks = [], [], [], [], [], [], [], [], []
    for d, h in items:
        q_ref, k_ref, v_ref, g, gt, lft, cum_col, cum_row, cos, sin, mask = per_dir[d]
        qb.append(_rope(q_ref[:, h * DK_M:(h + 1) * DK_M], cos, sin, even).astype(BF16))
        k.append(_rope(k_ref[:, h * DK_M:(h + 1) * DK_M], cos, sin, even) * (DK_M ** -0.5))
        v_ext.append(jnp.concatenate([v_ref[:, h * DV_M:(h + 1) * DV_M].astype(BF16), ones_col], axis=1))
        b_col.append(cum_col[:, H_M + h:H_M + h + 1])
        i_col.append(g[:, h:h + 1])
        b_row.append(cum_row[H_M + h:H_M + h + 1, :])
        i_row.append(gt[h:h + 1, :])
        b_last.append(jnp.sum(lft[H_M + h:H_M + h + 1, :], axis=1, keepdims=True))
        masks.append(mask)
    m_old = [m_refs[h][0:1, 0:1] for h in hr]
    c_old = [c_refs[h][...] for h in hr]
    m_new = [jnp.maximum(b_last[h] + m_old[h], jnp.max(b_last[h] - b_row[h] + i_row[h], axis=1, keepdims=True))
             for h in hr]
    w_end = [jnp.exp(b_last[h] - b_col[h] + i_col[h] - m_new[h]) for h in hr]
    decay = [jnp.exp(b_last[h] + m_old[h] - m_new[h]) for h in hr]
    kw = [(k[h] * w_end[h]).astype(BF16) for h in hr]
    kv = [lax.dot_general(kw[h], v_ext[h], TN_DIMS, preferred_element_type=F32) for h in hr]
    d_mat = [jnp.where(masks[h], b_col[h] - b_row[h] + i_row[h], NEG) for h in hr]
    m_in = [b_col[h] + m_old[h] for h in hr]
    m_t = [jnp.maximum(m_in[h], jnp.max(d_mat[h], axis=1, keepdims=True)) for h in hr]
    qk = [lax.dot_general(qb[h], k[h].astype(BF16), NT_DIMS, preferred_element_type=F32) for h in hr]
    s = [(qk[h] * jnp.exp(d_mat[h] - m_t[h])).astype(BF16) for h in hr]
    a_in = [jnp.exp(m_in[h] - m_t[h]) for h in hr]
    q_c = [jnp.dot(qb[h], c_old[h].astype(BF16), preferred_element_type=F32) for h in hr]
    numden = [jnp.dot(s[h], v_ext[h], preferred_element_type=F32) + a_in[h] * q_c[h] for h in hr]
    for n, (d, h) in enumerate(items):
        c_refs[n][...] = decay[n] * c_old[n] + kv[n]
        m_refs[n][...] = jnp.broadcast_to(m_new[n], m_refs[n].shape)
        den = numden[n][:, DV_M:DV_M + 1]
        o_refs[d][:, h * DV_M:(h + 1) * DV_M] = (numden[n][:, :DV_M]
                                                 / jnp.maximum(jnp.abs(den), jnp.exp(-m_t[n])))


def mlstm_scan(p, g_dir, gt_dir, cos_t, sin_t, dims):
    ncx = dims.ctx // CHUNK
    nlx = dims.seq // CHUNK
    x_blocks = dims.m_x // CHUNK

    def row_block(b, d, t):
        in_ctx = t < ncx
        cc = jnp.where(d == 0, t, ncx - 1 - t)
        cx = jnp.where(d == 0, t - ncx, nlx - 1 - (t - ncx))
        return jnp.where(in_ctx, x_blocks + b * ncx + cc, b * nlx + cx)

    def rope_block(b, d, t):
        in_ctx = t < ncx
        cx = jnp.where(d == 0, t - ncx, nlx - 1 - (t - ncx))
        return jnp.where(in_ctx, nlx, cx)

    def dir_specs(d):
        return [
            pl.BlockSpec((CHUNK, H_M * DK_M), lambda b, t: (row_block(b, d, t), PM_Q // (H_M * DK_M))),
            pl.BlockSpec((CHUNK, H_M * DK_M), lambda b, t: (row_block(b, d, t), PM_K // (H_M * DK_M))),
            pl.BlockSpec((CHUNK, H_M * DV_M), lambda b, t: (row_block(b, d, t), PM_V // (H_M * DV_M))),
            pl.BlockSpec((1, CHUNK, LANES), lambda b, t: (d, row_block(b, d, t), 0)),
            pl.BlockSpec((1, 8, CHUNK), lambda b, t: (d, 0, row_block(b, d, t))),
            pl.BlockSpec((CHUNK, DK_M), lambda b, t: (rope_block(b, d, t), 0)),
            pl.BlockSpec((CHUNK, DK_M), lambda b, t: (rope_block(b, d, t), 0)),
        ]

    h_shape = jax.ShapeDtypeStruct((dims.m_all, H_M * DV_M), F32)
    dir_args = (p, p, p, g_dir, gt_dir, cos_t, sin_t)
    return pl.pallas_call(
        _mlstm_kernel,
        out_shape=[h_shape, h_shape],
        grid=(dims.batch, ncx + nlx),
        in_specs=dir_specs(0) + dir_specs(1),
        out_specs=[pl.BlockSpec((CHUNK, H_M * DV_M), lambda b, t: (row_block(b, 0, t), 0)),
                   pl.BlockSpec((CHUNK, H_M * DV_M), lambda b, t: (row_block(b, 1, t), 0))],
        scratch_shapes=([pltpu.VMEM((DK_M, DV_EXT), F32)] * (2 * H_M)
                        + [pltpu.VMEM((8, LANES), F32)] * (2 * H_M)),
        compiler_params=_cparams(("arbitrary", "arbitrary")), name="mlstm_scan",
    )(*dir_args, *dir_args)


def _mlstm_out_kernel(hf_ref, hb_ref, o_ref, g_ref, out_ref):
    for h in range(H_M):
        sl = slice(h * DV_M, (h + 1) * DV_M)
        y = _rms(hf_ref[:, sl] + hb_ref[:, sl], g_ref[:, sl])
        out_ref[:, sl] = (y * jax.nn.sigmoid(o_ref[:, sl])).astype(out_ref.dtype)


def mlstm_out(hf, hb, p, g, dims):
    m = dims.m_all
    tm = 256
    n = H_M * DV_M
    return pl.pallas_call(
        _mlstm_out_kernel,
        out_shape=jax.ShapeDtypeStruct((m, n), BF16),
        grid=(m // tm,),
        in_specs=[pl.BlockSpec((tm, n), lambda i: (i, 0)),
                  pl.BlockSpec((tm, n), lambda i: (i, 0)),
                  pl.BlockSpec((tm, n), lambda i: (i, PM_O // n)),
                  pl.BlockSpec((1, n), lambda i: (0, 0))],
        out_specs=pl.BlockSpec((tm, n), lambda i: (i, 0)),
        compiler_params=_cparams(("arbitrary",)), name="mlstm_out",
    )(hf, hb, p, g.reshape(1, n))


def rope_tables(dims):
    t = jnp.arange(dims.seq)
    row = (t // GRID_W).astype(F32)
    col = (t % GRID_W).astype(F32)
    quarter = DK_M // 4
    inv = ROPE_BASE ** (-jnp.arange(quarter, dtype=F32) / quarter)
    ang = jnp.concatenate([row[:, None] * inv, col[:, None] * inv], axis=-1)
    cos = jnp.repeat(jnp.cos(ang), 2, axis=-1)
    sin = jnp.repeat(jnp.sin(ang), 2, axis=-1) * jnp.tile(jnp.array([-1.0, 1.0], F32), DK_M // 2)
    cos = jnp.concatenate([cos, jnp.ones((CHUNK, DK_M), F32)], axis=0)
    sin = jnp.concatenate([sin, jnp.zeros((CHUNK, DK_M), F32)], axis=0)
    return cos, sin


def _gather_kernel(src_ref, nrows_ref, h_hbm, o_ref, buf_ref, sem, *, tm):
    i = pl.program_id(0)
    n = nrows_ref[i]

    @pl.when(n < tm)
    def _():
        buf_ref[...] = jnp.zeros(buf_ref.shape, buf_ref.dtype)

    def start(r, carry):
        pltpu.make_async_copy(h_hbm.at[pl.ds(src_ref[i * tm + r], 1)], buf_ref.at[pl.ds(r, 1)], sem).start()
        return carry

    def wait(r, carry):
        pltpu.make_async_copy(h_hbm.at[pl.ds(0, 1)], buf_ref.at[pl.ds(r, 1)], sem).wait()
        return carry

    def start_pair(rp, carry):
        for prio in range(2):
            r = 2 * rp + prio
            pltpu.make_async_copy(h_hbm.at[pl.ds(src_ref[i * tm + r], 1)], buf_ref.at[pl.ds(r, 1)],
                                  sem).start(priority=prio)
        return carry

    @pl.when(n == tm)
    def _():
        lax.fori_loop(0, tm // 2, start_pair, 0, unroll=4)
        pltpu.make_async_copy(h_hbm.at[pl.ds(0, tm)], buf_ref, sem).wait()

    @pl.when(n < tm)
    def _():
        lax.fori_loop(0, n, start, 0)
        lax.fori_loop(0, n, wait, 0)

    o_ref[...] = buf_ref[...].astype(o_ref.dtype)


def gather_rows(h, src, tile_rows, *, tm):
    m, d = h.shape
    r_total = src.shape[0]
    return pl.pallas_call(
        functools.partial(_gather_kernel, tm=tm),
        out_shape=jax.ShapeDtypeStruct((r_total, d), BF16),
        grid_spec=pltpu.PrefetchScalarGridSpec(
            num_scalar_prefetch=2,
            grid=(r_total // tm,),
            in_specs=[pl.BlockSpec(memory_space=pl.ANY)],
            out_specs=pl.BlockSpec((tm, d), lambda i, *s: (i, 0)),
            scratch_shapes=[pltpu.VMEM((tm, d), F32), pltpu.SemaphoreType.DMA],
        ),
        compiler_params=_cparams(("arbitrary",)), name="moe_gather",
    )(src, tile_rows, h)


def _combine_kernel(p1_ref, p2_ref, y_hbm, x_ref, gate_ref, rt_ref, o_ref, y1_ref, y2_ref, sem,
                    *, tm, seq, n_groups):
    i = pl.program_id(0)

    def start(r, carry):
        t = i * tm + r
        pltpu.make_async_copy(y_hbm.at[pl.ds(p1_ref[t], 1)], y1_ref.at[pl.ds(r, 1)], sem.at[0]).start(priority=0)
        pltpu.make_async_copy(y_hbm.at[pl.ds(p2_ref[t], 1)], y2_ref.at[pl.ds(r, 1)], sem.at[1]).start(priority=1)
        return carry

    lax.fori_loop(0, tm, start, 0, unroll=8)
    pltpu.make_async_copy(y_hbm.at[pl.ds(0, tm)], y1_ref, sem.at[0]).wait()
    pltpu.make_async_copy(y_hbm.at[pl.ds(0, tm)], y2_ref, sem.at[1]).wait()
    row0 = i * tm
    grp = jnp.int32(n_groups - 1)
    for b in reversed(range(n_groups - 1)):
        grp = jnp.where(row0 < (b + 1) * seq, b, grp)
    gate = gate_ref[pl.ds(grp, 1), :]
    w1 = rt_ref[:, 2:3]
    w2 = rt_ref[:, 3:4]
    o_ref[...] = x_ref[...] + gate * (w1 * y1_ref[...] + w2 * y2_ref[...])


def moe_combine(xs, y, pos1, pos2, route_t, mod, layer, gate_chunk, dims, *, m):
    d = dims.d_model
    tm = _pick_tile(math.gcd(dims.seq, dims.ctx), (256, 128))
    kern = functools.partial(_combine_kernel, tm=tm, seq=dims.seq, n_groups=dims.batch + 1)
    return pl.pallas_call(
        kern,
        out_shape=jax.ShapeDtypeStruct((m, d), F32),
        grid_spec=pltpu.PrefetchScalarGridSpec(
            num_scalar_prefetch=2,
            grid=(m // tm,),
            in_specs=[pl.BlockSpec(memory_space=pl.ANY),
                      pl.BlockSpec((tm, d), lambda i, a, b: (i, 0)),
                      pl.BlockSpec((8, d), lambda i, a, b: (MOD_ROWS // 8 * layer, gate_chunk)),
                      pl.BlockSpec((tm, 8), lambda i, a, b: (i, 0))],
            out_specs=pl.BlockSpec((tm, d), lambda i, a, b: (i, 0)),
            scratch_shapes=[pltpu.VMEM((tm, d), F32), pltpu.VMEM((tm, d), F32),
                            pltpu.SemaphoreType.DMA((2,))],
        ),
        compiler_params=_cparams(("arbitrary",)), name="moe_combine",
    )(pos1, pos2, y, xs, mod, route_t)


def moe_dispatch_plan(route, m, tm):
    e = jnp.concatenate([route[0, :m], route[1, :m]]).astype(I32)
    tok = jnp.tile(jnp.arange(m, dtype=I32), 2)
    onehot = (e[:, None] == jnp.arange(N_EXPERTS, dtype=I32)[None, :]).astype(I32)
    rank = jnp.sum((jnp.cumsum(onehot, axis=0) - onehot) * onehot, axis=1)
    counts = jnp.sum(onehot, axis=0)
    padded = ((counts + tm - 1) // tm) * tm
    ends = jnp.cumsum(padded)
    offs = ends - padded
    pos = offs[e] + rank
    r_total = TOP_K * m + N_EXPERTS * tm
    src = jnp.zeros((r_total,), I32).at[pos].set(tok)
    tile_start = jnp.arange(r_total // tm, dtype=I32) * tm
    last_e = jnp.max(jnp.where(counts > 0, jnp.arange(N_EXPERTS, dtype=I32), 0))
    tile_e = jnp.minimum(jnp.sum((tile_start[:, None] >= ends[None, :]).astype(I32), axis=1), last_e)
    tile_rows = jnp.clip((offs + counts)[tile_e] - tile_start, 0, tm).astype(I32)
    return src, pos[:m], pos[m:], tile_e, tile_rows


def _dense_tile(m, candidates):
    return _pick_tile(m, candidates)


def kernel(x, c, ctx, c_ctx, ada_w, ada_b, norm1_g, norm2_g, w_in, gate_b, na_q_g, na_k_g, na_rpb, m_norm_g, w_out, ffn_w_gate, ffn_w_up, ffn_w_down, moe_router, moe_w_gate, moe_w_up, moe_w_down):
    batch, seq, d = x.shape
    dims = Dims(batch=batch, seq=seq, ctx=ctx.shape[1], d_model=d, d_ff=ffn_w_gate.shape[2], depth=ada_w.shape[0])
    depth = dims.depth
    n_groups = batch + 1
    rows = seq // GRID_W
    assert rows >= 4 * NA_QROWS and rows % NA_QROWS == 0 and batch + 1 <= 8
    assert seq % 256 == 0 and dims.ctx % CHUNK == 0 and dims.m_x % dims.ctx == 0

    xs = jnp.concatenate([x.reshape(dims.m_x, d), ctx.reshape(batch * dims.ctx, d)], axis=0)

    cond = jnp.concatenate([jax.nn.silu(c), jax.nn.silu(c_ctx)[None], jnp.zeros((MOD_ROWS - n_groups, d), F32)], axis=0)
    tn_ada = _pick_tile(6 * d, (1024, 512, 256))
    tn_d = min(512, d)
    mod = gmm((cond.astype(BF16),), (ada_w,), jnp.arange(depth, dtype=I32), jnp.full((depth,), MOD_ROWS, I32),
              tm=MOD_ROWS, tn=tn_ada, n_out=6 * d, out_dtype=F32, epilogue="bias",
              extras=(ada_b.reshape(depth, 1, 6 * d),),
              extra_specs=(((1, 1, tn_ada), lambda j, i, g, *s: (g[i], 0, j)),),
              a_fixed=True, rows_outer=True, name="ada_mod")

    cos_t, sin_t = rope_tables(dims)
    na_bias = na_bias_table(na_rpb, rows)
    w_gate_cols = jnp.pad(w_in[:, :, N_MAIN:], ((0, 0), (0, 0), (0, LANES - 4 * H_M)))
    gate_b_pad = jnp.pad(gate_b, ((0, 0), (0, LANES - 4 * H_M)))

    def dense_groups(m, tm, g):
        nt = m // tm
        return jnp.full((nt,), g, I32), jnp.full((nt,), tm, I32)

    for layer in range(depth):
        last = layer == depth - 1
        m_out = dims.m_x if last else dims.m_all

        hbf = norm_mod(xs, norm1_g[layer], mod, layer, 0, 1, dims, m=dims.m_all)
        qk, vbf, pm, gates = in_proj(hbf, w_in, w_gate_cols, gate_b_pad, layer, na_q_g[layer], na_k_g[layer])
        g_dir = jnp.stack([gates, jnp.roll(gates, -2 * H_M, axis=1)])
        gt_dir = jnp.stack([gates[:, :2 * H_M].T, gates[:, 2 * H_M:4 * H_M].T])

        na_out = na_attention(qk, vbf, na_bias, layer, dims)
        hf, hb = mlstm_scan(pm, g_dir, gt_dir, cos_t, sin_t, dims)
        m_mix = mlstm_out(hf, hb, pm, m_norm_g[layer], dims)
        if not last:
            na_out = jnp.concatenate([na_out, ctx_attention(qk, vbf, dims)], axis=0)

        tm_o = _dense_tile(m_out, (1088, 1024, 640, 512, 256))
        grp, nv = dense_groups(m_out, tm_o, layer)
        res_specs = lambda tm, tn, chunk: (
            ((tm, tn), lambda j, i, *s: (i, j)),
            ((8, tn), lambda j, i, *s: (MOD_ROWS // 8 * layer, chunk * (d // tn) + j)))
        tn_o = min(1024, d)
        xs = gmm((na_out, m_mix), (w_out,), grp, nv, tm=tm_o, tn=tn_o, n_out=d, out_dtype=F32, epilogue="residual",
                 extras=(xs, mod), extra_specs=res_specs(tm_o, tn_o, 2), seq=seq, n_groups=n_groups,
                 name="w_out_proj")

        idx = layer // 2
        if layer % 2 == 0:
            h2 = norm_mod(xs, norm2_g[layer], mod, layer, 3, 4, dims, m=m_out)
            tm_u = _dense_tile(m_out, (1088, 1024, 640, 512, 256))
            grp, nv = dense_groups(m_out, tm_u, idx)
            act = gmm_swiglu(h2, ffn_w_gate, ffn_w_up, grp, nv, tm=tm_u, tn=512, name="ffn_up")
            tm_d = _dense_tile(m_out, (544, 512, 320, 256))
            grp, nv = dense_groups(m_out, tm_d, idx)
            xs = gmm((act,), (ffn_w_down,), grp, nv, tm=tm_d, tn=tn_d, n_out=d, out_dtype=F32, epilogue="residual",
                     extras=(xs, mod), extra_specs=res_specs(tm_d, tn_d, 5), seq=seq, n_groups=n_groups,
                     name="ffn_down")
        else:
            h2, route = norm_mod(xs, norm2_g[layer], mod, layer, 3, 4, dims, m=m_out,
                                 router_t=moe_router[idx].T)
            tm_e = 512 if m_out >= 4096 else 128
            src, pos1, pos2, tile_e, tile_rows = moe_dispatch_plan(route, m_out, tm_e)
            hs = gather_rows(h2, src, tile_rows, tm=tm_e)
            grp = tile_e + idx * N_EXPERTS
            nw = moe_w_gate.shape[0] * N_EXPERTS
            act = gmm_swiglu(hs, moe_w_gate.reshape(nw, d, dims.d_ff), moe_w_up.reshape(nw, d, dims.d_ff),
                             grp, tile_rows, tm=tm_e, tn=512, half_tiles=True, name="moe_up")
            y = gmm((act,), (moe_w_down.reshape(nw, dims.d_ff, d),), grp, tile_rows, tm=tm_e, tn=tn_d, n_out=d,
                    out_dtype=F32, half_tiles=True, name="moe_down")
            xs = moe_combine(xs, y, pos1, pos2, route.T, mod, layer, 5, dims, m=m_out)

    return xs[:dims.m_x].reshape(batch, seq, d)
```
